```python
import math
import jax, jax.numpy as jnp
from jax import lax
import numpy as np

D_MODEL = 1024
BATCH = 8
SEQ = 8192
DEPTH = 4

CTX_LEN = 256
GRID_W = 64
EPS = 1e-6
NEG_INF = -1e30

GLA_HEADS = 6
GLA_DK = 32
GLA_DV = 64
GLA_LOWRANK = 16
GLA_TAU = 16.0
GLA_CHUNK = 16
ROPE_BASE = 10000.0
SC_WIDTH = 256
SC_GROUPS = 4
NA_HEADS = 6
NA_DH = 64
NA_WIN_ROWS = 8
NA_WIN_COLS = 16
FFN_DIM = 2816

GLA_QK = GLA_HEADS * GLA_DK
GLA_V = GLA_HEADS * GLA_DV
NA_W = NA_HEADS * NA_DH
MIX_WIDTH = GLA_V + SC_WIDTH + NA_W
IN_SPLIT = (GLA_QK, GLA_QK, GLA_V, GLA_LOWRANK, GLA_LOWRANK, GLA_V,
            SC_WIDTH, SC_WIDTH, SC_WIDTH, NA_W, NA_W, NA_W)
IN_WIDTH = 2 * GLA_QK + 2 * GLA_V + 2 * GLA_LOWRANK + 3 * SC_WIDTH + 3 * NA_W

kernel_name = "hybrid_gla_shortconv_natten_dit"


def rmsnorm(t, g):
    tf = t.astype(jnp.float32)
    y = tf * lax.rsqrt(jnp.mean(tf * tf, axis=-1, keepdims=True) + EPS)
    return (y * g.astype(jnp.float32)).astype(t.dtype)


def modulate(t, shift, scale):
    return t * (1 + scale) + shift


def split_cols(u):
    parts, start = [], 0
    for size in IN_SPLIT:
        parts.append(u[..., start:start + size])
        start += size
    return parts


def dwconv3(t, w, b):
    tp = jnp.pad(t, ((0, 0), (1, 1), (0, 0)))
    return tp[:, :-2] * w[0] + tp[:, 1:-1] * w[1] + tp[:, 2:] * w[2] + b


def axial_rope(n_tokens, dim):
    t = jnp.arange(n_tokens)
    n_freq = dim // 4
    inv_freq = ROPE_BASE ** (-jnp.arange(n_freq, dtype=jnp.float32) / n_freq)
    row = (t // GRID_W).astype(jnp.float32)[:, None] * inv_freq
    col = (t % GRID_W).astype(jnp.float32)[:, None] * inv_freq
    ang = jnp.concatenate([row, col], axis=-1)
    return jnp.cos(ang)[None, :, None, :], jnp.sin(ang)[None, :, None, :]


def apply_rope(t, cos, sin):
    t1, t2 = t[..., 0::2], t[..., 1::2]
    return jnp.stack([t1 * cos - t2 * sin, t1 * sin + t2 * cos], axis=-1).reshape(t.shape)


def gla_chunked(q, k, v, g, s0):
    bsz, n_tok, h, dk = q.shape
    dv = v.shape[-1]
    n_chunks = n_tok // GLA_CHUNK
    rs = lambda t: t.reshape(bsz, n_chunks, GLA_CHUNK, h, t.shape[-1])
    q, k, v, g = rs(q), rs(k), rs(v), rs(g)
    b = jnp.cumsum(g, axis=2)
    b_last = b[:, :, -1:]
    q_dec = q * jnp.exp(b)
    k_inv = k * jnp.exp(-b)
    k_end = k * jnp.exp(b_last - b)
    causal = jnp.tril(jnp.ones((GLA_CHUNK, GLA_CHUNK), dtype=bool))
    a = jnp.einsum('bnihk,bnjhk->bnhij', q_dec, k_inv)
    a = jnp.where(causal, a, 0.0)
    o_intra = jnp.einsum('bnhij,bnjhv->bnihv', a, v)

    def step(s, xs):
        qd, ke, vv, dec = xs
        o = jnp.einsum('bihk,bhkv->bihv', qd, s)
        s = s * dec[..., None] + jnp.einsum('bjhk,bjhv->bhkv', ke, vv)
        return s, o

    xs = (jnp.moveaxis(q_dec, 1, 0), jnp.moveaxis(k_end, 1, 0), jnp.moveaxis(v, 1, 0),
          jnp.moveaxis(jnp.exp(b_last[:, :, 0]), 1, 0))
    s_final, o_inter = lax.scan(step, s0, xs)
    o = o_intra + jnp.moveaxis(o_inter, 0, 1)
    return o.reshape(bsz, n_tok, h, dv), s_final


def gla_prep(q, k, v, glr_f, glr_b, wg2_fw, bg_fw, wg2_bw, bg_bw, rope):
    bsz, n_tok, _ = q.shape
    heads = lambda t, d: t.astype(jnp.float32).reshape(bsz, n_tok, GLA_HEADS, d)
    q = heads(q, GLA_DK) * (GLA_DK ** -0.5)
    k = heads(k, GLA_DK)
    v = heads(v, GLA_DV)
    if rope is not None:
        q = apply_rope(q, rope[0], rope[1])
        k = apply_rope(k, rope[0], rope[1])
    g_f = heads(jax.nn.log_sigmoid((glr_f @ wg2_fw + bg_fw).astype(jnp.float32)), GLA_DK) / GLA_TAU
    g_b = heads(jax.nn.log_sigmoid((glr_b @ wg2_bw + bg_bw).astype(jnp.float32)), GLA_DK) / GLA_TAU
    return q, k, v, g_f, g_b


def gla_output(o, r, norm_g):
    of = o * lax.rsqrt(jnp.mean(o * o, axis=-1, keepdims=True) + EPS) * norm_g.astype(jnp.float32)
    bsz, n_tok = o.shape[:2]
    return (of.reshape(bsz, n_tok, GLA_V) * jax.nn.silu(r.astype(jnp.float32))).astype(r.dtype)


def gla_mixer(lat, cpar, rope, wg2_fw, bg_fw, wg2_bw, bg_bw, norm_g, need_ctx_out):
    flip = lambda t: jnp.flip(t, axis=1)
    qc, kc, vc, gfc, gbc = gla_prep(*cpar[:5], wg2_fw, bg_fw, wg2_bw, bg_bw, None)
    ql, kl, vl, gfl, gbl = gla_prep(*lat[:5], wg2_fw, bg_fw, wg2_bw, bg_bw, rope)
    s0 = jnp.zeros((qc.shape[0], GLA_HEADS, GLA_DK, GLA_DV), jnp.float32)
    oc_f, sc_f = gla_chunked(qc, kc, vc, gfc, s0)
    oc_b, sc_b = gla_chunked(flip(qc), flip(kc), flip(vc), flip(gbc), s0)
    ol_f, _ = gla_chunked(ql, kl, vl, gfl, sc_f)
    ol_b, _ = gla_chunked(flip(ql), flip(kl), flip(vl), flip(gbl), sc_b)
    out_lat = gla_output(ol_f + flip(ol_b), lat[5], norm_g)
    out_ctx = gla_output(oc_f + flip(oc_b), cpar[5], norm_g) if need_ctx_out else None
    return out_lat, out_ctx


def short_conv(b_gate, c_gate, xh, w, bias):
    return b_gate * dwconv3(c_gate * xh, w, bias)


def neighbourhood_attention(q, k, v, k_ctx, v_ctx, rpb):
    bsz, n_tok, _ = q.shape
    rows = n_tok // GRID_W
    kh = min(NA_WIN_ROWS, rows)
    kw = NA_WIN_COLS
    grid = lambda t: t.reshape(bsz, rows, GRID_W, NA_HEADS, NA_DH)
    qg = grid(q.astype(jnp.float32) * (NA_DH ** -0.5))
    kg, vg = grid(k), grid(v)
    kc = k_ctx.reshape(bsz, -1, NA_HEADS, NA_DH).astype(jnp.float32)
    vc = v_ctx.reshape(bsz, -1, NA_HEADS, NA_DH).astype(jnp.float32)
    col = jnp.arange(GRID_W)
    col_start = jnp.clip(col - kw // 2, 0, GRID_W - kw)
    col_mask = (col[None, :] >= col_start[:, None]) & (col[None, :] < col_start[:, None] + kw)
    dc_idx = jnp.clip(col[None, :] - col[:, None] + NA_WIN_COLS - 1, 0, 2 * NA_WIN_COLS - 2)
    rpb_cols = jnp.take(rpb.astype(jnp.float32), dc_idx, axis=2)

    def row_block(r):
        rs = jnp.clip(r - kh // 2, 0, rows - kh)
        q_r = lax.dynamic_index_in_dim(qg, r, axis=1, keepdims=False)
        k_band = lax.dynamic_slice_in_dim(kg, rs, kh, axis=1).astype(jnp.float32)
        v_band = lax.dynamic_slice_in_dim(vg, rs, kh, axis=1).astype(jnp.float32)
        dr_idx = rs + jnp.arange(kh) - r + NA_WIN_ROWS - 1
        bias = jnp.take(rpb_cols, dr_idx, axis=1)
        s_loc = jnp.einsum('bqhd,bikhd->bhqik', q_r, k_band) + jnp.transpose(bias, (0, 2, 1, 3))[None]
        s_loc = jnp.where(col_mask[None, None, :, None, :], s_loc, NEG_INF)
        s_ctx = jnp.einsum('bqhd,bchd->bhqc', q_r, kc)
        s = jnp.concatenate([s_loc.reshape(bsz, NA_HEADS, GRID_W, kh * GRID_W), s_ctx], axis=-1)
        p = jax.nn.softmax(s, axis=-1)
        p_loc = p[..., :kh * GRID_W].reshape(bsz, NA_HEADS, GRID_W, kh, GRID_W)
        p_ctx = p[..., kh * GRID_W:]
        o = jnp.einsum('bhqik,bikhd->bqhd', p_loc, v_band) + jnp.einsum('bhqc,bchd->bqhd', p_ctx, vc)
        return o.astype(q.dtype)

    out = lax.map(row_block, jnp.arange(rows))
    return jnp.moveaxis(out, 0, 1).reshape(bsz, n_tok, NA_W)


def context_attention(q, k, v):
    bsz, n_tok, _ = q.shape
    hs = lambda t: t.astype(jnp.float32).reshape(bsz, n_tok, NA_HEADS, NA_DH)
    s = jnp.einsum('bqhd,bkhd->bhqk', hs(q) * (NA_DH ** -0.5), hs(k))
    o = jnp.einsum('bhqk,bkhd->bqhd', jax.nn.softmax(s, axis=-1), hs(v))
    return o.reshape(bsz, n_tok, NA_W).astype(q.dtype)


def conv_ffn(h, w_up, cw, cb, w_down):
    u = dwconv3(h @ w_up, cw, cb)
    a, b = jnp.split(u, 2, axis=-1)
    return (jax.nn.silu(a) * b) @ w_down


def setup_inputs(seed: int = 0) -> dict:
    key = jax.random.key(seed)
    ks = jax.random.split(key, 24)
    nrm = lambda k, shape, s: jax.random.normal(k, shape, jnp.float32) * s
    return {
        "x": nrm(ks[0], (BATCH, SEQ, D_MODEL), 1.0),
        "c": nrm(ks[1], (BATCH, D_MODEL), 1.0),
        "ctx": nrm(ks[2], (BATCH, CTX_LEN, D_MODEL), 1.0),
        "c_ctx": nrm(ks[3], (D_MODEL,), 1.0),
        "w_ada": nrm(ks[4], (DEPTH, D_MODEL, 6 * D_MODEL), 0.5 * D_MODEL ** -0.5),
        "b_ada": nrm(ks[5], (DEPTH, 6 * D_MODEL), 0.02),
        "norm_mix_g": 1.0 + nrm(ks[6], (DEPTH, D_MODEL), 0.05),
        "norm_ffn_g": 1.0 + nrm(ks[7], (DEPTH, D_MODEL), 0.05),
        "w_in": nrm(ks[8], (DEPTH, D_MODEL, IN_WIDTH), D_MODEL ** -0.5),
        "gla_wg2_fw": nrm(ks[9], (DEPTH, GLA_LOWRANK, GLA_QK), GLA_LOWRANK ** -0.5),
        "gla_bg_fw": nrm(ks[10], (DEPTH, GLA_QK), 0.02),
        "gla_wg2_bw": nrm(ks[11], (DEPTH, GLA_LOWRANK, GLA_QK), GLA_LOWRANK ** -0.5),
        "gla_bg_bw": nrm(ks[12], (DEPTH, GLA_QK), 0.02),
        "gla_norm_g": 1.0 + nrm(ks[13], (DEPTH, GLA_DV), 0.05),
        "sc_conv_w": nrm(ks[14], (DEPTH, 3, SC_WIDTH), 3 ** -0.5),
        "sc_conv_b": nrm(ks[15], (DEPTH, SC_WIDTH), 0.02),
        "na_rpb": nrm(ks[16], (DEPTH, NA_HEADS, 2 * NA_WIN_ROWS - 1, 2 * NA_WIN_COLS - 1), 0.02),
        "w_out": nrm(ks[17], (DEPTH, MIX_WIDTH, D_MODEL), MIX_WIDTH ** -0.5),
        "ffn_w_up": nrm(ks[18], (DEPTH, D_MODEL, 2 * FFN_DIM), D_MODEL ** -0.5),
        "ffn_conv_w": nrm(ks[19], (DEPTH, 3, 2 * FFN_DIM), 3 ** -0.5),
        "ffn_conv_b": nrm(ks[20], (DEPTH, 2 * FFN_DIM), 0.02),
        "ffn_w_down": nrm(ks[21], (DEPTH, FFN_DIM, D_MODEL), FFN_DIM ** -0.5),
        "final_norm_g": 1.0 + nrm(ks[22], (D_MODEL,), 0.05),
    }


def reference(x, c, ctx, c_ctx, w_ada, b_ada, norm_mix_g, norm_ffn_g, w_in, gla_wg2_fw, gla_bg_fw,
              gla_wg2_bw, gla_bg_bw, gla_norm_g, sc_conv_w, sc_conv_b, na_rpb, w_out, ffn_w_up,
              ffn_conv_w, ffn_conv_b, ffn_w_down, final_norm_g):
    n_lat = x.shape[1]
    rope = axial_rope(n_lat, GLA_DK)
    xc = ctx
    for layer in range(DEPTH):
        update_ctx = layer < DEPTH - 1
        mod = jax.nn.silu(c) @ w_ada[layer] + b_ada[layer]
        sh1, sc1, g1, sh2, sc2, g2 = jnp.split(mod[:, None, :], 6, axis=-1)
        modc = jax.nn.silu(c_ctx) @ w_ada[layer] + b_ada[layer]
        sh1c, sc1c, g1c, sh2c, sc2c, g2c = jnp.split(modc, 6, axis=-1)

        u = modulate(rmsnorm(x, norm_mix_g[layer]), sh1, sc1) @ w_in[layer]
        uc = modulate(rmsnorm(xc, norm_mix_g[layer]), sh1c, sc1c) @ w_in[layer]
        lat = split_cols(u)
        cpar = split_cols(uc)
        gla_lat, gla_ctx = gla_mixer(lat[:6], cpar[:6], rope, gla_wg2_fw[layer], gla_bg_fw[layer],
                                     gla_wg2_bw[layer], gla_bg_bw[layer], gla_norm_g[layer], update_ctx)
        sc_lat = short_conv(lat[6], lat[7], lat[8], sc_conv_w[layer], sc_conv_b[layer])
        na_lat = neighbourhood_attention(lat[9], lat[10], lat[11], cpar[10], cpar[11], na_rpb[layer])
        x = x + g1 * (jnp.concatenate([gla_lat, sc_lat, na_lat], axis=-1) @ w_out[layer])
        h = modulate(rmsnorm(x, norm_ffn_g[layer]), sh2, sc2)
        x = x + g2 * conv_ffn(h, ffn_w_up[layer], ffn_conv_w[layer], ffn_conv_b[layer], ffn_w_down[layer])

        if update_ctx:
            sc_ctx = short_conv(cpar[6], cpar[7], cpar[8], sc_conv_w[layer], sc_conv_b[layer])
            na_ctx = context_attention(cpar[9], cpar[10], cpar[11])
            xc = xc + g1c * (jnp.concatenate([gla_ctx, sc_ctx, na_ctx], axis=-1) @ w_out[layer])
            hc = modulate(rmsnorm(xc, norm_ffn_g[layer]), sh2c, sc2c)
            xc = xc + g2c * conv_ffn(hc, ffn_w_up[layer], ffn_conv_w[layer], ffn_conv_b[layer],
                                     ffn_w_down[layer])
    return rmsnorm(x, final_norm_g)
```

```python
import functools

import numpy as np
import jax
import jax.numpy as jnp
from jax import lax
from jax.experimental import pallas as pl
from jax.experimental.pallas import tpu as pltpu

F32 = jnp.float32
BF16 = jnp.bfloat16

GRID_W = 64
EPS = 1e-6
NEG_INF = -1e30
GLA_HEADS = 6
GLA_DK = 32
GLA_DV = 64
GLA_LOWRANK = 16
GLA_TAU = 16.0
ROPE_BASE = 10000.0
SC_WIDTH = 256
NA_HEADS = 6
NA_DH = 64
NA_WIN_ROWS = 8
NA_WIN_COLS = 16
GLA_QK = GLA_HEADS * GLA_DK
GLA_V = GLA_HEADS * GLA_DV
NA_W = NA_HEADS * NA_DH

LANES = 128
SUBLANES = 8
VMEM_LIMIT_BYTES = 56 * 1024 * 1024

QK_PAD = 2 * LANES
GLR_PAD = LANES
GLA_GROUP = 128
GLA_LEVELS = 4
NA_Q_ROWS = 4
NA_K_ROWS = 12

NT_DIMS = (((1,), (1,)), ((), ()))


def _cparams(*sem):
    return pltpu.CompilerParams(dimension_semantics=sem, vmem_limit_bytes=VMEM_LIMIT_BYTES)


def _silu(t):
    return t / (1.0 + jnp.exp(-t))


def _norm_mod(t, gain, shift, scale):
    ms = jnp.mean(t * t, axis=-1, keepdims=True)
    return (t * lax.rsqrt(ms + EPS) * gain) * (1.0 + scale) + shift


def _ada_kernel(c_ref, w_ref, b_ref, o_ref):
    c = c_ref[...]
    o_ref[0] = jnp.dot(_silu(c), w_ref[0], preferred_element_type=F32,
                       precision=lax.Precision.HIGHEST) + b_ref[0]


def _ada_call(crows, w_ada, b_ada):
    depth, d, n = w_ada.shape
    rows = crows.shape[0]
    tn = 1536
    return pl.pallas_call(
        _ada_kernel,
        grid=(depth, n // tn),
        in_specs=[pl.BlockSpec((rows, d), lambda l, j: (0, 0)),
                  pl.BlockSpec((1, d, tn), lambda l, j: (l, 0, j)),
                  pl.BlockSpec((1, 1, tn), lambda l, j: (l, 0, j))],
        out_specs=pl.BlockSpec((1, rows, tn), lambda l, j: (l, 0, j)),
        out_shape=jax.ShapeDtypeStruct((depth, rows, n), F32),
        compiler_params=_cparams("arbitrary", "arbitrary"),
        name="ada_mod",
    )(crows, w_ada, b_ada.reshape(depth, 1, n))


def _inproj_kernel(*refs, rope):
    if rope:
        (x_ref, sh_ref, sc_ref, g_ref, cos_ref, sin_ref, wqk_ref, wvr_ref, wsc_ref, wna_ref,
         q_ref, k_ref, v_ref, r_ref, glr_ref, scu_ref, naq_ref, nak_ref, nav_ref) = refs
    else:
        (x_ref, sh_ref, sc_ref, g_ref, wqk_ref, wvr_ref, wsc_ref, wna_ref,
         q_ref, k_ref, v_ref, r_ref, glr_ref, scu_ref, naq_ref, nak_ref, nav_ref) = refs
    hb = _norm_mod(x_ref[0], g_ref[...], sh_ref[0], sc_ref[0]).astype(BF16)
    qk = jnp.dot(hb, wqk_ref[...], preferred_element_type=F32)
    p = QK_PAD
    if rope:
        c = cos_ref[...]
        s = sin_ref[...]
        q = (qk[:, 0:p] * c + qk[:, 2 * p:3 * p] * s) * (GLA_DK ** -0.5)
        k = qk[:, p:2 * p] * c + qk[:, 3 * p:4 * p] * s
    else:
        q = qk[:, 0:p] * (GLA_DK ** -0.5)
        k = qk[:, p:2 * p]
    q_ref[0] = q
    k_ref[0] = k
    vr = jnp.dot(hb, wvr_ref[...], preferred_element_type=F32)
    v_ref[0] = vr[:, 0:GLA_V]
    r_ref[0] = vr[:, GLA_V:2 * GLA_V]
    glr_ref[0] = vr[:, 2 * GLA_V:2 * GLA_V + GLR_PAD]
    scu_ref[0] = jnp.dot(hb, wsc_ref[...], preferred_element_type=F32)
    na = jnp.dot(hb, wna_ref[...], preferred_element_type=F32)
    naq_ref[0] = na[:, 0:NA_W].astype(BF16)
    nak_ref[0] = na[:, NA_W:2 * NA_W].astype(BF16)
    nav_ref[0] = na[:, 2 * NA_W:3 * NA_W].astype(BF16)


def _inproj_call(x, shift, scale, gain, wqk, wvr, wsc, wna, rope_tabs, tm):
    b, l, d = x.shape
    bm = shift.shape[0]
    rope = rope_tabs is not None
    mod_map = (lambda i, bb: (bb, 0, 0)) if bm > 1 else (lambda i, bb: (0, 0, 0))
    tok = lambda w: pl.BlockSpec((1, tm, w), lambda i, bb: (bb, i, 0))
    full = lambda a: pl.BlockSpec(a.shape, lambda i, bb: (0,) * a.ndim)
    in_specs = [tok(d), pl.BlockSpec((1, 1, d), mod_map), pl.BlockSpec((1, 1, d), mod_map), full(gain)]
    args = [x, shift, scale, gain]
    if rope:
        in_specs += [pl.BlockSpec((tm, QK_PAD), lambda i, bb: (i, 0))] * 2
        args += list(rope_tabs)
    in_specs += [full(wqk), full(wvr), full(wsc), full(wna)]
    args += [wqk, wvr, wsc, wna]
    widths = [(QK_PAD, F32), (QK_PAD, F32), (GLA_V, F32), (GLA_V, F32), (GLR_PAD, F32),
              (3 * SC_WIDTH, F32), (NA_W, BF16), (NA_W, BF16), (NA_W, BF16)]
    return pl.pallas_call(
        functools.partial(_inproj_kernel, rope=rope),
        grid=(l // tm, b),
        in_specs=in_specs,
        out_specs=[tok(w) for w, _ in widths],
        out_shape=[jax.ShapeDtypeStruct((b, l, w), dt) for w, dt in widths],
        compiler_params=_cparams("arbitrary", "arbitrary"),
        name="in_proj_rope" if rope else "in_proj_ctx",
    )(*args)


def _gla_tables(reverse):
    t = np.arange(GLA_GROUP)
    i, j = t[:, None], t[None, :]
    before = (j >= i) if reverse else (j <= i)
    strictly_after = (j < i) if reverse else (j > i)
    maps = []
    for n in (16, 32, 64, 128):
        maps.append((i // n == j // n) & before)
    for n in (16, 32, 64, 128):
        maps.append((i // n == j // n) & strictly_after)
    masks = [(i // 16 == j // 16) & before]
    for n in (32, 64, 128):
        h = n // 2
        q_late, k_early = (i % n) >= h, (j % n) < h
        if reverse:
            q_late, k_early = (i % n) < h, (j % n) >= h
        masks.append((i // n == j // n) & q_late & k_early)
    cm = np.concatenate(maps, axis=0).astype(np.float32)
    lm = np.stack(masks, axis=0).astype(np.float32)
    return jnp.asarray(cm, BF16), jnp.asarray(lm, F32)


def _gla_kernel(*refs, reverse, finalize, ngroups):
    if finalize:
        (q_ref, k_ref, v_ref, glr_ref, wg_ref, bg_ref, cm_ref, lm_ref, s0_ref,
         r_ref, of_ref, ng_ref, bd_ref, out_ref, sfin_ref, st_ref) = refs
    else:
        (q_ref, k_ref, v_ref, glr_ref, wg_ref, bg_ref, cm_ref, lm_ref, s0_ref,
         out_ref, sfin_ref, st_ref) = refs
    step = pl.program_id(1)
    g = GLA_GROUP

    @pl.when(step == 0)
    def _():
        st_ref[...] = s0_ref[0]

    lane_qk = lax.broadcasted_iota(jnp.int32, (g, QK_PAD), 1) // GLA_DK
    lane_v = lax.broadcasted_iota(jnp.int32, (g, GLA_V), 1) // GLA_DV
    kmask = [lane_qk == h for h in range(GLA_HEADS)]
    vmask = [lane_v == h for h in range(GLA_HEADS)]
    state_mask = (lax.broadcasted_iota(jnp.int32, (GLA_V, QK_PAD), 0) // GLA_DV
                  == lax.broadcasted_iota(jnp.int32, (GLA_V, QK_PAD), 1) // GLA_DK)
    lmask = [lm_ref[l] > 0.5 for l in range(GLA_LEVELS)]

    def scores(qs, ks):
        kb = jnp.concatenate([jnp.where(kmask[h], ks, 0.0) for h in range(GLA_HEADS)], axis=0)
        return lax.dot_general(qs.astype(BF16), kb.astype(BF16), NT_DIMS, preferred_element_type=F32)

    def group(gi, carry):
        gidx = (ngroups - 1 - gi) if reverse else gi
        rows = pl.ds(pl.multiple_of(gidx * g, g), g)
        q = q_ref[0, rows, :]
        k = k_ref[0, rows, :]
        v = v_ref[0, rows, :]
        z = jnp.dot(glr_ref[0, rows, :].astype(BF16), wg_ref[...], preferred_element_type=F32) + bg_ref[...]
        gate = (jnp.minimum(z, 0.0) - jnp.log(1.0 + jnp.exp(-jnp.abs(z)))) * (1.0 / GLA_TAU)
        ghi = gate.astype(BF16)
        glo = (gate - ghi.astype(F32)).astype(BF16)
        cs = (jnp.dot(cm_ref[...], ghi, preferred_element_type=F32)
              + jnp.dot(cm_ref[...], glo, preferred_element_type=F32))
        p16, p32, p64, p128, s16, s32, s64, s128 = [cs[n * g:(n + 1) * g] for n in range(8)]
        q0 = q * jnp.exp(p16)
        a = [scores(q0, k * jnp.exp(-p16)),
             scores(q0, k * jnp.exp(s16)),
             scores(q * jnp.exp(p32), k * jnp.exp(s32)),
             scores(q * jnp.exp(p64), k * jnp.exp(s64))]
        blocks = []
        for h in range(GLA_HEADS):
            sl = slice(h * g, (h + 1) * g)
            blk = jnp.where(lmask[3], a[3][:, sl], 0.0)
            for l in (2, 1, 0):
                blk = jnp.where(lmask[l], a[l][:, sl], blk)
            blocks.append(blk)
        amat = jnp.concatenate(blocks, axis=1).astype(BF16)
        vb = jnp.concatenate([jnp.where(vmask[h], v, 0.0) for h in range(GLA_HEADS)], axis=0).astype(BF16)
        st = st_ref[...]
        o = jnp.dot(amat, vb, preferred_element_type=F32)
        o = o + lax.dot_general((q * jnp.exp(p128)).astype(BF16), st.astype(BF16), NT_DIMS,
                                preferred_element_type=F32)
        upd = jnp.dot(v.T.astype(BF16), (k * jnp.exp(s128)).astype(BF16), preferred_element_type=F32)
        decay = jnp.exp(p128[0:1] + s128[0:1])
        st_ref[...] = st * decay + jnp.where(state_mask, upd, 0.0)
        if finalize:
            ot = o + of_ref[0, rows, :]
            o2 = ot * ot
            hi = o2.astype(BF16)
            lo = (o2 - hi.astype(F32)).astype(BF16)
            ms = (jnp.dot(hi, bd_ref[...], preferred_element_type=F32)
                  + jnp.dot(lo, bd_ref[...], preferred_element_type=F32))
            y = ot * lax.rsqrt(ms + EPS) * ng_ref[...] * _silu(r_ref[0, rows, :])
            out_ref[0, rows, :] = y.astype(out_ref.dtype)
        else:
            out_ref[0, rows, :] = o
        return carry

    lax.fori_loop(0, ngroups, group, 0)

    @pl.when(step == pl.num_programs(1) - 1)
    def _():
        sfin_ref[0] = st_ref[...]


def _gla_call(q, k, v, glr, wg, bg, tabs, s0, ts, reverse, fin=None):
    b, l, _ = q.shape
    nb = l // ts
    finalize = fin is not None
    cm, lm = tabs
    blk = (lambda bb, i: (bb, nb - 1 - i, 0)) if reverse else (lambda bb, i: (bb, i, 0))
    tok = lambda w: pl.BlockSpec((1, ts, w), blk)
    full = lambda a: pl.BlockSpec(a.shape, lambda bb, i: (0,) * a.ndim)
    state = pl.BlockSpec((1, GLA_V, QK_PAD), lambda bb, i: (bb, 0, 0))
    in_specs = [tok(QK_PAD), tok(QK_PAD), tok(GLA_V), tok(GLR_PAD), full(wg), full(bg), full(cm), full(lm), state]
    args = [q, k, v, glr, wg, bg, cm, lm, s0]
    if finalize:
        r, o_other, ng, bd = fin
        in_specs += [tok(GLA_V), tok(GLA_V), full(ng), full(bd)]
        args += [r, o_other, ng, bd]
    out_dt = BF16 if finalize else F32
    return pl.pallas_call(
        functools.partial(_gla_kernel, reverse=reverse, finalize=finalize, ngroups=ts // GLA_GROUP),
        grid=(b, nb),
        in_specs=in_specs,
        out_specs=[tok(GLA_V), state],
        out_shape=[jax.ShapeDtypeStruct((b, l, GLA_V), out_dt),
                   jax.ShapeDtypeStruct((b, GLA_V, QK_PAD), F32)],
        scratch_shapes=[pltpu.VMEM((GLA_V, QK_PAD), F32)],
        compiler_params=_cparams("arbitrary", "arbitrary"),
        name="gla_bwd" if reverse else "gla_fwd",
    )(*args)


def _na_bias_tables(rpb, rows):
    nq, nk = NA_Q_ROWS * GRID_W, NA_K_ROWS * GRID_W
    qi, kj = np.arange(nq)[:, None], np.arange(nk)[None, :]
    kh = min(NA_WIN_ROWS, rows)
    tabs = []
    for r0, ks in ((0, 0), (NA_WIN_ROWS // 2, 0), (rows - NA_Q_ROWS, rows - NA_K_ROWS)):
        r, c = r0 + qi // GRID_W, qi % GRID_W
        r2, c2 = ks + kj // GRID_W, kj % GRID_W
        rs = np.clip(r - kh // 2, 0, rows - kh)
        cs = np.clip(c - NA_WIN_COLS // 2, 0, GRID_W - NA_WIN_COLS)
        valid = (r2 >= rs) & (r2 < rs + kh) & (c2 >= cs) & (c2 < cs + NA_WIN_COLS)
        dr = np.clip(r2 - r + NA_WIN_ROWS - 1, 0, 2 * NA_WIN_ROWS - 2)
        dc = np.clip(c2 - c + NA_WIN_COLS - 1, 0, 2 * NA_WIN_COLS - 2)
        tabs.append(jnp.where(jnp.asarray(valid)[None], rpb.astype(F32)[:, dr, dc], NEG_INF))
    return jnp.stack(tabs, axis=0)


def _attend(qh, parts):
    s = []
    for kh, _, bias in parts:
        sc = lax.dot_general(qh, kh, NT_DIMS, preferred_element_type=F32)
        s.append(sc if bias is None else sc + bias)
    m = functools.reduce(jnp.maximum, [jnp.max(t, axis=-1, keepdims=True) for t in s])
    p = [jnp.exp(t - m) for t in s]
    denom = functools.reduce(jnp.add, [jnp.sum(t, axis=-1, keepdims=True) for t in p])
    o = functools.reduce(jnp.add, [jnp.dot(t.astype(BF16), vh, preferred_element_type=F32)
                                   for t, (_, vh, _) in zip(p, parts)])
    return o / denom


def _na_kernel(q_ref, k_ref, v_ref, kc_ref, vc_ref, bias_ref, o_ref, *, rows):
    i = pl.program_id(1)
    ks = jnp.clip(i * NA_Q_ROWS - NA_WIN_ROWS // 2, 0, rows - NA_K_ROWS)
    win = pl.ds(pl.multiple_of(ks * GRID_W, GRID_W), NA_K_ROWS * GRID_W)
    q = q_ref[0]
    kw = k_ref[0, win, :]
    vw = v_ref[0, win, :]
    kc = kc_ref[0]
    vc = vc_ref[0]
    outs = []
    for h in range(NA_HEADS):
        sl = slice(h * NA_DH, (h + 1) * NA_DH)
        outs.append(_attend(q[:, sl], [(kw[:, sl], vw[:, sl], bias_ref[0, h]), (kc[:, sl], vc[:, sl], None)]))
    o_ref[0] = jnp.concatenate(outs, axis=1).astype(o_ref.dtype)


def _na_call(q, k, v, kc, vc, bias):
    b, l, w = q.shape
    c = kc.shape[1]
    rows = l // GRID_W
    nrb = rows // NA_Q_ROWS
    nq, nk = NA_Q_ROWS * GRID_W, NA_K_ROWS * GRID_W
    seq = lambda n: pl.BlockSpec((1, n, w), lambda bb, i: (bb, 0, 0))
    case = lambda bb, i: (jnp.where(i == 0, 0, jnp.where(i == nrb - 1, 2, 1)), 0, 0, 0)
    return pl.pallas_call(
        functools.partial(_na_kernel, rows=rows),
        grid=(b, nrb),
        in_specs=[pl.BlockSpec((1, nq, w), lambda bb, i: (bb, i, 0)), seq(l), seq(l), seq(c), seq(c),
                  pl.BlockSpec((1, NA_HEADS, nq, nk), case)],
        out_specs=pl.BlockSpec((1, nq, w), lambda bb, i: (bb, i, 0)),
        out_shape=jax.ShapeDtypeStruct((b, l, w), BF16),
        compiler_params=_cparams("arbitrary", "arbitrary"),
        name="na_latent",
    )(q, k, v, kc, vc, bias)


def _ctx_attn_kernel(q_ref, k_ref, v_ref, o_ref):
    q, k, v = q_ref[0], k_ref[0], v_ref[0]
    outs = []
    for h in range(NA_HEADS):
        sl = slice(h * NA_DH, (h + 1) * NA_DH)
        outs.append(_attend(q[:, sl], [(k[:, sl], v[:, sl], None)]))
    o_ref[0] = jnp.concatenate(outs, axis=1).astype(o_ref.dtype)


def _ctx_attn_call(q, k, v):
    b, c, w = q.shape
    spec = pl.BlockSpec((1, c, w), lambda bb: (bb, 0, 0))
    return pl.pallas_call(
        _ctx_attn_kernel, grid=(b,), in_specs=[spec, spec, spec], out_specs=spec,
        out_shape=jax.ShapeDtypeStruct((b, c, w), BF16),
        compiler_params=_cparams("arbitrary"), name="ctx_attn",
    )(q, k, v)


def _outproj_kernel(gla_ref, na_ref, scu_ref, scp_ref, scn_ref, x_ref, g1_ref,
                    wg_ref, ws_ref, wn_ref, cw_ref, cb_ref, o_ref):
    i = pl.program_id(1)
    last = pl.num_programs(1) - 1
    w = SC_WIDTH
    u = scu_ref[0]
    tm = u.shape[0]
    cx = u[:, w:2 * w] * u[:, 2 * w:3 * w]
    up = scp_ref[0]
    un = scn_ref[0]
    cx_prev = up[SUBLANES - 1:SUBLANES, w:2 * w] * up[SUBLANES - 1:SUBLANES, 2 * w:3 * w] * jnp.where(i > 0, 1.0, 0.0)
    cx_next = un[0:1, w:2 * w] * un[0:1, 2 * w:3 * w] * jnp.where(i < last, 1.0, 0.0)
    row = lax.broadcasted_iota(jnp.int32, cx.shape, 0)
    prev = jnp.where(row == 0, cx_prev, pltpu.roll(cx, 1, 0))
    nxt = jnp.where(row == tm - 1, cx_next, pltpu.roll(cx, tm - 1, 0))
    cw = cw_ref[...]
    sc = u[:, 0:w] * (prev * cw[0:1] + cx * cw[1:2] + nxt * cw[2:3] + cb_ref[...])
    mix = (jnp.dot(gla_ref[0], wg_ref[...], preferred_element_type=F32)
           + jnp.dot(sc.astype(BF16), ws_ref[...], preferred_element_type=F32)
           + jnp.dot(na_ref[0], wn_ref[...], preferred_element_type=F32))
    o_ref[0] = x_ref[0] + g1_ref[0] * mix


def _outproj_call(gla, na, scu, x, g1, wg, ws, wn, cw, cb, tm):
    b, l, d = x.shape
    bm = g1.shape[0]
    nblk8 = l // SUBLANES
    t8 = tm // SUBLANES
    mod_map = (lambda bb, i: (bb, 0, 0)) if bm > 1 else (lambda bb, i: (0, 0, 0))
    tok = lambda w: pl.BlockSpec((1, tm, w), lambda bb, i: (bb, i, 0))
    full = lambda a: pl.BlockSpec(a.shape, lambda bb, i: (0,) * a.ndim)
    halo_prev = pl.BlockSpec((1, SUBLANES, 3 * SC_WIDTH), lambda bb, i: (bb, jnp.maximum(i * t8 - 1, 0), 0))
    halo_next = pl.BlockSpec((1, SUBLANES, 3 * SC_WIDTH), lambda bb, i: (bb, jnp.minimum((i + 1) * t8, nblk8 - 1), 0))
    return pl.pallas_call(
        _outproj_kernel,
        grid=(b, l // tm),
        in_specs=[tok(GLA_V), tok(NA_W), tok(3 * SC_WIDTH), halo_prev, halo_next, tok(d),
                  pl.BlockSpec((1, 1, d), mod_map), full(wg), full(ws), full(wn), full(cw), full(cb)],
        out_specs=tok(d),
        out_shape=jax.ShapeDtypeStruct((b, l, d), F32),
        compiler_params=_cparams("arbitrary", "arbitrary"),
        name="out_proj",
    )(gla, na, scu, scu, scu, x, g1, wg, ws, wn, cw, cb)


def _ffn_kernel(*refs, final, chunk):
    if final:
        (x_ref, xp_ref, xn_ref, sh_ref, sc_ref, g2_ref, ng_ref, wup_ref, cw_ref, cb_ref, wdn_ref, fg_ref,
         o_ref, acc_ref) = refs
    else:
        (x_ref, xp_ref, xn_ref, sh_ref, sc_ref, g2_ref, ng_ref, wup_ref, cw_ref, cb_ref, wdn_ref,
         o_ref, acc_ref) = refs
    i = pl.program_id(1)
    last = pl.num_programs(1) - 1
    x = x_ref[0]
    tm = x.shape[0]
    hid = wdn_ref.shape[0]
    nm = lambda t: _norm_mod(t, ng_ref[...], sh_ref[0], sc_ref[0])
    hp = nm(xp_ref[0]) * jnp.where(i > 0, 1.0, 0.0)
    hn = nm(xn_ref[0]) * jnp.where(i < last, 1.0, 0.0)
    hb = jnp.concatenate([hp, nm(x), hn], axis=0).astype(BF16)
    ext = tm + 2 * SUBLANES
    inner = slice(SUBLANES, SUBLANES + tm)

    def conv(u, c0):
        cw = cw_ref[:, c0:c0 + chunk]
        return (pltpu.roll(u, 1, 0)[inner] * cw[0:1] + u[inner] * cw[1:2]
                + pltpu.roll(u, ext - 1, 0)[inner] * cw[2:3] + cb_ref[:, c0:c0 + chunk])

    for j in range(hid // chunk):
        c0 = j * chunk
        ua = jnp.dot(hb, wup_ref[:, c0:c0 + chunk], preferred_element_type=F32)
        ub = jnp.dot(hb, wup_ref[:, hid + c0:hid + c0 + chunk], preferred_element_type=F32)
        act = (_silu(conv(ua, c0)) * conv(ub, hid + c0)).astype(BF16)
        part = jnp.dot(act, wdn_ref[c0:c0 + chunk, :], preferred_element_type=F32)
        if j == 0:
            acc_ref[...] = part
        else:
            acc_ref[...] += part
    y = x + g2_ref[0] * acc_ref[...]
    if final:
        ms = jnp.mean(y * y, axis=-1, keepdims=True)
        y = y * lax.rsqrt(ms + EPS) * fg_ref[...]
    o_ref[0] = y


def _ffn_call(x, shift, scale, g2, gain, wup, cw, cb, wdn, final_gain, tm):
    b, l, d = x.shape
    bm = shift.shape[0]
    nblk8 = l // SUBLANES
    t8 = tm // SUBLANES
    final = final_gain is not None
    mod_map = (lambda bb, i: (bb, 0, 0)) if bm > 1 else (lambda bb, i: (0, 0, 0))
    tok = pl.BlockSpec((1, tm, d), lambda bb, i: (bb, i, 0))
    full = lambda a: pl.BlockSpec(a.shape, lambda bb, i: (0,) * a.ndim)
    resident = lambda a: pl.BlockSpec(a.shape, lambda bb, i: (0,) * a.ndim, pipeline_mode=pl.Buffered(1))
    halo_prev = pl.BlockSpec((1, SUBLANES, d), lambda bb, i: (bb, jnp.maximum(i * t8 - 1, 0), 0))
    halo_next = pl.BlockSpec((1, SUBLANES, d), lambda bb, i: (bb, jnp.minimum((i + 1) * t8, nblk8 - 1), 0))
    mod = pl.BlockSpec((1, 1, d), mod_map)
    in_specs = [tok, halo_prev, halo_next, mod, mod, mod, full(gain), resident(wup), full(cw), full(cb), resident(wdn)]
    args = [x, x, x, shift, scale, g2, gain, wup, cw, cb, wdn]
    if final:
        in_specs.append(full(final_gain))
        args.append(final_gain)
    return pl.pallas_call(
        functools.partial(_ffn_kernel, final=final, chunk=2 * LANES),
        grid=(b, l // tm),
        in_specs=in_specs,
        out_specs=tok,
        out_shape=jax.ShapeDtypeStruct((b, l, d), F32),
        scratch_shapes=[pltpu.VMEM((tm, d), F32)],
        compiler_params=_cparams("arbitrary", "arbitrary"),
        name="conv_ffn_final" if final else "conv_ffn",
    )(*args)


def _rope_tables(n_tokens):
    t = jnp.arange(n_tokens)
    n_freq = GLA_DK // 4
    inv_freq = ROPE_BASE ** (-jnp.arange(n_freq, dtype=F32) / n_freq)
    row = (t // GRID_W).astype(F32)[:, None] * inv_freq
    col = (t % GRID_W).astype(F32)[:, None] * inv_freq
    ang = jnp.concatenate([row, col], axis=-1)
    cos = jnp.repeat(jnp.cos(ang), 2, axis=-1)
    sin = jnp.repeat(jnp.sin(ang), 2, axis=-1) * jnp.tile(jnp.asarray([-1.0, 1.0], F32), GLA_DK // 2)
    pad = QK_PAD - GLA_QK
    cos = jnp.concatenate([jnp.tile(cos, (1, GLA_HEADS)), jnp.ones((n_tokens, pad), F32)], axis=-1)
    sin = jnp.concatenate([jnp.tile(sin, (1, GLA_HEADS)), jnp.zeros((n_tokens, pad), F32)], axis=-1)
    return cos, sin


def _tile_rows(n, target):
    t = min(n, target)
    assert n % t == 0 and t % GLA_GROUP == 0, (n, t)
    return t


def kernel(x, c, ctx, c_ctx, w_ada, b_ada, norm_mix_g, norm_ffn_g, w_in, gla_wg2_fw, gla_bg_fw, gla_wg2_bw, gla_bg_bw, gla_norm_g, sc_conv_w, sc_conv_b, na_rpb, w_out, ffn_w_up, ffn_conv_w, ffn_conv_b, ffn_w_down, final_norm_g):
    bsz, n_lat, d = x.shape
    n_ctx = ctx.shape[1]
    depth = w_in.shape[0]
    rows = n_lat // GRID_W
    assert n_lat % (NA_Q_ROWS * GRID_W) == 0 and rows >= NA_K_ROWS and bsz + 1 <= 2 * SUBLANES
    tm_lat, tm_ctx = _tile_rows(n_lat, 512), _tile_rows(n_ctx, 512)

    crows = jnp.zeros((2 * SUBLANES, d), F32).at[:bsz].set(c).at[bsz].set(c_ctx)
    mod_all = _ada_call(crows, w_ada, b_ada)

    zpad = lambda n: jnp.zeros((depth, d, n), F32)
    swap = np.arange(GLA_QK) ^ 1
    o = np.cumsum([0, GLA_QK, GLA_QK, GLA_V, GLA_LOWRANK, GLA_LOWRANK, GLA_V,
                   SC_WIDTH, SC_WIDTH, SC_WIDTH, NA_W, NA_W, NA_W])
    col = lambda n: w_in[:, :, o[n]:o[n + 1]]
    wq, wk = col(0), col(1)
    qpad = zpad(QK_PAD - GLA_QK)
    wqk_ctx = jnp.concatenate([wq, qpad, wk, qpad], axis=-1).astype(BF16)
    wqk_lat = jnp.concatenate([wq, qpad, wk, qpad, wq[:, :, swap], qpad, wk[:, :, swap], qpad], axis=-1).astype(BF16)
    wvr = jnp.concatenate([col(2), col(5), col(3), col(4), zpad(GLR_PAD - 2 * GLA_LOWRANK)], axis=-1).astype(BF16)
    wsc = w_in[:, :, o[6]:o[9]].astype(BF16)
    wna = jnp.concatenate([col(9) * (NA_DH ** -0.5), col(10), col(11)], axis=-1).astype(BF16)
    w_out_b = w_out.astype(BF16)
    wo_gla, wo_sc, wo_na = w_out_b[:, :GLA_V], w_out_b[:, GLA_V:GLA_V + SC_WIDTH], w_out_b[:, GLA_V + SC_WIDTH:]
    wup = ffn_w_up.astype(BF16)
    wdn = ffn_w_down.astype(BF16)

    def gate_w(w2, first_row):
        full = jnp.zeros((depth, GLR_PAD, QK_PAD), F32)
        return full.at[:, first_row:first_row + GLA_LOWRANK, :GLA_QK].set(w2).astype(BF16)

    def gate_b(bias):
        return jnp.zeros((depth, 1, QK_PAD), F32).at[:, 0, :GLA_QK].set(bias)

    wg_f, wg_b = gate_w(gla_wg2_fw, 0), gate_w(gla_wg2_bw, GLA_LOWRANK)
    bg_f, bg_b = gate_b(gla_bg_fw), gate_b(gla_bg_bw)
    gla_gain = jnp.tile(gla_norm_g, (1, GLA_HEADS))[:, None, :]
    hv = np.arange(GLA_V) // GLA_DV
    head_mean = jnp.asarray((hv[:, None] == hv[None, :]) / GLA_DV, BF16)
    tabs_f, tabs_b = _gla_tables(False), _gla_tables(True)
    rope_tabs = _rope_tables(n_lat)
    zero_state = jnp.zeros((bsz, GLA_V, QK_PAD), F32)

    xc = ctx
    for layer in range(depth):
        update_ctx = layer < depth - 1
        mod = mod_all[layer]
        sh1, sc1, g1, sh2, sc2, g2 = [mod[:bsz, n * d:(n + 1) * d][:, None, :] for n in range(6)]
        sh1c, sc1c, g1c, sh2c, sc2c, g2c = [mod[bsz:bsz + 1, n * d:(n + 1) * d][:, None, :] for n in range(6)]
        gain_mix = norm_mix_g[layer][None, :]
        gain_ffn = norm_ffn_g[layer][None, :]
        wts = (wvr[layer], wsc[layer], wna[layer])

        cq, ck, cv, cr, cglr, cscu, cnaq, cnak, cnav = _inproj_call(
            xc, sh1c, sc1c, gain_mix, wqk_ctx[layer], *wts, None, tm_ctx)
        lq, lk, lv, lr, lglr, lscu, lnaq, lnak, lnav = _inproj_call(
            x, sh1, sc1, gain_mix, wqk_lat[layer], *wts, rope_tabs, tm_lat)

        fin = lambda r, o_f: (r, o_f, gla_gain[layer], head_mean)
        oc_f, state_f = _gla_call(cq, ck, cv, cglr, wg_f[layer], bg_f[layer], tabs_f, zero_state, tm_ctx, False)
        gla_ctx, state_b = _gla_call(cq, ck, cv, cglr, wg_b[layer], bg_b[layer], tabs_b, zero_state, tm_ctx, True,
                                     fin(cr, oc_f))
        ol_f, _ = _gla_call(lq, lk, lv, lglr, wg_f[layer], bg_f[layer], tabs_f, state_f, tm_lat, False)
        gla_lat, _ = _gla_call(lq, lk, lv, lglr, wg_b[layer], bg_b[layer], tabs_b, state_b, tm_lat, True,
                               fin(lr, ol_f))

        na_lat = _na_call(lnaq, lnak, lnav, cnak, cnav, _na_bias_tables(na_rpb[layer], rows))

        cw_sc, cb_sc = sc_conv_w[layer], sc_conv_b[layer][None, :]
        wo = (wo_gla[layer], wo_sc[layer], wo_na[layer])
        x = _outproj_call(gla_lat, na_lat, lscu, x, g1, *wo, cw_sc, cb_sc, tm_lat)
        ffn = (wup[layer], ffn_conv_w[layer], ffn_conv_b[layer][None, :], wdn[layer])
        x = _ffn_call(x, sh2, sc2, g2, gain_ffn, *ffn,
                      final_norm_g[None, :] if layer == depth - 1 else None, tm_lat)

        if update_ctx:
            na_ctx = _ctx_attn_call(cnaq, cnak, cnav)
            xc = _outproj_call(gla_ctx, na_ctx, cscu, xc, g1c, *wo, cw_sc, cb_sc, tm_ctx)
            xc = _ffn_call(xc, sh2c, sc2c, g2c, gain_ffn, *ffn, None, tm_ctx)
    return x
```

```python
import functools

import numpy as np
import jax
import jax.numpy as jnp
from jax import lax
from jax.experimental import pallas as pl
from jax.experimental.pallas import tpu as pltpu

F32 = jnp.float32
BF16 = jnp.bfloat16

GRID_W = 64
EPS = 1e-6
NEG_INF = -1e30
GLA_HEADS = 6
GLA_DK = 32
GLA_DV = 64
GLA_LOWRANK = 16
GLA_TAU = 16.0
ROPE_BASE = 10000.0
SC_WIDTH = 256
NA_HEADS = 6
NA_DH = 64
NA_WIN_ROWS = 8
NA_WIN_COLS = 16
GLA_QK = GLA_HEADS * GLA_DK
GLA_V = GLA_HEADS * GLA_DV
NA_W = NA_HEADS * NA_DH

LANES = 128
SUBLANES = 8
VMEM_LIMIT_BYTES = 56 * 1024 * 1024

QK_PAD = 2 * LANES
GLR_PAD = LANES
GLA_GROUP = 128
GLA_LEVELS = 4
HEADS_A = LANES // GLA_DK
GLA_SINGLE_LEVEL_DECAY = 64.0
NA_Q_ROWS = 4
NA_K_ROWS = 12

NT_DIMS = (((1,), (1,)), ((), ()))


def _cparams(*sem):
    return pltpu.CompilerParams(dimension_semantics=sem, vmem_limit_bytes=VMEM_LIMIT_BYTES)


def _silu(t):
    return t / (1.0 + jnp.exp(-t))


def _norm_mod(t, gain, shift, scale):
    ms = jnp.mean(t * t, axis=-1, keepdims=True)
    return (t * lax.rsqrt(ms + EPS) * gain) * (1.0 + scale) + shift


def _ada_kernel(c_ref, w_ref, b_ref, o_ref):
    c = c_ref[...]
    o_ref[0] = jnp.dot(_silu(c), w_ref[0], preferred_element_type=F32,
                       precision=lax.Precision.HIGHEST) + b_ref[0]


def _ada_call(crows, w_ada, b_ada):
    depth, d, n = w_ada.shape
    rows = crows.shape[0]
    tn = 1536
    return pl.pallas_call(
        _ada_kernel,
        grid=(depth, n // tn),
        in_specs=[pl.BlockSpec((rows, d), lambda l, j: (0, 0)),
                  pl.BlockSpec((1, d, tn), lambda l, j: (l, 0, j)),
                  pl.BlockSpec((1, 1, tn), lambda l, j: (l, 0, j))],
        out_specs=pl.BlockSpec((1, rows, tn), lambda l, j: (l, 0, j)),
        out_shape=jax.ShapeDtypeStruct((depth, rows, n), F32),
        compiler_params=_cparams("arbitrary", "arbitrary"),
        name="ada_mod",
    )(crows, w_ada, b_ada.reshape(depth, 1, n))


def _inproj_kernel(*refs, rope):
    if rope:
        (x_ref, sh_ref, sc_ref, g_ref, cos_ref, sin_ref, wqk_ref, wvr_ref, wsc_ref, wna_ref,
         q_ref, k_ref, v_ref, r_ref, glr_ref, scu_ref, naq_ref, nak_ref, nav_ref) = refs
    else:
        (x_ref, sh_ref, sc_ref, g_ref, wqk_ref, wvr_ref, wsc_ref, wna_ref,
         q_ref, k_ref, v_ref, r_ref, glr_ref, scu_ref, naq_ref, nak_ref, nav_ref) = refs
    hb = _norm_mod(x_ref[0], g_ref[...], sh_ref[0], sc_ref[0]).astype(BF16)
    qk = jnp.dot(hb, wqk_ref[...], preferred_element_type=F32)
    p = QK_PAD
    if rope:
        c = cos_ref[...]
        s = sin_ref[...]
        q = (qk[:, 0:p] * c + qk[:, 2 * p:3 * p] * s) * (GLA_DK ** -0.5)
        k = qk[:, p:2 * p] * c + qk[:, 3 * p:4 * p] * s
    else:
        q = qk[:, 0:p] * (GLA_DK ** -0.5)
        k = qk[:, p:2 * p]
    q_ref[0] = q
    k_ref[0] = k
    vr = jnp.dot(hb, wvr_ref[...], preferred_element_type=F32)
    v_ref[0] = vr[:, 0:GLA_V]
    r_ref[0] = vr[:, GLA_V:2 * GLA_V]
    glr_ref[0] = vr[:, 2 * GLA_V:2 * GLA_V + GLR_PAD]
    scu_ref[0] = jnp.dot(hb, wsc_ref[...], preferred_element_type=F32)
    na = jnp.dot(hb, wna_ref[...], preferred_element_type=F32)
    naq_ref[0] = na[:, 0:NA_W].astype(BF16)
    nak_ref[0] = na[:, NA_W:2 * NA_W].astype(BF16)
    nav_ref[0] = na[:, 2 * NA_W:3 * NA_W].astype(BF16)


def _inproj_call(x, shift, scale, gain, wqk, wvr, wsc, wna, rope_tabs, tm):
    b, l, d = x.shape
    bm = shift.shape[0]
    rope = rope_tabs is not None
    mod_map = (lambda i, bb: (bb, 0, 0)) if bm > 1 else (lambda i, bb: (0, 0, 0))
    tok = lambda w: pl.BlockSpec((1, tm, w), lambda i, bb: (bb, i, 0))
    full = lambda a: pl.BlockSpec(a.shape, lambda i, bb: (0,) * a.ndim)
    in_specs = [tok(d), pl.BlockSpec((1, 1, d), mod_map), pl.BlockSpec((1, 1, d), mod_map), full(gain)]
    args = [x, shift, scale, gain]
    if rope:
        in_specs += [pl.BlockSpec((tm, QK_PAD), lambda i, bb: (i, 0))] * 2
        args += list(rope_tabs)
    in_specs += [full(wqk), full(wvr), full(wsc), full(wna)]
    args += [wqk, wvr, wsc, wna]
    widths = [(QK_PAD, F32), (QK_PAD, F32), (GLA_V, F32), (GLA_V, F32), (GLR_PAD, F32),
              (3 * SC_WIDTH, F32), (NA_W, BF16), (NA_W, BF16), (NA_W, BF16)]
    return pl.pallas_call(
        functools.partial(_inproj_kernel, rope=rope),
        grid=(l // tm, b),
        in_specs=in_specs,
        out_specs=[tok(w) for w, _ in widths],
        out_shape=[jax.ShapeDtypeStruct((b, l, w), dt) for w, dt in widths],
        compiler_params=_cparams("arbitrary", "arbitrary"),
        name="in_proj_rope" if rope else "in_proj_ctx",
    )(*args)


def _gla_tables(reverse):
    t = np.arange(GLA_GROUP)
    i, j = t[:, None], t[None, :]
    before = (j >= i) if reverse else (j <= i)
    strictly_after = (j < i) if reverse else (j > i)
    maps = []
    for n in (16, 32, 64, 128):
        maps.append((i // n == j // n) & before)
    for n in (16, 32, 64, 128):
        maps.append((i // n == j // n) & strictly_after)
    masks = [(i // 16 == j // 16) & before]
    for n in (32, 64, 128):
        h = n // 2
        q_late, k_early = (i % n) >= h, (j % n) < h
        if reverse:
            q_late, k_early = (i % n) < h, (j % n) >= h
        masks.append((i // n == j // n) & q_late & k_early)
    cm = np.concatenate(maps, axis=0).astype(np.float32)
    cm2 = np.concatenate([maps[3], maps[7]], axis=0).astype(np.float32)
    lm = np.stack(masks, axis=0).astype(np.float32)
    return jnp.asarray(cm, BF16), jnp.asarray(cm2, BF16), jnp.asarray(lm, F32)


def _gla_kernel(*refs, reverse, finalize, ngroups):
    if finalize:
        (q_ref, k_ref, v_ref, glr_ref, wg_ref, bg_ref, cm_ref, cm2_ref, lm_ref, s0_ref,
         r_ref, of_ref, ng_ref, bd_ref, out_ref, sfin_ref, st_ref, gate_ref) = refs
    else:
        (q_ref, k_ref, v_ref, glr_ref, wg_ref, bg_ref, cm_ref, cm2_ref, lm_ref, s0_ref,
         out_ref, sfin_ref, st_ref, gate_ref) = refs
    step = pl.program_id(1)
    g = GLA_GROUP
    kc = QK_PAD // 2
    vc = HEADS_A * GLA_DV

    @pl.when(step == 0)
    def _():
        st_ref[...] = s0_ref[0]

    z = jnp.dot(glr_ref[0].astype(BF16), wg_ref[...], preferred_element_type=F32) + bg_ref[...]
    gate_all = (jnp.minimum(z, 0.0) - jnp.log(1.0 + jnp.exp(-jnp.abs(z)))) * (1.0 / GLA_TAU)
    gate_ref[...] = gate_all
    total = jnp.sum(gate_all.reshape(ngroups, g, QK_PAD), axis=1)
    single_level = jnp.min(total) > -GLA_SINGLE_LEVEL_DECAY

    k_head = lax.broadcasted_iota(jnp.int32, (g, kc), 1) // GLA_DK
    kmask = [k_head == h for h in range(HEADS_A)]
    va_head = lax.broadcasted_iota(jnp.int32, (g, vc), 1) // GLA_DV
    vb_head = lax.broadcasted_iota(jnp.int32, (g, GLA_V - vc), 1) // GLA_DV
    st_row = lax.broadcasted_iota(jnp.int32, (GLA_V, kc), 0)
    st_col = lax.broadcasted_iota(jnp.int32, (GLA_V, kc), 1) // GLA_DK
    state_mask = st_row // GLA_DV == st_col + jnp.where(st_row >= vc, HEADS_A, 0)
    lmask = [lm_ref[l] > 0.5 for l in range(GLA_LEVELS)]
    causal = functools.reduce(jnp.logical_or, lmask)

    def stack_heads(t, masks, n):
        return jnp.concatenate([jnp.where(masks[h], t, 0.0) for h in range(n)], axis=0).astype(BF16)

    def scores(qs, ks):
        sa = lax.dot_general(qs[:, :kc].astype(BF16), stack_heads(ks[:, :kc], kmask, HEADS_A), NT_DIMS,
                             preferred_element_type=F32)
        sb = lax.dot_general(qs[:, kc:].astype(BF16), stack_heads(ks[:, kc:], kmask, GLA_HEADS - HEADS_A),
                             NT_DIMS, preferred_element_type=F32)
        return jnp.concatenate([sa, sb], axis=1)

    def group(gi, robust):
        gidx = (ngroups - 1 - gi) if reverse else gi
        rows = pl.ds(gidx * g, g)
        q = q_ref[0, rows, :]
        k = k_ref[0, rows, :]
        v = v_ref[0, rows, :]
        gate = gate_ref[rows, :]
        ghi = gate.astype(BF16)
        glo = (gate - ghi.astype(F32)).astype(BF16)
        cmat = cm_ref[...] if robust else cm2_ref[...]
        cs = jnp.dot(cmat, ghi, preferred_element_type=F32) + jnp.dot(cmat, glo, preferred_element_type=F32)
        if robust:
            p16, p32, p64, p128, s16, s32, s64, s128 = [cs[n * g:(n + 1) * g] for n in range(8)]
            q0 = q * jnp.exp(p16)
            a = [scores(q0, k * jnp.exp(-p16)),
                 scores(q0, k * jnp.exp(s16)),
                 scores(q * jnp.exp(p32), k * jnp.exp(s32)),
                 scores(q * jnp.exp(p64), k * jnp.exp(s64))]
            blocks = []
            for h in range(GLA_HEADS):
                sl = slice(h * g, (h + 1) * g)
                blk = jnp.where(lmask[3], a[3][:, sl], 0.0)
                for l in (2, 1, 0):
                    blk = jnp.where(lmask[l], a[l][:, sl], blk)
                blocks.append(blk)
            q4 = q * jnp.exp(p128)
        else:
            p128, s128 = cs[0:g], cs[g:2 * g]
            q4 = q * jnp.exp(p128)
            a = scores(q4, k * jnp.exp(-p128))
            blocks = [jnp.where(causal, a[:, h * g:(h + 1) * g], 0.0) for h in range(GLA_HEADS)]
        amat_a = jnp.concatenate(blocks[:HEADS_A], axis=1).astype(BF16)
        amat_b = jnp.concatenate(blocks[HEADS_A:], axis=1).astype(BF16)
        v_a = stack_heads(v[:, :vc], [va_head == h for h in range(HEADS_A)], HEADS_A)
        v_b = stack_heads(v[:, vc:], [vb_head == h for h in range(GLA_HEADS - HEADS_A)], GLA_HEADS - HEADS_A)
        st = st_ref[...]
        stb = st.astype(BF16)
        q4b = q4.astype(BF16)
        o_a = (jnp.dot(amat_a, v_a, preferred_element_type=F32)
               + lax.dot_general(q4b[:, :kc], stb[:vc], NT_DIMS, preferred_element_type=F32))
        o_b = (jnp.dot(amat_b, v_b, preferred_element_type=F32)
               + lax.dot_general(q4b[:, kc:], stb[vc:], NT_DIMS, preferred_element_type=F32))
        o = jnp.concatenate([o_a, o_b], axis=1)
        vt = v.T.astype(BF16)
        k4 = (k * jnp.exp(s128)).astype(BF16)
        upd = jnp.concatenate([jnp.dot(vt[:vc], k4[:, :kc], preferred_element_type=F32),
                               jnp.dot(vt[vc:], k4[:, kc:], preferred_element_type=F32)], axis=0)
        decay = jnp.exp(p128[0:1] + s128[0:1])
        dec = jnp.concatenate([jnp.broadcast_to(decay[:, :kc], (vc, kc)),
                               jnp.broadcast_to(decay[:, kc:], (GLA_V - vc, kc))], axis=0)
        st_ref[...] = st * dec + jnp.where(state_mask, upd, 0.0)
        if finalize:
            ot = o + of_ref[0, rows, :]
            o2 = ot * ot
            hi = o2.astype(BF16)
            lo = (o2 - hi.astype(F32)).astype(BF16)
            ms = (jnp.dot(hi, bd_ref[...], preferred_element_type=F32)
                  + jnp.dot(lo, bd_ref[...], preferred_element_type=F32))
            y = ot * lax.rsqrt(ms + EPS) * ng_ref[...] * _silu(r_ref[0, rows, :])
            out_ref[0, rows, :] = y.astype(out_ref.dtype)
        else:
            out_ref[0, rows, :] = o

    @pl.when(single_level)
    def _():
        for gi in range(ngroups):
            group(gi, False)

    @pl.when(jnp.logical_not(single_level))
    def _():
        for gi in range(ngroups):
            group(gi, True)

    @pl.when(step == pl.num_programs(1) - 1)
    def _():
        sfin_ref[0] = st_ref[...]


def _gla_call(q, k, v, glr, wg, bg, tabs, s0, ts, reverse, fin=None):
    b, l, _ = q.shape
    nb = l // ts
    finalize = fin is not None
    cm, cm2, lm = tabs
    blk = (lambda bb, i: (bb, nb - 1 - i, 0)) if reverse else (lambda bb, i: (bb, i, 0))
    tok = lambda w: pl.BlockSpec((1, ts, w), blk)
    full = lambda a: pl.BlockSpec(a.shape, lambda bb, i: (0,) * a.ndim)
    state = pl.BlockSpec((1, GLA_V, QK_PAD // 2), lambda bb, i: (bb, 0, 0))
    in_specs = [tok(QK_PAD), tok(QK_PAD), tok(GLA_V), tok(GLR_PAD), full(wg), full(bg), full(cm), full(cm2),
                full(lm), state]
    args = [q, k, v, glr, wg, bg, cm, cm2, lm, s0]
    if finalize:
        r, o_other, ng, bd = fin
        in_specs += [tok(GLA_V), tok(GLA_V), full(ng), full(bd)]
        args += [r, o_other, ng, bd]
    out_dt = BF16 if finalize else F32
    return pl.pallas_call(
        functools.partial(_gla_kernel, reverse=reverse, finalize=finalize, ngroups=ts // GLA_GROUP),
        grid=(b, nb),
        in_specs=in_specs,
        out_specs=[tok(GLA_V), state],
        out_shape=[jax.ShapeDtypeStruct((b, l, GLA_V), out_dt),
                   jax.ShapeDtypeStruct((b, GLA_V, QK_PAD // 2), F32)],
        scratch_shapes=[pltpu.VMEM((GLA_V, QK_PAD // 2), F32), pltpu.VMEM((ts, QK_PAD), F32)],
        compiler_params=_cparams("arbitrary", "arbitrary"),
        name="gla_bwd" if reverse else "gla_fwd",
    )(*args)


def _na_bias_tables(rpb, rows):
    kh = min(NA_WIN_ROWS, rows)
    n_dr, n_dc = 2 * NA_WIN_ROWS - 1, 2 * NA_WIN_COLS - 1
    qr, kr = np.arange(NA_Q_ROWS)[:, None], np.arange(NA_K_ROWS)[None, :]
    row_pick, row_ok = [], []
    for r0, ks in ((0, 0), (NA_WIN_ROWS // 2, 0), (rows - NA_Q_ROWS, rows - NA_K_ROWS)):
        r, r2 = r0 + qr, ks + kr
        rs = np.clip(r - kh // 2, 0, rows - kh)
        row_ok.append((r2 >= rs) & (r2 < rs + kh))
        row_pick.append(np.eye(n_dr, dtype=np.float32)[np.clip(r2 - r + NA_WIN_ROWS - 1, 0, n_dr - 1)])
    row_pick, row_ok = np.stack(row_pick), np.stack(row_ok)
    qc, kc = np.arange(GRID_W)[:, None], np.arange(GRID_W)[None, :]
    cs = np.clip(qc - NA_WIN_COLS // 2, 0, GRID_W - NA_WIN_COLS)
    col_ok = (kc >= cs) & (kc < cs + NA_WIN_COLS)
    col_pick = np.eye(n_dc, dtype=np.float32)[np.clip(kc - qc + NA_WIN_COLS - 1, 0, n_dc - 1)]
    hi = lax.Precision.HIGHEST
    by_row = jnp.einsum('lhde,cqkd->lchqke', rpb.astype(F32), row_pick, precision=hi)
    bias = jnp.einsum('lchqke,xye->lchqxky', by_row, col_pick, precision=hi)
    valid = row_ok[:, None, :, None, :, None] & col_ok[None, None, None, :, None, :]
    bias = jnp.where(valid[None], bias, NEG_INF)
    return bias.reshape(rpb.shape[0], 3, NA_HEADS, NA_Q_ROWS * GRID_W, NA_K_ROWS * GRID_W)


def _attend(qh, parts):
    s = []
    for kh, _, bias in parts:
        sc = lax.dot_general(qh, kh, NT_DIMS, preferred_element_type=F32)
        s.append(sc if bias is None else sc + bias)
    m = functools.reduce(jnp.maximum, [jnp.max(t, axis=-1, keepdims=True) for t in s])
    p = [jnp.exp(t - m) for t in s]
    denom = functools.reduce(jnp.add, [jnp.sum(t, axis=-1, keepdims=True) for t in p])
    o = functools.reduce(jnp.add, [jnp.dot(t.astype(BF16), vh, preferred_element_type=F32)
                                   for t, (_, vh, _) in zip(p, parts)])
    return o / denom


def _na_kernel(q_ref, k_ref, v_ref, kc_ref, vc_ref, bias_ref, o_ref, *, rows):
    i = pl.program_id(1)
    ks = jnp.clip(i * NA_Q_ROWS - NA_WIN_ROWS // 2, 0, rows - NA_K_ROWS)
    win = pl.ds(pl.multiple_of(ks * GRID_W, GRID_W), NA_K_ROWS * GRID_W)
    q = q_ref[0]
    kw = k_ref[0, win, :]
    vw = v_ref[0, win, :]
    kc = kc_ref[0]
    vc = vc_ref[0]
    outs = []
    for h in range(NA_HEADS):
        sl = slice(h * NA_DH, (h + 1) * NA_DH)
        outs.append(_attend(q[:, sl], [(kw[:, sl], vw[:, sl], bias_ref[0, h]), (kc[:, sl], vc[:, sl], None)]))
    o_ref[0] = jnp.concatenate(outs, axis=1).astype(o_ref.dtype)


def _na_call(q, k, v, kc, vc, bias):
    b, l, w = q.shape
    c = kc.shape[1]
    rows = l // GRID_W
    nrb = rows // NA_Q_ROWS
    nq, nk = NA_Q_ROWS * GRID_W, NA_K_ROWS * GRID_W
    seq = lambda n: pl.BlockSpec((1, n, w), lambda bb, i: (bb, 0, 0))
    case = lambda bb, i: (jnp.where(i == 0, 0, jnp.where(i == nrb - 1, 2, 1)), 0, 0, 0)
    return pl.pallas_call(
        functools.partial(_na_kernel, rows=rows),
        grid=(b, nrb),
        in_specs=[pl.BlockSpec((1, nq, w), lambda bb, i: (bb, i, 0)), seq(l), seq(l), seq(c), seq(c),
                  pl.BlockSpec((1, NA_HEADS, nq, nk), case)],
        out_specs=pl.BlockSpec((1, nq, w), lambda bb, i: (bb, i, 0)),
        out_shape=jax.ShapeDtypeStruct((b, l, w), BF16),
        compiler_params=_cparams("arbitrary", "arbitrary"),
        name="na_latent",
    )(q, k, v, kc, vc, bias)


def _ctx_attn_kernel(q_ref, k_ref, v_ref, o_ref):
    q, k, v = q_ref[0], k_ref[0], v_ref[0]
    outs = []
    for h in range(NA_HEADS):
        sl = slice(h * NA_DH, (h + 1) * NA_DH)
        outs.append(_attend(q[:, sl], [(k[:, sl], v[:, sl], None)]))
    o_ref[0] = jnp.concatenate(outs, axis=1).astype(o_ref.dtype)


def _ctx_attn_call(q, k, v):
    b, c, w = q.shape
    spec = pl.BlockSpec((1, c, w), lambda bb: (bb, 0, 0))
    return pl.pallas_call(
        _ctx_attn_kernel, grid=(b,), in_specs=[spec, spec, spec], out_specs=spec,
        out_shape=jax.ShapeDtypeStruct((b, c, w), BF16),
        compiler_params=_cparams("arbitrary"), name="ctx_attn",
    )(q, k, v)


def _outproj_kernel(gla_ref, na_ref, scu_ref, scp_ref, scn_ref, x_ref, g1_ref,
                    wg_ref, ws_ref, wn_ref, cw_ref, cb_ref, o_ref):
    i = pl.program_id(1)
    last = pl.num_programs(1) - 1
    w = SC_WIDTH
    u = scu_ref[0]
    tm = u.shape[0]
    cx = u[:, w:2 * w] * u[:, 2 * w:3 * w]
    up = scp_ref[0]
    un = scn_ref[0]
    cx_prev = up[SUBLANES - 1:SUBLANES, w:2 * w] * up[SUBLANES - 1:SUBLANES, 2 * w:3 * w] * jnp.where(i > 0, 1.0, 0.0)
    cx_next = un[0:1, w:2 * w] * un[0:1, 2 * w:3 * w] * jnp.where(i < last, 1.0, 0.0)
    row = lax.broadcasted_iota(jnp.int32, cx.shape, 0)
    prev = jnp.where(row == 0, cx_prev, pltpu.roll(cx, 1, 0))
    nxt = jnp.where(row == tm - 1, cx_next, pltpu.roll(cx, tm - 1, 0))
    cw = cw_ref[...]
    sc = u[:, 0:w] * (prev * cw[0:1] + cx * cw[1:2] + nxt * cw[2:3] + cb_ref[...])
    mix = (jnp.dot(gla_ref[0], wg_ref[...], preferred_element_type=F32)
           + jnp.dot(sc.astype(BF16), ws_ref[...], preferred_element_type=F32)
           + jnp.dot(na_ref[0], wn_ref[...], preferred_element_type=F32))
    o_ref[0] = x_ref[0] + g1_ref[0] * mix


def _outproj_call(gla, na, scu, x, g1, wg, ws, wn, cw, cb, tm):
    b, l, d = x.shape
    bm = g1.shape[0]
    nblk8 = l // SUBLANES
    t8 = tm // SUBLANES
    mod_map = (lambda bb, i: (bb, 0, 0)) if bm > 1 else (lambda bb, i: (0, 0, 0))
    tok = lambda w: pl.BlockSpec((1, tm, w), lambda bb, i: (bb, i, 0))
    full = lambda a: pl.BlockSpec(a.shape, lambda bb, i: (0,) * a.ndim)
    halo_prev = pl.BlockSpec((1, SUBLANES, 3 * SC_WIDTH), lambda bb, i: (bb, jnp.maximum(i * t8 - 1, 0), 0))
    halo_next = pl.BlockSpec((1, SUBLANES, 3 * SC_WIDTH), lambda bb, i: (bb, jnp.minimum((i + 1) * t8, nblk8 - 1), 0))
    return pl.pallas_call(
        _outproj_kernel,
        grid=(b, l // tm),
        in_specs=[tok(GLA_V), tok(NA_W), tok(3 * SC_WIDTH), halo_prev, halo_next, tok(d),
                  pl.BlockSpec((1, 1, d), mod_map), full(wg), full(ws), full(wn), full(cw), full(cb)],
        out_specs=tok(d),
        out_shape=jax.ShapeDtypeStruct((b, l, d), F32),
        compiler_params=_cparams("arbitrary", "arbitrary"),
        name="out_proj",
    )(gla, na, scu, scu, scu, x, g1, wg, ws, wn, cw, cb)


def _ffn_kernel(*refs, final, chunk):
    if final:
        (x_ref, xp_ref, xn_ref, sh_ref, sc_ref, g2_ref, ng_ref, wup_ref, cw_ref, cb_ref, wdn_ref, fg_ref,
         o_ref, ua_ref, ub_ref, act_ref) = refs
    else:
        (x_ref, xp_ref, xn_ref, sh_ref, sc_ref, g2_ref, ng_ref, wup_ref, cw_ref, cb_ref, wdn_ref,
         o_ref, ua_ref, ub_ref, act_ref) = refs
    i = pl.program_id(1)
    last = pl.num_programs(1) - 1
    x = x_ref[0]
    tm = x.shape[0]
    hid = wdn_ref.shape[0]
    nm = lambda t: _norm_mod(t, ng_ref[...], sh_ref[0], sc_ref[0])
    hp = nm(xp_ref[0]) * jnp.where(i > 0, 1.0, 0.0)
    hn = nm(xn_ref[0]) * jnp.where(i < last, 1.0, 0.0)
    hb = jnp.concatenate([hp, nm(x), hn], axis=0).astype(BF16)

    def conv(u_ref, slot, c0):
        cw = cw_ref[:, c0:c0 + chunk]
        return (u_ref[slot, pl.ds(SUBLANES - 1, tm), :] * cw[0:1] + u_ref[slot, pl.ds(SUBLANES, tm), :] * cw[1:2]
                + u_ref[slot, pl.ds(SUBLANES + 1, tm), :] * cw[2:3] + cb_ref[:, c0:c0 + chunk])

    for j in range(hid // chunk):
        c0 = j * chunk
        slot = j % 2
        ua_ref[slot] = jnp.dot(hb, wup_ref[:, c0:c0 + chunk], preferred_element_type=F32)
        ub_ref[slot] = jnp.dot(hb, wup_ref[:, hid + c0:hid + c0 + chunk], preferred_element_type=F32)
        act_ref[:, c0:c0 + chunk] = (_silu(conv(ua_ref, slot, c0)) * conv(ub_ref, slot, hid + c0)).astype(BF16)
    y = x + g2_ref[0] * jnp.dot(act_ref[...], wdn_ref[...], preferred_element_type=F32)
    if final:
        ms = jnp.mean(y * y, axis=-1, keepdims=True)
        y = y * lax.rsqrt(ms + EPS) * fg_ref[...]
    o_ref[0] = y


def _ffn_call(x, shift, scale, g2, gain, wup, cw, cb, wdn, final_gain, tm):
    b, l, d = x.shape
    bm = shift.shape[0]
    nblk8 = l // SUBLANES
    t8 = tm // SUBLANES
    final = final_gain is not None
    mod_map = (lambda bb, i: (bb, 0, 0)) if bm > 1 else (lambda bb, i: (0, 0, 0))
    tok = pl.BlockSpec((1, tm, d), lambda bb, i: (bb, i, 0))
    full = lambda a: pl.BlockSpec(a.shape, lambda bb, i: (0,) * a.ndim)
    resident = lambda a: pl.BlockSpec(a.shape, lambda bb, i: (0,) * a.ndim, pipeline_mode=pl.Buffered(1))
    halo_prev = pl.BlockSpec((1, SUBLANES, d), lambda bb, i: (bb, jnp.maximum(i * t8 - 1, 0), 0))
    halo_next = pl.BlockSpec((1, SUBLANES, d), lambda bb, i: (bb, jnp.minimum((i + 1) * t8, nblk8 - 1), 0))
    mod = pl.BlockSpec((1, 1, d), mod_map)
    in_specs = [tok, halo_prev, halo_next, mod, mod, mod, full(gain), resident(wup), full(cw), full(cb), resident(wdn)]
    args = [x, x, x, shift, scale, g2, gain, wup, cw, cb, wdn]
    if final:
        in_specs.append(full(final_gain))
        args.append(final_gain)
    chunk = 2 * LANES
    return pl.pallas_call(
        functools.partial(_ffn_kernel, final=final, chunk=chunk),
        grid=(b, l // tm),
        in_specs=in_specs,
        out_specs=tok,
        out_shape=jax.ShapeDtypeStruct((b, l, d), F32),
        scratch_shapes=[pltpu.VMEM((2, tm + 2 * SUBLANES, chunk), F32),
                        pltpu.VMEM((2, tm + 2 * SUBLANES, chunk), F32),
                        pltpu.VMEM((tm, wdn.shape[0]), BF16)],
        compiler_params=_cparams("arbitrary", "arbitrary"),
        name="conv_ffn_final" if final else "conv_ffn",
    )(*args)


def _rope_tables(n_tokens):
    t = jnp.arange(n_tokens)
    n_freq = GLA_DK // 4
    inv_freq = ROPE_BASE ** (-jnp.arange(n_freq, dtype=F32) / n_freq)
    row = (t // GRID_W).astype(F32)[:, None] * inv_freq
    col = (t % GRID_W).astype(F32)[:, None] * inv_freq
    ang = jnp.concatenate([row, col], axis=-1)
    cos = jnp.repeat(jnp.cos(ang), 2, axis=-1)
    sin = jnp.repeat(jnp.sin(ang), 2, axis=-1) * jnp.tile(jnp.asarray([-1.0, 1.0], F32), GLA_DK // 2)
    pad = QK_PAD - GLA_QK
    cos = jnp.concatenate([jnp.tile(cos, (1, GLA_HEADS)), jnp.ones((n_tokens, pad), F32)], axis=-1)
    sin = jnp.concatenate([jnp.tile(sin, (1, GLA_HEADS)), jnp.zeros((n_tokens, pad), F32)], axis=-1)
    return cos, sin


def _tile_rows(n, target):
    t = min(n, target)
    assert n % t == 0 and t % GLA_GROUP == 0, (n, t)
    return t


def kernel(x, c, ctx, c_ctx, w_ada, b_ada, norm_mix_g, norm_ffn_g, w_in, gla_wg2_fw, gla_bg_fw, gla_wg2_bw, gla_bg_bw, gla_norm_g, sc_conv_w, sc_conv_b, na_rpb, w_out, ffn_w_up, ffn_conv_w, ffn_conv_b, ffn_w_down, final_norm_g):
    bsz, n_lat, d = x.shape
    n_ctx = ctx.shape[1]
    depth = w_in.shape[0]
    rows = n_lat // GRID_W
    assert n_lat % (NA_Q_ROWS * GRID_W) == 0 and rows >= NA_K_ROWS and bsz + 1 <= 2 * SUBLANES
    tm_lat, tm_ctx = _tile_rows(n_lat, 512), _tile_rows(n_ctx, 512)

    crows = jnp.zeros((2 * SUBLANES, d), F32).at[:bsz].set(c).at[bsz].set(c_ctx)
    mod_all = _ada_call(crows, w_ada, b_ada)

    zpad = lambda n: jnp.zeros((depth, d, n), F32)
    swap = np.arange(GLA_QK) ^ 1
    o = np.cumsum([0, GLA_QK, GLA_QK, GLA_V, GLA_LOWRANK, GLA_LOWRANK, GLA_V,
                   SC_WIDTH, SC_WIDTH, SC_WIDTH, NA_W, NA_W, NA_W])
    col = lambda n: w_in[:, :, o[n]:o[n + 1]]
    wq, wk = col(0), col(1)
    qpad = zpad(QK_PAD - GLA_QK)
    wqk_ctx = jnp.concatenate([wq, qpad, wk, qpad], axis=-1).astype(BF16)
    wqk_lat = jnp.concatenate([wq, qpad, wk, qpad, wq[:, :, swap], qpad, wk[:, :, swap], qpad], axis=-1).astype(BF16)
    wvr = jnp.concatenate([col(2), col(5), col(3), col(4), zpad(GLR_PAD - 2 * GLA_LOWRANK)], axis=-1).astype(BF16)
    wsc = w_in[:, :, o[6]:o[9]].astype(BF16)
    wna = jnp.concatenate([col(9) * (NA_DH ** -0.5), col(10), col(11)], axis=-1).astype(BF16)
    w_out_b = w_out.astype(BF16)
    wo_gla, wo_sc, wo_na = w_out_b[:, :GLA_V], w_out_b[:, GLA_V:GLA_V + SC_WIDTH], w_out_b[:, GLA_V + SC_WIDTH:]
    wup = ffn_w_up.astype(BF16)
    wdn = ffn_w_down.astype(BF16)

    def gate_w(w2, first_row):
        full = jnp.zeros((depth, GLR_PAD, QK_PAD), F32)
        return full.at[:, first_row:first_row + GLA_LOWRANK, :GLA_QK].set(w2).astype(BF16)

    def gate_b(bias):
        return jnp.zeros((depth, 1, QK_PAD), F32).at[:, 0, :GLA_QK].set(bias)

    wg_f, wg_b = gate_w(gla_wg2_fw, 0), gate_w(gla_wg2_bw, GLA_LOWRANK)
    bg_f, bg_b = gate_b(gla_bg_fw), gate_b(gla_bg_bw)
    gla_gain = jnp.tile(gla_norm_g, (1, GLA_HEADS))[:, None, :]
    hv = np.arange(GLA_V) // GLA_DV
    head_mean = jnp.asarray((hv[:, None] == hv[None, :]) / GLA_DV, BF16)
    tabs_f, tabs_b = _gla_tables(False), _gla_tables(True)
    rope_tabs = _rope_tables(n_lat)
    zero_state = jnp.zeros((bsz, GLA_V, QK_PAD // 2), F32)
    na_bias = _na_bias_tables(na_rpb, rows)

    xc = ctx
    for layer in range(depth):
        update_ctx = layer < depth - 1
        mod = mod_all[layer]
        sh1, sc1, g1, sh2, sc2, g2 = [mod[:bsz, n * d:(n + 1) * d][:, None, :] for n in range(6)]
        sh1c, sc1c, g1c, sh2c, sc2c, g2c = [mod[bsz:bsz + 1, n * d:(n + 1) * d][:, None, :] for n in range(6)]
        gain_mix = norm_mix_g[layer][None, :]
        gain_ffn = norm_ffn_g[layer][None, :]
        wts = (wvr[layer], wsc[layer], wna[layer])

        cq, ck, cv, cr, cglr, cscu, cnaq, cnak, cnav = _inproj_call(
            xc, sh1c, sc1c, gain_mix, wqk_ctx[layer], *wts, None, tm_ctx)
        lq, lk, lv, lr, lglr, lscu, lnaq, lnak, lnav = _inproj_call(
            x, sh1, sc1, gain_mix, wqk_lat[layer], *wts, rope_tabs, tm_lat)

        fin = lambda r, o_f: (r, o_f, gla_gain[layer], head_mean)
        oc_f, state_f = _gla_call(cq, ck, cv, cglr, wg_f[layer], bg_f[layer], tabs_f, zero_state, tm_ctx, False)
        gla_ctx, state_b = _gla_call(cq, ck, cv, cglr, wg_b[layer], bg_b[layer], tabs_b, zero_state, tm_ctx, True,
                                     fin(cr, oc_f))
        ol_f, _ = _gla_call(lq, lk, lv, lglr, wg_f[layer], bg_f[layer], tabs_f, state_f, tm_lat, False)
        gla_lat, _ = _gla_call(lq, lk, lv, lglr, wg_b[layer], bg_b[layer], tabs_b, state_b, tm_lat, True,
                               fin(lr, ol_f))

        na_lat = _na_call(lnaq, lnak, lnav, cnak, cnav, na_bias[layer])

        cw_sc, cb_sc = sc_conv_w[layer], sc_conv_b[layer][None, :]
        wo = (wo_gla[layer], wo_sc[layer], wo_na[layer])
        x = _outproj_call(gla_lat, na_lat, lscu, x, g1, *wo, cw_sc, cb_sc, tm_lat)
        ffn = (wup[layer], ffn_conv_w[layer], ffn_conv_b[layer][None, :], wdn[layer])
        x = _ffn_call(x, sh2, sc2, g2, gain_ffn, *ffn,
                      final_norm_g[None, :] if layer == depth - 1 else None, tm_lat)

        if update_ctx:
            na_ctx = _ctx_attn_call(cnaq, cnak, cnav)
            xc = _outproj_call(gla_ctx, na_ctx, cscu, xc, g1c, *wo, cw_sc, cb_sc, tm_ctx)
            xc = _ffn_call(xc, sh2c, sc2c, g2c, gain_ffn, *ffn, None, tm_ctx)
    return x
```

```python
import functools

import numpy as np
import jax
import jax.numpy as jnp
from jax import lax
from jax.experimental import pallas as pl
from jax.experimental.pallas import tpu as pltpu

F32 = jnp.float32
BF16 = jnp.bfloat16

GRID_W = 64
EPS = 1e-6
NEG_INF = -1e30
GLA_HEADS = 6
GLA_DK = 32
GLA_DV = 64
GLA_LOWRANK = 16
GLA_TAU = 16.0
ROPE_BASE = 10000.0
SC_WIDTH = 256
NA_HEADS = 6
NA_DH = 64
NA_WIN_ROWS = 8
NA_WIN_COLS = 16
GLA_QK = GLA_HEADS * GLA_DK
GLA_V = GLA_HEADS * GLA_DV
NA_W = NA_HEADS * NA_DH

LANES = 128
SUBLANES = 8
VMEM_LIMIT_BYTES = 56 * 1024 * 1024

QK_PAD = 2 * LANES
GLR_PAD = LANES
GLA_GROUP = 128
GLA_LEVELS = 4
HEADS_A = LANES // GLA_DK
GLA_SINGLE_LEVEL_DECAY = 64.0
NA_Q_ROWS = 4
NA_K_ROWS = 12

NT_DIMS = (((1,), (1,)), ((), ()))


def _cparams(*sem):
    return pltpu.CompilerParams(dimension_semantics=sem, vmem_limit_bytes=VMEM_LIMIT_BYTES)


def _silu(t):
    return t / (1.0 + jnp.exp(-t))


def _norm_mod(t, gain, shift, scale):
    ms = jnp.mean(t * t, axis=-1, keepdims=True)
    return (t * lax.rsqrt(ms + EPS) * gain) * (1.0 + scale) + shift


def _ada_kernel(c_ref, w_ref, b_ref, o_ref):
    c = c_ref[...]
    o_ref[0] = jnp.dot(_silu(c), w_ref[0], preferred_element_type=F32,
                       precision=lax.Precision.HIGHEST) + b_ref[0]


def _ada_call(crows, w_ada, b_ada):
    depth, d, n = w_ada.shape
    rows = crows.shape[0]
    tn = 1536
    return pl.pallas_call(
        _ada_kernel,
        grid=(depth, n // tn),
        in_specs=[pl.BlockSpec((rows, d), lambda l, j: (0, 0)),
                  pl.BlockSpec((1, d, tn), lambda l, j: (l, 0, j)),
                  pl.BlockSpec((1, 1, tn), lambda l, j: (l, 0, j))],
        out_specs=pl.BlockSpec((1, rows, tn), lambda l, j: (l, 0, j)),
        out_shape=jax.ShapeDtypeStruct((depth, rows, n), F32),
        compiler_params=_cparams("arbitrary", "arbitrary"),
        name="ada_mod",
    )(crows, w_ada, b_ada.reshape(depth, 1, n))


INPROJ_OUTS = ((QK_PAD, F32), (QK_PAD, F32), (GLA_V, F32), (GLA_V, F32), (GLR_PAD, F32),
               (3 * SC_WIDTH, F32), (NA_W, BF16), (NA_W, BF16), (NA_W, BF16))
INPROJ_STARTS = tuple(int(s) for s in np.cumsum([0] + [w for w, _ in INPROJ_OUTS]))
INPROJ_WIDTH = INPROJ_STARTS[-1]
MXU_COLS = 2 * LANES


def _rotary(t, c, s):
    lane = lax.broadcasted_iota(jnp.int32, t.shape, 1)
    swapped = jnp.where(lane % 2 == 0, pltpu.roll(t, LANES - 1, 1), pltpu.roll(t, 1, 1))
    return t * c + swapped * s


def _inproj_kernel(*refs, rope):
    if rope:
        x_ref, sh_ref, sc_ref, g_ref, cos_ref, sin_ref, w_ref = refs[:7]
        outs = refs[7:]
    else:
        x_ref, sh_ref, sc_ref, g_ref, w_ref = refs[:5]
        outs = refs[5:]
    hb = _norm_mod(x_ref[0], g_ref[...], sh_ref[0], sc_ref[0]).astype(BF16)
    for n0 in range(0, INPROJ_WIDTH, MXU_COLS):
        res = jnp.dot(hb, w_ref[:, n0:n0 + MXU_COLS], preferred_element_type=F32)
        for a in range(n0, n0 + MXU_COLS, LANES):
            idx = max(i for i, s0 in enumerate(INPROJ_STARTS[:-1]) if s0 <= a)
            off = a - INPROJ_STARTS[idx]
            piece = res[:, a - n0:a - n0 + LANES]
            if idx < 2:
                if rope:
                    piece = _rotary(piece, cos_ref[:, off:off + LANES], sin_ref[:, off:off + LANES])
                if idx == 0:
                    piece = piece * (GLA_DK ** -0.5)
            outs[idx][0, :, off:off + LANES] = piece.astype(outs[idx].dtype)


def _inproj_call(x, shift, scale, gain, w_all, rope_tabs, tm):
    b, l, d = x.shape
    bm = shift.shape[0]
    rope = rope_tabs is not None
    mod_map = (lambda i, bb: (bb, 0, 0)) if bm > 1 else (lambda i, bb: (0, 0, 0))
    tok = lambda w: pl.BlockSpec((1, tm, w), lambda i, bb: (bb, i, 0))
    full = lambda a: pl.BlockSpec(a.shape, lambda i, bb: (0,) * a.ndim)
    in_specs = [tok(d), pl.BlockSpec((1, 1, d), mod_map), pl.BlockSpec((1, 1, d), mod_map), full(gain)]
    args = [x, shift, scale, gain]
    if rope:
        in_specs += [pl.BlockSpec((tm, QK_PAD), lambda i, bb: (i, 0))] * 2
        args += list(rope_tabs)
    in_specs.append(full(w_all))
    args.append(w_all)
    return pl.pallas_call(
        functools.partial(_inproj_kernel, rope=rope),
        grid=(l // tm, b),
        in_specs=in_specs,
        out_specs=[tok(w) for w, _ in INPROJ_OUTS],
        out_shape=[jax.ShapeDtypeStruct((b, l, w), dt) for w, dt in INPROJ_OUTS],
        compiler_params=_cparams("arbitrary", "arbitrary"),
        name="in_proj_rope" if rope else "in_proj_ctx",
    )(*args)


def _gla_tables(reverse):
    t = np.arange(GLA_GROUP)
    i, j = t[:, None], t[None, :]
    before = (j >= i) if reverse else (j <= i)
    strictly_after = (j < i) if reverse else (j > i)
    maps = []
    for n in (16, 32, 64, 128):
        maps.append((i // n == j // n) & before)
    for n in (16, 32, 64, 128):
        maps.append((i // n == j // n) & strictly_after)
    masks = [(i // 16 == j // 16) & before]
    for n in (32, 64, 128):
        h = n // 2
        q_late, k_early = (i % n) >= h, (j % n) < h
        if reverse:
            q_late, k_early = (i % n) < h, (j % n) >= h
        masks.append((i // n == j // n) & q_late & k_early)
    cm = np.concatenate(maps, axis=0).astype(np.float32)
    cm2 = np.concatenate([maps[3], maps[7]], axis=0).astype(np.float32)
    lm = np.stack(masks, axis=0).astype(np.float32)
    return jnp.asarray(cm, BF16), jnp.asarray(cm2, BF16), jnp.asarray(lm, F32)


def _gla_kernel(*refs, reverse, finalize, ngroups):
    if finalize:
        (q_ref, k_ref, v_ref, glr_ref, wg_ref, bg_ref, cm_ref, cm2_ref, lm_ref, s0_ref,
         r_ref, of_ref, ng_ref, bd_ref, out_ref, sfin_ref, st_ref, gate_ref) = refs
    else:
        (q_ref, k_ref, v_ref, glr_ref, wg_ref, bg_ref, cm_ref, cm2_ref, lm_ref, s0_ref,
         out_ref, sfin_ref, st_ref, gate_ref) = refs
    step = pl.program_id(1)
    g = GLA_GROUP
    kc = QK_PAD // 2
    vc = HEADS_A * GLA_DV

    @pl.when(step == 0)
    def _():
        st_ref[...] = s0_ref[0]

    z = jnp.dot(glr_ref[0].astype(BF16), wg_ref[...], preferred_element_type=F32) + bg_ref[...]
    gate_all = (jnp.minimum(z, 0.0) - jnp.log(1.0 + jnp.exp(-jnp.abs(z)))) * (1.0 / GLA_TAU)
    gate_ref[...] = gate_all
    total = jnp.sum(gate_all.reshape(ngroups, g, QK_PAD), axis=1)
    single_level = jnp.min(total) > -GLA_SINGLE_LEVEL_DECAY

    k_head = lax.broadcasted_iota(jnp.int32, (g, kc), 1) // GLA_DK
    kmask = [k_head == h for h in range(HEADS_A)]
    va_head = lax.broadcasted_iota(jnp.int32, (g, vc), 1) // GLA_DV
    vb_head = lax.broadcasted_iota(jnp.int32, (g, GLA_V - vc), 1) // GLA_DV
    st_row = lax.broadcasted_iota(jnp.int32, (GLA_V, kc), 0)
    st_col = lax.broadcasted_iota(jnp.int32, (GLA_V, kc), 1) // GLA_DK
    state_mask = st_row // GLA_DV == st_col + jnp.where(st_row >= vc, HEADS_A, 0)
    lmask = [lm_ref[l] > 0.5 for l in range(GLA_LEVELS)]
    causal = functools.reduce(jnp.logical_or, lmask)

    def stack_heads(t, masks, n):
        return jnp.concatenate([jnp.where(masks[h], t, 0.0) for h in range(n)], axis=0).astype(BF16)

    def scores(qs, ks):
        sa = lax.dot_general(qs[:, :kc].astype(BF16), stack_heads(ks[:, :kc], kmask, HEADS_A), NT_DIMS,
                             preferred_element_type=F32)
        sb = lax.dot_general(qs[:, kc:].astype(BF16), stack_heads(ks[:, kc:], kmask, GLA_HEADS - HEADS_A),
                             NT_DIMS, preferred_element_type=F32)
        return jnp.concatenate([sa, sb], axis=1)

    def group(gi, robust):
        gidx = (ngroups - 1 - gi) if reverse else gi
        rows = pl.ds(gidx * g, g)
        q = q_ref[0, rows, :]
        k = k_ref[0, rows, :]
        v = v_ref[0, rows, :]
        gate = gate_ref[rows, :]
        ghi = gate.astype(BF16)
        glo = (gate - ghi.astype(F32)).astype(BF16)
        cmat = cm_ref[...] if robust else cm2_ref[...]
        cs = jnp.dot(cmat, ghi, preferred_element_type=F32) + jnp.dot(cmat, glo, preferred_element_type=F32)
        if robust:
            p16, p32, p64, p128, s16, s32, s64, s128 = [cs[n * g:(n + 1) * g] for n in range(8)]
            q0 = q * jnp.exp(p16)
            a = [scores(q0, k * jnp.exp(-p16)),
                 scores(q0, k * jnp.exp(s16)),
                 scores(q * jnp.exp(p32), k * jnp.exp(s32)),
                 scores(q * jnp.exp(p64), k * jnp.exp(s64))]
            blocks = []
            for h in range(GLA_HEADS):
                sl = slice(h * g, (h + 1) * g)
                blk = jnp.where(lmask[3], a[3][:, sl], 0.0)
                for l in (2, 1, 0):
                    blk = jnp.where(lmask[l], a[l][:, sl], blk)
                blocks.append(blk)
            q4 = q * jnp.exp(p128)
        else:
            p128, s128 = cs[0:g], cs[g:2 * g]
            q4 = q * jnp.exp(p128)
            a = scores(q4, k * jnp.exp(-p128))
            blocks = [jnp.where(causal, a[:, h * g:(h + 1) * g], 0.0) for h in range(GLA_HEADS)]
        amat_a = jnp.concatenate(blocks[:HEADS_A], axis=1).astype(BF16)
        amat_b = jnp.concatenate(blocks[HEADS_A:], axis=1).astype(BF16)
        v_a = stack_heads(v[:, :vc], [va_head == h for h in range(HEADS_A)], HEADS_A)
        v_b = stack_heads(v[:, vc:], [vb_head == h for h in range(GLA_HEADS - HEADS_A)], GLA_HEADS - HEADS_A)
        st = st_ref[...]
        stb = st.astype(BF16)
        q4b = q4.astype(BF16)
        o_a = (jnp.dot(amat_a, v_a, preferred_element_type=F32)
               + lax.dot_general(q4b[:, :kc], stb[:vc], NT_DIMS, preferred_element_type=F32))
        o_b = (jnp.dot(amat_b, v_b, preferred_element_type=F32)
               + lax.dot_general(q4b[:, kc:], stb[vc:], NT_DIMS, preferred_element_type=F32))
        o = jnp.concatenate([o_a, o_b], axis=1)
        vt = v.T.astype(BF16)
        k4 = (k * jnp.exp(s128)).astype(BF16)
        upd = jnp.concatenate([jnp.dot(vt[:vc], k4[:, :kc], preferred_element_type=F32),
                               jnp.dot(vt[vc:], k4[:, kc:], preferred_element_type=F32)], axis=0)
        decay = jnp.exp(p128[0:1] + s128[0:1])
        dec = jnp.concatenate([jnp.broadcast_to(decay[:, :kc], (vc, kc)),
                               jnp.broadcast_to(decay[:, kc:], (GLA_V - vc, kc))], axis=0)
        st_ref[...] = st * dec + jnp.where(state_mask, upd, 0.0)
        if finalize:
            ot = o + of_ref[0, rows, :]
            o2 = ot * ot
            hi = o2.astype(BF16)
            lo = (o2 - hi.astype(F32)).astype(BF16)
            ms = (jnp.dot(hi, bd_ref[...], preferred_element_type=F32)
                  + jnp.dot(lo, bd_ref[...], preferred_element_type=F32))
            y = ot * lax.rsqrt(ms + EPS) * ng_ref[...] * _silu(r_ref[0, rows, :])
            out_ref[0, rows, :] = y.astype(out_ref.dtype)
        else:
            out_ref[0, rows, :] = o

    @pl.when(single_level)
    def _():
        for gi in range(ngroups):
            group(gi, False)

    @pl.when(jnp.logical_not(single_level))
    def _():
        for gi in range(ngroups):
            group(gi, True)

    @pl.when(step == pl.num_programs(1) - 1)
    def _():
        sfin_ref[0] = st_ref[...]


def _gla_call(q, k, v, glr, wg, bg, tabs, s0, ts, reverse, fin=None):
    b, l, _ = q.shape
    nb = l // ts
    finalize = fin is not None
    cm, cm2, lm = tabs
    blk = (lambda bb, i: (bb, nb - 1 - i, 0)) if reverse else (lambda bb, i: (bb, i, 0))
    tok = lambda w: pl.BlockSpec((1, ts, w), blk)
    full = lambda a: pl.BlockSpec(a.shape, lambda bb, i: (0,) * a.ndim)
    state = pl.BlockSpec((1, GLA_V, QK_PAD // 2), lambda bb, i: (bb, 0, 0))
    in_specs = [tok(QK_PAD), tok(QK_PAD), tok(GLA_V), tok(GLR_PAD), full(wg), full(bg), full(cm), full(cm2),
                full(lm), state]
    args = [q, k, v, glr, wg, bg, cm, cm2, lm, s0]
    if finalize:
        r, o_other, ng, bd = fin
        in_specs += [tok(GLA_V), tok(GLA_V), full(ng), full(bd)]
        args += [r, o_other, ng, bd]
    out_dt = BF16 if finalize else F32
    return pl.pallas_call(
        functools.partial(_gla_kernel, reverse=reverse, finalize=finalize, ngroups=ts // GLA_GROUP),
        grid=(b, nb),
        in_specs=in_specs,
        out_specs=[tok(GLA_V), state],
        out_shape=[jax.ShapeDtypeStruct((b, l, GLA_V), out_dt),
                   jax.ShapeDtypeStruct((b, GLA_V, QK_PAD // 2), F32)],
        scratch_shapes=[pltpu.VMEM((GLA_V, QK_PAD // 2), F32), pltpu.VMEM((ts, QK_PAD), F32)],
        compiler_params=_cparams("arbitrary", "arbitrary"),
        name="gla_bwd" if reverse else "gla_fwd",
    )(*args)


def _na_bias_tables(rpb, rows):
    kh = min(NA_WIN_ROWS, rows)
    n_dr, n_dc = 2 * NA_WIN_ROWS - 1, 2 * NA_WIN_COLS - 1
    qr, kr = np.arange(NA_Q_ROWS)[:, None], np.arange(NA_K_ROWS)[None, :]
    row_pick, row_ok = [], []
    for r0, ks in ((0, 0), (NA_WIN_ROWS // 2, 0), (rows - NA_Q_ROWS, rows - NA_K_ROWS)):
        r, r2 = r0 + qr, ks + kr
        rs = np.clip(r - kh // 2, 0, rows - kh)
        row_ok.append((r2 >= rs) & (r2 < rs + kh))
        row_pick.append(np.eye(n_dr, dtype=np.float32)[np.clip(r2 - r + NA_WIN_ROWS - 1, 0, n_dr - 1)])
    row_pick, row_ok = np.stack(row_pick), np.stack(row_ok)
    qc, kc = np.arange(GRID_W)[:, None], np.arange(GRID_W)[None, :]
    cs = np.clip(qc - NA_WIN_COLS // 2, 0, GRID_W - NA_WIN_COLS)
    col_ok = (kc >= cs) & (kc < cs + NA_WIN_COLS)
    col_pick = np.eye(n_dc, dtype=np.float32)[np.clip(kc - qc + NA_WIN_COLS - 1, 0, n_dc - 1)]
    hi = lax.Precision.HIGHEST
    by_row = jnp.einsum('lhde,cqkd->lchqke', rpb.astype(F32), row_pick, precision=hi)
    bias = jnp.einsum('lchqke,xye->lchqxky', by_row, col_pick, precision=hi)
    valid = row_ok[:, None, :, None, :, None] & col_ok[None, None, None, :, None, :]
    bias = jnp.where(valid[None], bias, NEG_INF)
    return bias.reshape(rpb.shape[0], 3, NA_HEADS, NA_Q_ROWS * GRID_W, NA_K_ROWS * GRID_W)


def _attend(qh, parts):
    s = []
    for kh, _, bias in parts:
        sc = lax.dot_general(qh, kh, NT_DIMS, preferred_element_type=F32)
        s.append(sc if bias is None else sc + bias)
    m = functools.reduce(jnp.maximum, [jnp.max(t, axis=-1, keepdims=True) for t in s])
    p = [jnp.exp(t - m) for t in s]
    denom = functools.reduce(jnp.add, [jnp.sum(t, axis=-1, keepdims=True) for t in p])
    o = functools.reduce(jnp.add, [jnp.dot(t.astype(BF16), vh, preferred_element_type=F32)
                                   for t, (_, vh, _) in zip(p, parts)])
    return o / denom


def _mix_residual(gla, na, u, u_prev, u_next, x, g1, wg, ws, wn, cw, cb, first, last):
    w = SC_WIDTH
    tm = u.shape[0]
    cx = u[:, w:2 * w] * u[:, 2 * w:3 * w]
    cx_prev = (u_prev[SUBLANES - 1:SUBLANES, w:2 * w] * u_prev[SUBLANES - 1:SUBLANES, 2 * w:3 * w]
               * jnp.where(first, 0.0, 1.0))
    cx_next = u_next[0:1, w:2 * w] * u_next[0:1, 2 * w:3 * w] * jnp.where(last, 0.0, 1.0)
    row = lax.broadcasted_iota(jnp.int32, cx.shape, 0)
    prev = jnp.where(row == 0, cx_prev, pltpu.roll(cx, 1, 0))
    nxt = jnp.where(row == tm - 1, cx_next, pltpu.roll(cx, tm - 1, 0))
    sc = u[:, 0:w] * (prev * cw[0:1] + cx * cw[1:2] + nxt * cw[2:3] + cb)
    mix = (jnp.dot(gla, wg, preferred_element_type=F32)
           + jnp.dot(sc.astype(BF16), ws, preferred_element_type=F32)
           + jnp.dot(na, wn, preferred_element_type=F32))
    return x + g1 * mix


def _na_kernel(q_ref, k_ref, v_ref, kc_ref, vc_ref, bias_ref, gla_ref, scu_ref, scp_ref, scn_ref, x_ref, g1_ref,
               wg_ref, ws_ref, wn_ref, cw_ref, cb_ref, o_ref, *, rows):
    i = pl.program_id(1)
    ks = jnp.clip(i * NA_Q_ROWS - NA_WIN_ROWS // 2, 0, rows - NA_K_ROWS)
    win = pl.ds(pl.multiple_of(ks * GRID_W, GRID_W), NA_K_ROWS * GRID_W)
    q = q_ref[0]
    kw = k_ref[0, win, :]
    vw = v_ref[0, win, :]
    kc = kc_ref[0]
    vc = vc_ref[0]
    outs = []
    for h in range(NA_HEADS):
        sl = slice(h * NA_DH, (h + 1) * NA_DH)
        outs.append(_attend(q[:, sl], [(kw[:, sl], vw[:, sl], bias_ref[0, h]), (kc[:, sl], vc[:, sl], None)]))
    na = jnp.concatenate(outs, axis=1).astype(BF16)
    o_ref[0] = _mix_residual(gla_ref[0], na, scu_ref[0], scp_ref[0], scn_ref[0], x_ref[0], g1_ref[0],
                             wg_ref[...], ws_ref[...], wn_ref[...], cw_ref[...], cb_ref[...],
                             i == 0, i == pl.num_programs(1) - 1)


def _na_call(q, k, v, kc, vc, bias, gla, scu, x, g1, wg, ws, wn, cw, cb):
    b, l, w = q.shape
    d = x.shape[-1]
    c = kc.shape[1]
    rows = l // GRID_W
    nrb = rows // NA_Q_ROWS
    nq, nk = NA_Q_ROWS * GRID_W, NA_K_ROWS * GRID_W
    nblk8, t8 = l // SUBLANES, nq // SUBLANES
    seq = lambda n: pl.BlockSpec((1, n, w), lambda bb, i: (bb, 0, 0))
    tok = lambda width: pl.BlockSpec((1, nq, width), lambda bb, i: (bb, i, 0))
    full = lambda a: pl.BlockSpec(a.shape, lambda bb, i: (0,) * a.ndim)
    case = lambda bb, i: (jnp.where(i == 0, 0, jnp.where(i == nrb - 1, 2, 1)), 0, 0, 0)
    halo_prev = pl.BlockSpec((1, SUBLANES, 3 * SC_WIDTH), lambda bb, i: (bb, jnp.maximum(i * t8 - 1, 0), 0))
    halo_next = pl.BlockSpec((1, SUBLANES, 3 * SC_WIDTH), lambda bb, i: (bb, jnp.minimum((i + 1) * t8, nblk8 - 1), 0))
    return pl.pallas_call(
        functools.partial(_na_kernel, rows=rows),
        grid=(b, nrb),
        in_specs=[tok(w), seq(l), seq(l), seq(c), seq(c), pl.BlockSpec((1, NA_HEADS, nq, nk), case),
                  tok(GLA_V), tok(3 * SC_WIDTH), halo_prev, halo_next, tok(d),
                  pl.BlockSpec((1, 1, d), lambda bb, i: (bb, 0, 0)),
                  full(wg), full(ws), full(wn), full(cw), full(cb)],
        out_specs=tok(d),
        out_shape=jax.ShapeDtypeStruct((b, l, d), F32),
        compiler_params=_cparams("arbitrary", "arbitrary"),
        name="na_out_proj",
    )(q, k, v, kc, vc, bias, gla, scu, scu, scu, x, g1, wg, ws, wn, cw, cb)


def _ctx_attn_kernel(q_ref, k_ref, v_ref, o_ref):
    q, k, v = q_ref[0], k_ref[0], v_ref[0]
    outs = []
    for h in range(NA_HEADS):
        sl = slice(h * NA_DH, (h + 1) * NA_DH)
        outs.append(_attend(q[:, sl], [(k[:, sl], v[:, sl], None)]))
    o_ref[0] = jnp.concatenate(outs, axis=1).astype(o_ref.dtype)


def _ctx_attn_call(q, k, v):
    b, c, w = q.shape
    spec = pl.BlockSpec((1, c, w), lambda bb: (bb, 0, 0))
    return pl.pallas_call(
        _ctx_attn_kernel, grid=(b,), in_specs=[spec, spec, spec], out_specs=spec,
        out_shape=jax.ShapeDtypeStruct((b, c, w), BF16),
        compiler_params=_cparams("arbitrary"), name="ctx_attn",
    )(q, k, v)


def _outproj_kernel(gla_ref, na_ref, scu_ref, scp_ref, scn_ref, x_ref, g1_ref,
                    wg_ref, ws_ref, wn_ref, cw_ref, cb_ref, o_ref):
    i = pl.program_id(1)
    o_ref[0] = _mix_residual(gla_ref[0], na_ref[0], scu_ref[0], scp_ref[0], scn_ref[0], x_ref[0], g1_ref[0],
                             wg_ref[...], ws_ref[...], wn_ref[...], cw_ref[...], cb_ref[...],
                             i == 0, i == pl.num_programs(1) - 1)


def _outproj_call(gla, na, scu, x, g1, wg, ws, wn, cw, cb, tm):
    b, l, d = x.shape
    bm = g1.shape[0]
    nblk8 = l // SUBLANES
    t8 = tm // SUBLANES
    mod_map = (lambda bb, i: (bb, 0, 0)) if bm > 1 else (lambda bb, i: (0, 0, 0))
    tok = lambda w: pl.BlockSpec((1, tm, w), lambda bb, i: (bb, i, 0))
    full = lambda a: pl.BlockSpec(a.shape, lambda bb, i: (0,) * a.ndim)
    halo_prev = pl.BlockSpec((1, SUBLANES, 3 * SC_WIDTH), lambda bb, i: (bb, jnp.maximum(i * t8 - 1, 0), 0))
    halo_next = pl.BlockSpec((1, SUBLANES, 3 * SC_WIDTH), lambda bb, i: (bb, jnp.minimum((i + 1) * t8, nblk8 - 1), 0))
    return pl.pallas_call(
        _outproj_kernel,
        grid=(b, l // tm),
        in_specs=[tok(GLA_V), tok(NA_W), tok(3 * SC_WIDTH), halo_prev, halo_next, tok(d),
                  pl.BlockSpec((1, 1, d), mod_map), full(wg), full(ws), full(wn), full(cw), full(cb)],
        out_specs=tok(d),
        out_shape=jax.ShapeDtypeStruct((b, l, d), F32),
        compiler_params=_cparams("arbitrary", "arbitrary"),
        name="out_proj",
    )(gla, na, scu, scu, scu, x, g1, wg, ws, wn, cw, cb)


def _ffn_kernel(*refs, final, chunk):
    if final:
        (x_ref, xp_ref, xn_ref, sh_ref, sc_ref, g2_ref, ng_ref, wup_ref, cw_ref, cb_ref, wdn_ref, fg_ref,
         o_ref, ua_ref, ub_ref, act_ref) = refs
    else:
        (x_ref, xp_ref, xn_ref, sh_ref, sc_ref, g2_ref, ng_ref, wup_ref, cw_ref, cb_ref, wdn_ref,
         o_ref, ua_ref, ub_ref, act_ref) = refs
    i = pl.program_id(1)
    last = pl.num_programs(1) - 1
    x = x_ref[0]
    tm = x.shape[0]
    hid = wdn_ref.shape[0]
    nm = lambda t: _norm_mod(t, ng_ref[...], sh_ref[0], sc_ref[0])
    hp = nm(xp_ref[0]) * jnp.where(i > 0, 1.0, 0.0)
    hn = nm(xn_ref[0]) * jnp.where(i < last, 1.0, 0.0)
    hb = jnp.concatenate([hp, nm(x), hn], axis=0).astype(BF16)

    def conv(u_ref, slot, c0):
        cw = cw_ref[:, c0:c0 + chunk]
        return (u_ref[slot, pl.ds(SUBLANES - 1, tm), :] * cw[0:1] + u_ref[slot, pl.ds(SUBLANES, tm), :] * cw[1:2]
                + u_ref[slot, pl.ds(SUBLANES + 1, tm), :] * cw[2:3] + cb_ref[:, c0:c0 + chunk])

    for j in range(hid // chunk):
        c0 = j * chunk
        slot = j % 2
        ua_ref[slot] = jnp.dot(hb, wup_ref[:, c0:c0 + chunk], preferred_element_type=F32)
        ub_ref[slot] = jnp.dot(hb, wup_ref[:, hid + c0:hid + c0 + chunk], preferred_element_type=F32)
        act_ref[:, c0:c0 + chunk] = (_silu(conv(ua_ref, slot, c0)) * conv(ub_ref, slot, hid + c0)).astype(BF16)
    y = x + g2_ref[0] * jnp.dot(act_ref[...], wdn_ref[...], preferred_element_type=F32)
    if final:
        ms = jnp.mean(y * y, axis=-1, keepdims=True)
        y = y * lax.rsqrt(ms + EPS) * fg_ref[...]
    o_ref[0] = y


def _ffn_call(x, shift, scale, g2, gain, wup, cw, cb, wdn, final_gain, tm):
    b, l, d = x.shape
    bm = shift.shape[0]
    nblk8 = l // SUBLANES
    t8 = tm // SUBLANES
    final = final_gain is not None
    mod_map = (lambda bb, i: (bb, 0, 0)) if bm > 1 else (lambda bb, i: (0, 0, 0))
    tok = pl.BlockSpec((1, tm, d), lambda bb, i: (bb, i, 0))
    full = lambda a: pl.BlockSpec(a.shape, lambda bb, i: (0,) * a.ndim)
    resident = lambda a: pl.BlockSpec(a.shape, lambda bb, i: (0,) * a.ndim, pipeline_mode=pl.Buffered(1))
    halo_prev = pl.BlockSpec((1, SUBLANES, d), lambda bb, i: (bb, jnp.maximum(i * t8 - 1, 0), 0))
    halo_next = pl.BlockSpec((1, SUBLANES, d), lambda bb, i: (bb, jnp.minimum((i + 1) * t8, nblk8 - 1), 0))
    mod = pl.BlockSpec((1, 1, d), mod_map)
    in_specs = [tok, halo_prev, halo_next, mod, mod, mod, full(gain), resident(wup), full(cw), full(cb), resident(wdn)]
    args = [x, x, x, shift, scale, g2, gain, wup, cw, cb, wdn]
    if final:
        in_specs.append(full(final_gain))
        args.append(final_gain)
    chunk = 2 * LANES
    return pl.pallas_call(
        functools.partial(_ffn_kernel, final=final, chunk=chunk),
        grid=(b, l // tm),
        in_specs=in_specs,
        out_specs=tok,
        out_shape=jax.ShapeDtypeStruct((b, l, d), F32),
        scratch_shapes=[pltpu.VMEM((2, tm + 2 * SUBLANES, chunk), F32),
                        pltpu.VMEM((2, tm + 2 * SUBLANES, chunk), F32),
                        pltpu.VMEM((tm, wdn.shape[0]), BF16)],
        compiler_params=_cparams("arbitrary", "arbitrary"),
        name="conv_ffn_final" if final else "conv_ffn",
    )(*args)


def _rope_tables(n_tokens):
    t = jnp.arange(n_tokens)
    n_freq = GLA_DK // 4
    inv_freq = ROPE_BASE ** (-jnp.arange(n_freq, dtype=F32) / n_freq)
    row = (t // GRID_W).astype(F32)[:, None] * inv_freq
    col = (t % GRID_W).astype(F32)[:, None] * inv_freq
    ang = jnp.concatenate([row, col], axis=-1)
    cos = jnp.repeat(jnp.cos(ang), 2, axis=-1)
    sin = jnp.repeat(jnp.sin(ang), 2, axis=-1) * jnp.tile(jnp.asarray([-1.0, 1.0], F32), GLA_DK // 2)
    pad = QK_PAD - GLA_QK
    cos = jnp.concatenate([jnp.tile(cos, (1, GLA_HEADS)), jnp.ones((n_tokens, pad), F32)], axis=-1)
    sin = jnp.concatenate([jnp.tile(sin, (1, GLA_HEADS)), jnp.zeros((n_tokens, pad), F32)], axis=-1)
    return cos, sin


def _tile_rows(n, target):
    t = min(n, target)
    assert n % t == 0 and t % GLA_GROUP == 0, (n, t)
    return t


def kernel(x, c, ctx, c_ctx, w_ada, b_ada, norm_mix_g, norm_ffn_g, w_in, gla_wg2_fw, gla_bg_fw, gla_wg2_bw, gla_bg_bw, gla_norm_g, sc_conv_w, sc_conv_b, na_rpb, w_out, ffn_w_up, ffn_conv_w, ffn_conv_b, ffn_w_down, final_norm_g):
    bsz, n_lat, d = x.shape
    n_ctx = ctx.shape[1]
    depth = w_in.shape[0]
    rows = n_lat // GRID_W
    assert n_lat % (NA_Q_ROWS * GRID_W) == 0 and rows >= NA_K_ROWS and bsz + 1 <= 2 * SUBLANES
    tm_lat, tm_ctx = _tile_rows(n_lat, 512), _tile_rows(n_ctx, 512)
    ts_lat = _tile_rows(n_lat, 1024)

    crows = jnp.zeros((2 * SUBLANES, d), F32).at[:bsz].set(c).at[bsz].set(c_ctx)
    mod_all = _ada_call(crows, w_ada, b_ada)

    zpad = lambda n: jnp.zeros((depth, d, n), F32)
    o = np.cumsum([0, GLA_QK, GLA_QK, GLA_V, GLA_LOWRANK, GLA_LOWRANK, GLA_V,
                   SC_WIDTH, SC_WIDTH, SC_WIDTH, NA_W, NA_W, NA_W])
    col = lambda n: w_in[:, :, o[n]:o[n + 1]]
    qpad = zpad(QK_PAD - GLA_QK)
    w_all = jnp.concatenate([col(0), qpad, col(1), qpad, col(2), col(5), col(3), col(4),
                             zpad(GLR_PAD - 2 * GLA_LOWRANK), w_in[:, :, o[6]:o[9]],
                             col(9) * (NA_DH ** -0.5), col(10), col(11)], axis=-1).astype(BF16)
    assert w_all.shape[-1] == INPROJ_WIDTH
    w_out_b = w_out.astype(BF16)
    wo_gla, wo_sc, wo_na = w_out_b[:, :GLA_V], w_out_b[:, GLA_V:GLA_V + SC_WIDTH], w_out_b[:, GLA_V + SC_WIDTH:]
    wup = ffn_w_up.astype(BF16)
    wdn = ffn_w_down.astype(BF16)

    def gate_w(w2, first_row):
        full = jnp.zeros((depth, GLR_PAD, QK_PAD), F32)
        return full.at[:, first_row:first_row + GLA_LOWRANK, :GLA_QK].set(w2).astype(BF16)

    def gate_b(bias):
        return jnp.zeros((depth, 1, QK_PAD), F32).at[:, 0, :GLA_QK].set(bias)

    wg_f, wg_b = gate_w(gla_wg2_fw, 0), gate_w(gla_wg2_bw, GLA_LOWRANK)
    bg_f, bg_b = gate_b(gla_bg_fw), gate_b(gla_bg_bw)
    gla_gain = jnp.tile(gla_norm_g, (1, GLA_HEADS))[:, None, :]
    hv = np.arange(GLA_V) // GLA_DV
    head_mean = jnp.asarray((hv[:, None] == hv[None, :]) / GLA_DV, BF16)
    tabs_f, tabs_b = _gla_tables(False), _gla_tables(True)
    rope_tabs = _rope_tables(n_lat)
    zero_state = jnp.zeros((bsz, GLA_V, QK_PAD // 2), F32)
    na_bias = _na_bias_tables(na_rpb, rows)

    xc = ctx
    for layer in range(depth):
        update_ctx = layer < depth - 1
        mod = mod_all[layer]
        sh1, sc1, g1, sh2, sc2, g2 = [mod[:bsz, n * d:(n + 1) * d][:, None, :] for n in range(6)]
        sh1c, sc1c, g1c, sh2c, sc2c, g2c = [mod[bsz:bsz + 1, n * d:(n + 1) * d][:, None, :] for n in range(6)]
        gain_mix = norm_mix_g[layer][None, :]
        gain_ffn = norm_ffn_g[layer][None, :]

        cq, ck, cv, cr, cglr, cscu, cnaq, cnak, cnav = _inproj_call(
            xc, sh1c, sc1c, gain_mix, w_all[layer], None, tm_ctx)
        lq, lk, lv, lr, lglr, lscu, lnaq, lnak, lnav = _inproj_call(
            x, sh1, sc1, gain_mix, w_all[layer], rope_tabs, tm_lat)

        fin = lambda r, o_f: (r, o_f, gla_gain[layer], head_mean)
        oc_f, state_f = _gla_call(cq, ck, cv, cglr, wg_f[layer], bg_f[layer], tabs_f, zero_state, tm_ctx, False)
        gla_ctx, state_b = _gla_call(cq, ck, cv, cglr, wg_b[layer], bg_b[layer], tabs_b, zero_state, tm_ctx, True,
                                     fin(cr, oc_f))
        ol_f, _ = _gla_call(lq, lk, lv, lglr, wg_f[layer], bg_f[layer], tabs_f, state_f, ts_lat, False)
        gla_lat, _ = _gla_call(lq, lk, lv, lglr, wg_b[layer], bg_b[layer], tabs_b, state_b, ts_lat, True,
                               fin(lr, ol_f))

        cw_sc, cb_sc = sc_conv_w[layer], sc_conv_b[layer][None, :]
        wo = (wo_gla[layer], wo_sc[layer], wo_na[layer])
        x = _na_call(lnaq, lnak, lnav, cnak, cnav, na_bias[layer], gla_lat, lscu, x, g1, *wo, cw_sc, cb_sc)
        ffn = (wup[layer], ffn_conv_w[layer], ffn_conv_b[layer][None, :], wdn[layer])
        x = _ffn_call(x, sh2, sc2, g2, gain_ffn, *ffn,
                      final_norm_g[None, :] if layer == depth - 1 else None, tm_lat)

        if update_ctx:
            na_ctx = _ctx_attn_call(cnaq, cnak, cnav)
            xc = _outproj_call(gla_ctx, na_ctx, cscu, xc, g1c, *wo, cw_sc, cb_sc, tm_ctx)
            xc = _ffn_call(xc, sh2c, sc2c, g2c, gain_ffn, *ffn, None, tm_ctx)
    return x
```

```python
import functools

import numpy as np
import jax
import jax.numpy as jnp
from jax import lax
from jax.experimental import pallas as pl
from jax.experimental.pallas import tpu as pltpu

F32 = jnp.float32
BF16 = jnp.bfloat16

GRID_W = 64
EPS = 1e-6
NEG_INF = -1e30
GLA_HEADS = 6
GLA_DK = 32
GLA_DV = 64
GLA_LOWRANK = 16
GLA_TAU = 16.0
ROPE_BASE = 10000.0
SC_WIDTH = 256
NA_HEADS = 6
NA_DH = 64
NA_WIN_ROWS = 8
NA_WIN_COLS = 16
GLA_QK = GLA_HEADS * GLA_DK
GLA_V = GLA_HEADS * GLA_DV
NA_W = NA_HEADS * NA_DH

LANES = 128
SUBLANES = 8
VMEM_LIMIT_BYTES = 56 * 1024 * 1024

QK_PAD = 2 * LANES
GLR_PAD = LANES
GLA_GROUP = 128
GLA_LEVELS = 4
FFN_CHUNK = 6 * LANES
HEADS_A = LANES // GLA_DK
GLA_SINGLE_LEVEL_DECAY = 64.0
NA_Q_ROWS = 4
NA_K_ROWS = 12

NT_DIMS = (((1,), (1,)), ((), ()))


def _cparams(*sem):
    return pltpu.CompilerParams(dimension_semantics=sem, vmem_limit_bytes=VMEM_LIMIT_BYTES)


def _silu(t):
    return t / (1.0 + jnp.exp(-t))


def _norm_mod(t, gain, shift, scale):
    ms = jnp.mean(t * t, axis=-1, keepdims=True)
    return (t * lax.rsqrt(ms + EPS) * gain) * (1.0 + scale) + shift


def _ada_kernel(c_ref, w_ref, b_ref, o_ref):
    c = c_ref[...]
    o_ref[0] = jnp.dot(_silu(c), w_ref[0], preferred_element_type=F32,
                       precision=lax.Precision.HIGHEST) + b_ref[0]


def _ada_call(crows, w_ada, b_ada):
    depth, d, n = w_ada.shape
    rows = crows.shape[0]
    tn = 1536
    return pl.pallas_call(
        _ada_kernel,
        grid=(depth, n // tn),
        in_specs=[pl.BlockSpec((rows, d), lambda l, j: (0, 0)),
                  pl.BlockSpec((1, d, tn), lambda l, j: (l, 0, j)),
                  pl.BlockSpec((1, 1, tn), lambda l, j: (l, 0, j))],
        out_specs=pl.BlockSpec((1, rows, tn), lambda l, j: (l, 0, j)),
        out_shape=jax.ShapeDtypeStruct((depth, rows, n), F32),
        compiler_params=_cparams("arbitrary", "arbitrary"),
        name="ada_mod",
    )(crows, w_ada, b_ada.reshape(depth, 1, n))


INPROJ_OUTS = ((QK_PAD, F32), (QK_PAD, F32), (GLA_V, F32), (GLA_V, F32), (GLR_PAD, F32),
               (3 * SC_WIDTH, F32), (NA_W, BF16), (NA_W, BF16), (NA_W, BF16))
INPROJ_STARTS = tuple(int(s) for s in np.cumsum([0] + [w for w, _ in INPROJ_OUTS]))
INPROJ_WIDTH = INPROJ_STARTS[-1]
MXU_COLS = 2 * LANES


def _rotary(t, c, s):
    lane = lax.broadcasted_iota(jnp.int32, t.shape, 1)
    swapped = jnp.where(lane % 2 == 0, pltpu.roll(t, LANES - 1, 1), pltpu.roll(t, 1, 1))
    return t * c + swapped * s


def _inproj_kernel(*refs, rope):
    if rope:
        x_ref, sh_ref, sc_ref, g_ref, cos_ref, sin_ref, w_ref = refs[:7]
        outs = refs[7:]
    else:
        x_ref, sh_ref, sc_ref, g_ref, w_ref = refs[:5]
        outs = refs[5:]
    hb = _norm_mod(x_ref[0], g_ref[...], sh_ref[0], sc_ref[0]).astype(BF16)
    for n0 in range(0, INPROJ_WIDTH, MXU_COLS):
        res = jnp.dot(hb, w_ref[:, n0:n0 + MXU_COLS], preferred_element_type=F32)
        for a in range(n0, n0 + MXU_COLS, LANES):
            idx = max(i for i, s0 in enumerate(INPROJ_STARTS[:-1]) if s0 <= a)
            off = a - INPROJ_STARTS[idx]
            piece = res[:, a - n0:a - n0 + LANES]
            if idx < 2:
                if rope:
                    piece = _rotary(piece, cos_ref[:, off:off + LANES], sin_ref[:, off:off + LANES])
                if idx == 0:
                    piece = piece * (GLA_DK ** -0.5)
            outs[idx][0, :, off:off + LANES] = piece.astype(outs[idx].dtype)


def _inproj_call(x, shift, scale, gain, w_all, rope_tabs, tm):
    b, l, d = x.shape
    bm = shift.shape[0]
    rope = rope_tabs is not None
    mod_map = (lambda i, bb: (bb, 0, 0)) if bm > 1 else (lambda i, bb: (0, 0, 0))
    tok = lambda w: pl.BlockSpec((1, tm, w), lambda i, bb: (bb, i, 0))
    full = lambda a: pl.BlockSpec(a.shape, lambda i, bb: (0,) * a.ndim)
    in_specs = [tok(d), pl.BlockSpec((1, 1, d), mod_map), pl.BlockSpec((1, 1, d), mod_map), full(gain)]
    args = [x, shift, scale, gain]
    if rope:
        in_specs += [pl.BlockSpec((tm, QK_PAD), lambda i, bb: (i, 0))] * 2
        args += list(rope_tabs)
    in_specs.append(full(w_all))
    args.append(w_all)
    return pl.pallas_call(
        functools.partial(_inproj_kernel, rope=rope),
        grid=(l // tm, b),
        in_specs=in_specs,
        out_specs=[tok(w) for w, _ in INPROJ_OUTS],
        out_shape=[jax.ShapeDtypeStruct((b, l, w), dt) for w, dt in INPROJ_OUTS],
        compiler_params=_cparams("arbitrary", "arbitrary"),
        name="in_proj_rope" if rope else "in_proj_ctx",
    )(*args)


def _gla_tables(reverse):
    t = np.arange(GLA_GROUP)
    i, j = t[:, None], t[None, :]
    before = (j >= i) if reverse else (j <= i)
    strictly_after = (j < i) if reverse else (j > i)
    maps = []
    for n in (16, 32, 64, 128):
        maps.append((i // n == j // n) & before)
    for n in (16, 32, 64, 128):
        maps.append((i // n == j // n) & strictly_after)
    masks = [(i // 16 == j // 16) & before]
    for n in (32, 64, 128):
        h = n // 2
        q_late, k_early = (i % n) >= h, (j % n) < h
        if reverse:
            q_late, k_early = (i % n) < h, (j % n) >= h
        masks.append((i // n == j // n) & q_late & k_early)
    cm = np.concatenate(maps, axis=0).astype(np.float32)
    cm2 = np.concatenate([maps[3], maps[7]], axis=0).astype(np.float32)
    lm = np.stack(masks, axis=0).astype(np.float32)
    return jnp.asarray(cm, BF16), jnp.asarray(cm2, BF16), jnp.asarray(lm, F32)


def _gla_kernel(*refs, reverse, finalize, ngroups):
    if finalize:
        (q_ref, k_ref, v_ref, glr_ref, wg_ref, bg_ref, cm_ref, cm2_ref, lm_ref, s0_ref,
         r_ref, of_ref, ng_ref, bd_ref, out_ref, sfin_ref, st_ref, gate_ref) = refs
    else:
        (q_ref, k_ref, v_ref, glr_ref, wg_ref, bg_ref, cm_ref, cm2_ref, lm_ref, s0_ref,
         out_ref, sfin_ref, st_ref, gate_ref) = refs
    step = pl.program_id(1)
    g = GLA_GROUP
    kc = QK_PAD // 2
    vc = HEADS_A * GLA_DV

    @pl.when(step == 0)
    def _():
        st_ref[...] = s0_ref[0]

    z = jnp.dot(glr_ref[0].astype(BF16), wg_ref[...], preferred_element_type=F32) + bg_ref[...]
    gate_all = (jnp.minimum(z, 0.0) - jnp.log(1.0 + jnp.exp(-jnp.abs(z)))) * (1.0 / GLA_TAU)
    gate_ref[...] = gate_all
    total = jnp.sum(gate_all.reshape(ngroups, g, QK_PAD), axis=1)
    single_level = jnp.min(total) > -GLA_SINGLE_LEVEL_DECAY

    k_head = lax.broadcasted_iota(jnp.int32, (g, kc), 1) // GLA_DK
    kmask = [k_head == h for h in range(HEADS_A)]
    va_head = lax.broadcasted_iota(jnp.int32, (g, vc), 1) // GLA_DV
    vb_head = lax.broadcasted_iota(jnp.int32, (g, GLA_V - vc), 1) // GLA_DV
    st_row = lax.broadcasted_iota(jnp.int32, (GLA_V, kc), 0)
    st_col = lax.broadcasted_iota(jnp.int32, (GLA_V, kc), 1) // GLA_DK
    state_mask = st_row // GLA_DV == st_col + jnp.where(st_row >= vc, HEADS_A, 0)
    lmask = [lm_ref[l] > 0.5 for l in range(GLA_LEVELS)]
    causal = functools.reduce(jnp.logical_or, lmask)

    def stack_heads(t, masks, n):
        return jnp.concatenate([jnp.where(masks[h], t, 0.0) for h in range(n)], axis=0).astype(BF16)

    def scores(qs, ks):
        sa = lax.dot_general(qs[:, :kc].astype(BF16), stack_heads(ks[:, :kc], kmask, HEADS_A), NT_DIMS,
                             preferred_element_type=F32)
        sb = lax.dot_general(qs[:, kc:].astype(BF16), stack_heads(ks[:, kc:], kmask, GLA_HEADS - HEADS_A),
                             NT_DIMS, preferred_element_type=F32)
        return jnp.concatenate([sa, sb], axis=1)

    def group(gi, robust):
        gidx = (ngroups - 1 - gi) if reverse else gi
        rows = pl.ds(gidx * g, g)
        q = q_ref[0, rows, :]
        k = k_ref[0, rows, :]
        v = v_ref[0, rows, :]
        gate = gate_ref[rows, :]
        ghi = gate.astype(BF16)
        glo = (gate - ghi.astype(F32)).astype(BF16)
        cmat = cm_ref[...] if robust else cm2_ref[...]
        cs = jnp.dot(cmat, ghi, preferred_element_type=F32) + jnp.dot(cmat, glo, preferred_element_type=F32)
        if robust:
            p16, p32, p64, p128, s16, s32, s64, s128 = [cs[n * g:(n + 1) * g] for n in range(8)]
            q0 = q * jnp.exp(p16)
            a = [scores(q0, k * jnp.exp(-p16)),
                 scores(q0, k * jnp.exp(s16)),
                 scores(q * jnp.exp(p32), k * jnp.exp(s32)),
                 scores(q * jnp.exp(p64), k * jnp.exp(s64))]
            blocks = []
            for h in range(GLA_HEADS):
                sl = slice(h * g, (h + 1) * g)
                blk = jnp.where(lmask[3], a[3][:, sl], 0.0)
                for l in (2, 1, 0):
                    blk = jnp.where(lmask[l], a[l][:, sl], blk)
                blocks.append(blk)
            q4 = q * jnp.exp(p128)
        else:
            p128, s128 = cs[0:g], cs[g:2 * g]
            q4 = q * jnp.exp(p128)
            a = scores(q4, k * jnp.exp(-p128))
            blocks = [jnp.where(causal, a[:, h * g:(h + 1) * g], 0.0) for h in range(GLA_HEADS)]
        amat_a = jnp.concatenate(blocks[:HEADS_A], axis=1).astype(BF16)
        amat_b = jnp.concatenate(blocks[HEADS_A:], axis=1).astype(BF16)
        v_a = stack_heads(v[:, :vc], [va_head == h for h in range(HEADS_A)], HEADS_A)
        v_b = stack_heads(v[:, vc:], [vb_head == h for h in range(GLA_HEADS - HEADS_A)], GLA_HEADS - HEADS_A)
        st = st_ref[...]
        stb = st.astype(BF16)
        q4b = q4.astype(BF16)
        o_a = (jnp.dot(amat_a, v_a, preferred_element_type=F32)
               + lax.dot_general(q4b[:, :kc], stb[:vc], NT_DIMS, preferred_element_type=F32))
        o_b = (jnp.dot(amat_b, v_b, preferred_element_type=F32)
               + lax.dot_general(q4b[:, kc:], stb[vc:], NT_DIMS, preferred_element_type=F32))
        o = jnp.concatenate([o_a, o_b], axis=1)
        vt = v.T.astype(BF16)
        k4 = (k * jnp.exp(s128)).astype(BF16)
        upd = jnp.concatenate([jnp.dot(vt[:vc], k4[:, :kc], preferred_element_type=F32),
                               jnp.dot(vt[vc:], k4[:, kc:], preferred_element_type=F32)], axis=0)
        decay = jnp.exp(p128[0:1] + s128[0:1])
        dec = jnp.concatenate([jnp.broadcast_to(decay[:, :kc], (vc, kc)),
                               jnp.broadcast_to(decay[:, kc:], (GLA_V - vc, kc))], axis=0)
        st_ref[...] = st * dec + jnp.where(state_mask, upd, 0.0)
        if finalize:
            ot = o + of_ref[0, rows, :]
            o2 = ot * ot
            hi = o2.astype(BF16)
            lo = (o2 - hi.astype(F32)).astype(BF16)
            ms = (jnp.dot(hi, bd_ref[...], preferred_element_type=F32)
                  + jnp.dot(lo, bd_ref[...], preferred_element_type=F32))
            y = ot * lax.rsqrt(ms + EPS) * ng_ref[...] * _silu(r_ref[0, rows, :])
            out_ref[0, rows, :] = y.astype(out_ref.dtype)
        else:
            out_ref[0, rows, :] = o

    @pl.when(single_level)
    def _():
        for gi in range(ngroups):
            group(gi, False)

    @pl.when(jnp.logical_not(single_level))
    def _():
        for gi in range(ngroups):
            group(gi, True)

    @pl.when(step == pl.num_programs(1) - 1)
    def _():
        sfin_ref[0] = st_ref[...]


def _gla_call(q, k, v, glr, wg, bg, tabs, s0, ts, reverse, fin=None):
    b, l, _ = q.shape
    nb = l // ts
    finalize = fin is not None
    cm, cm2, lm = tabs
    blk = (lambda bb, i: (bb, nb - 1 - i, 0)) if reverse else (lambda bb, i: (bb, i, 0))
    tok = lambda w: pl.BlockSpec((1, ts, w), blk)
    full = lambda a: pl.BlockSpec(a.shape, lambda bb, i: (0,) * a.ndim)
    state = pl.BlockSpec((1, GLA_V, QK_PAD // 2), lambda bb, i: (bb, 0, 0))
    in_specs = [tok(QK_PAD), tok(QK_PAD), tok(GLA_V), tok(GLR_PAD), full(wg), full(bg), full(cm), full(cm2),
                full(lm), state]
    args = [q, k, v, glr, wg, bg, cm, cm2, lm, s0]
    if finalize:
        r, o_other, ng, bd = fin
        in_specs += [tok(GLA_V), tok(GLA_V), full(ng), full(bd)]
        args += [r, o_other, ng, bd]
    out_dt = BF16 if finalize else F32
    return pl.pallas_call(
        functools.partial(_gla_kernel, reverse=reverse, finalize=finalize, ngroups=ts // GLA_GROUP),
        grid=(b, nb),
        in_specs=in_specs,
        out_specs=[tok(GLA_V), state],
        out_shape=[jax.ShapeDtypeStruct((b, l, GLA_V), out_dt),
                   jax.ShapeDtypeStruct((b, GLA_V, QK_PAD // 2), F32)],
        scratch_shapes=[pltpu.VMEM((GLA_V, QK_PAD // 2), F32), pltpu.VMEM((ts, QK_PAD), F32)],
        compiler_params=_cparams("arbitrary", "arbitrary"),
        name="gla_bwd" if reverse else "gla_fwd",
    )(*args)


def _na_bias_tables(rpb, rows):
    kh = min(NA_WIN_ROWS, rows)
    n_dr, n_dc = 2 * NA_WIN_ROWS - 1, 2 * NA_WIN_COLS - 1
    qr, kr = np.arange(NA_Q_ROWS)[:, None], np.arange(NA_K_ROWS)[None, :]
    row_pick, row_ok = [], []
    for r0, ks in ((0, 0), (NA_WIN_ROWS // 2, 0), (rows - NA_Q_ROWS, rows - NA_K_ROWS)):
        r, r2 = r0 + qr, ks + kr
        rs = np.clip(r - kh // 2, 0, rows - kh)
        row_ok.append((r2 >= rs) & (r2 < rs + kh))
        row_pick.append(np.eye(n_dr, dtype=np.float32)[np.clip(r2 - r + NA_WIN_ROWS - 1, 0, n_dr - 1)])
    row_pick, row_ok = np.stack(row_pick), np.stack(row_ok)
    qc, kc = np.arange(GRID_W)[:, None], np.arange(GRID_W)[None, :]
    cs = np.clip(qc - NA_WIN_COLS // 2, 0, GRID_W - NA_WIN_COLS)
    col_ok = (kc >= cs) & (kc < cs + NA_WIN_COLS)
    col_pick = np.eye(n_dc, dtype=np.float32)[np.clip(kc - qc + NA_WIN_COLS - 1, 0, n_dc - 1)]
    hi = lax.Precision.HIGHEST
    by_row = jnp.einsum('lhde,cqkd->lchqke', rpb.astype(F32), row_pick, precision=hi)
    bias = jnp.einsum('lchqke,xye->lchqxky', by_row, col_pick, precision=hi)
    valid = row_ok[:, None, :, None, :, None] & col_ok[None, None, None, :, None, :]
    bias = jnp.where(valid[None], bias, NEG_INF)
    return bias.reshape(rpb.shape[0], 3, NA_HEADS, NA_Q_ROWS * GRID_W, NA_K_ROWS * GRID_W)


def _scores(qh, parts):
    out = []
    for kh, _, bias in parts:
        sc = lax.dot_general(qh, kh, NT_DIMS, preferred_element_type=F32)
        out.append(sc if bias is None else sc + bias)
    return out


def _softmax_parts(s):
    m = functools.reduce(jnp.maximum, [jnp.max(t, axis=-1, keepdims=True) for t in s])
    p = [jnp.exp(t - m) for t in s]
    denom = functools.reduce(jnp.add, [jnp.sum(t, axis=-1, keepdims=True) for t in p])
    return [t.astype(BF16) for t in p], denom


def _weighted_values(p, denom, parts):
    o = functools.reduce(jnp.add, [jnp.dot(t, vh, preferred_element_type=F32) for t, (_, vh, _) in zip(p, parts)])
    return o / denom


def _attend_heads(q, parts_of_head, n_heads):
    s, p, outs = {}, {}, {}
    for t in range(n_heads + 2):
        if t < n_heads:
            s[t] = _scores(q[:, t * NA_DH:(t + 1) * NA_DH], parts_of_head(t))
        if 0 <= t - 1 < n_heads:
            p[t - 1] = _softmax_parts(s.pop(t - 1))
        if 0 <= t - 2 < n_heads:
            pp, denom = p.pop(t - 2)
            outs[t - 2] = _weighted_values(pp, denom, parts_of_head(t - 2))
    return jnp.concatenate([outs[h] for h in range(n_heads)], axis=1)


def _mix_residual(gla, na, u, u_prev, u_next, x, g1, wg, ws, wn, cw, cb, first, last):
    w = SC_WIDTH
    tm = u.shape[0]
    cx = u[:, w:2 * w] * u[:, 2 * w:3 * w]
    cx_prev = (u_prev[SUBLANES - 1:SUBLANES, w:2 * w] * u_prev[SUBLANES - 1:SUBLANES, 2 * w:3 * w]
               * jnp.where(first, 0.0, 1.0))
    cx_next = u_next[0:1, w:2 * w] * u_next[0:1, 2 * w:3 * w] * jnp.where(last, 0.0, 1.0)
    row = lax.broadcasted_iota(jnp.int32, cx.shape, 0)
    prev = jnp.where(row == 0, cx_prev, pltpu.roll(cx, 1, 0))
    nxt = jnp.where(row == tm - 1, cx_next, pltpu.roll(cx, tm - 1, 0))
    sc = u[:, 0:w] * (prev * cw[0:1] + cx * cw[1:2] + nxt * cw[2:3] + cb)
    mix = (jnp.dot(gla, wg, preferred_element_type=F32)
           + jnp.dot(sc.astype(BF16), ws, preferred_element_type=F32)
           + jnp.dot(na, wn, preferred_element_type=F32))
    return x + g1 * mix


def _na_kernel(q_ref, k_ref, v_ref, kc_ref, vc_ref, bias_ref, gla_ref, scu_ref, scp_ref, scn_ref, x_ref, g1_ref,
               wg_ref, ws_ref, wn_ref, cw_ref, cb_ref, o_ref, *, rows):
    i = pl.program_id(1)
    ks = jnp.clip(i * NA_Q_ROWS - NA_WIN_ROWS // 2, 0, rows - NA_K_ROWS)
    win = pl.ds(pl.multiple_of(ks * GRID_W, GRID_W), NA_K_ROWS * GRID_W)
    q = q_ref[0]
    kw = k_ref[0, win, :]
    vw = v_ref[0, win, :]
    kc = kc_ref[0]
    vc = vc_ref[0]
    def parts(h):
        sl = slice(h * NA_DH, (h + 1) * NA_DH)
        return [(kw[:, sl], vw[:, sl], bias_ref[0, h]), (kc[:, sl], vc[:, sl], None)]

    na = _attend_heads(q, parts, NA_HEADS).astype(BF16)
    o_ref[0] = _mix_residual(gla_ref[0], na, scu_ref[0], scp_ref[0], scn_ref[0], x_ref[0], g1_ref[0],
                             wg_ref[...], ws_ref[...], wn_ref[...], cw_ref[...], cb_ref[...],
                             i == 0, i == pl.num_programs(1) - 1)


def _na_call(q, k, v, kc, vc, bias, gla, scu, x, g1, wg, ws, wn, cw, cb):
    b, l, w = q.shape
    d = x.shape[-1]
    c = kc.shape[1]
    rows = l // GRID_W
    nrb = rows // NA_Q_ROWS
    nq, nk = NA_Q_ROWS * GRID_W, NA_K_ROWS * GRID_W
    nblk8, t8 = l // SUBLANES, nq // SUBLANES
    seq = lambda n: pl.BlockSpec((1, n, w), lambda bb, i: (bb, 0, 0))
    tok = lambda width: pl.BlockSpec((1, nq, width), lambda bb, i: (bb, i, 0))
    full = lambda a: pl.BlockSpec(a.shape, lambda bb, i: (0,) * a.ndim)
    case = lambda bb, i: (jnp.where(i == 0, 0, jnp.where(i == nrb - 1, 2, 1)), 0, 0, 0)
    halo_prev = pl.BlockSpec((1, SUBLANES, 3 * SC_WIDTH), lambda bb, i: (bb, jnp.maximum(i * t8 - 1, 0), 0))
    halo_next = pl.BlockSpec((1, SUBLANES, 3 * SC_WIDTH), lambda bb, i: (bb, jnp.minimum((i + 1) * t8, nblk8 - 1), 0))
    return pl.pallas_call(
        functools.partial(_na_kernel, rows=rows),
        grid=(b, nrb),
        in_specs=[tok(w), seq(l), seq(l), seq(c), seq(c), pl.BlockSpec((1, NA_HEADS, nq, nk), case),
                  tok(GLA_V), tok(3 * SC_WIDTH), halo_prev, halo_next, tok(d),
                  pl.BlockSpec((1, 1, d), lambda bb, i: (bb, 0, 0)),
                  full(wg), full(ws), full(wn), full(cw), full(cb)],
        out_specs=tok(d),
        out_shape=jax.ShapeDtypeStruct((b, l, d), F32),
        compiler_params=_cparams("arbitrary", "arbitrary"),
        name="na_out_proj",
    )(q, k, v, kc, vc, bias, gla, scu, scu, scu, x, g1, wg, ws, wn, cw, cb)


def _ctx_attn_kernel(q_ref, k_ref, v_ref, o_ref):
    q, k, v = q_ref[0], k_ref[0], v_ref[0]
    def parts(h):
        sl = slice(h * NA_DH, (h + 1) * NA_DH)
        return [(k[:, sl], v[:, sl], None)]

    o_ref[0] = _attend_heads(q, parts, NA_HEADS).astype(o_ref.dtype)


def _ctx_attn_call(q, k, v):
    b, c, w = q.shape
    spec = pl.BlockSpec((1, c, w), lambda bb: (bb, 0, 0))
    return pl.pallas_call(
        _ctx_attn_kernel, grid=(b,), in_specs=[spec, spec, spec], out_specs=spec,
        out_shape=jax.ShapeDtypeStruct((b, c, w), BF16),
        compiler_params=_cparams("arbitrary"), name="ctx_attn",
    )(q, k, v)


def _outproj_kernel(gla_ref, na_ref, scu_ref, scp_ref, scn_ref, x_ref, g1_ref,
                    wg_ref, ws_ref, wn_ref, cw_ref, cb_ref, o_ref):
    i = pl.program_id(1)
    o_ref[0] = _mix_residual(gla_ref[0], na_ref[0], scu_ref[0], scp_ref[0], scn_ref[0], x_ref[0], g1_ref[0],
                             wg_ref[...], ws_ref[...], wn_ref[...], cw_ref[...], cb_ref[...],
                             i == 0, i == pl.num_programs(1) - 1)


def _outproj_call(gla, na, scu, x, g1, wg, ws, wn, cw, cb, tm):
    b, l, d = x.shape
    bm = g1.shape[0]
    nblk8 = l // SUBLANES
    t8 = tm // SUBLANES
    mod_map = (lambda bb, i: (bb, 0, 0)) if bm > 1 else (lambda bb, i: (0, 0, 0))
    tok = lambda w: pl.BlockSpec((1, tm, w), lambda bb, i: (bb, i, 0))
    full = lambda a: pl.BlockSpec(a.shape, lambda bb, i: (0,) * a.ndim)
    halo_prev = pl.BlockSpec((1, SUBLANES, 3 * SC_WIDTH), lambda bb, i: (bb, jnp.maximum(i * t8 - 1, 0), 0))
    halo_next = pl.BlockSpec((1, SUBLANES, 3 * SC_WIDTH), lambda bb, i: (bb, jnp.minimum((i + 1) * t8, nblk8 - 1), 0))
    return pl.pallas_call(
        _outproj_kernel,
        grid=(b, l // tm),
        in_specs=[tok(GLA_V), tok(NA_W), tok(3 * SC_WIDTH), halo_prev, halo_next, tok(d),
                  pl.BlockSpec((1, 1, d), mod_map), full(wg), full(ws), full(wn), full(cw), full(cb)],
        out_specs=tok(d),
        out_shape=jax.ShapeDtypeStruct((b, l, d), F32),
        compiler_params=_cparams("arbitrary", "arbitrary"),
        name="out_proj",
    )(gla, na, scu, scu, scu, x, g1, wg, ws, wn, cw, cb)


def _ffn_kernel(*refs, final, chunk):
    if final:
        (x_ref, xp_ref, xn_ref, sh_ref, sc_ref, g2_ref, ng_ref, wup_ref, cw_ref, cb_ref, wdn_ref, fg_ref,
         o_ref, act_ref) = refs
    else:
        (x_ref, xp_ref, xn_ref, sh_ref, sc_ref, g2_ref, ng_ref, wup_ref, cw_ref, cb_ref, wdn_ref,
         o_ref, act_ref) = refs
    i = pl.program_id(1)
    last = pl.num_programs(1) - 1
    x = x_ref[0]
    tm = x.shape[0]
    hid = wdn_ref.shape[0]
    nm = lambda t: _norm_mod(t, ng_ref[...], sh_ref[0], sc_ref[0])
    hp = nm(xp_ref[0]) * jnp.where(i > 0, 1.0, 0.0)
    hn = nm(xn_ref[0]) * jnp.where(i < last, 1.0, 0.0)
    hb = jnp.concatenate([hp, nm(x), hn], axis=0).astype(BF16)
    ext = tm + 2 * SUBLANES
    inner = slice(SUBLANES, SUBLANES + tm)

    def conv(u, c0, w):
        cw = cw_ref[:, c0:c0 + w]
        return (pltpu.roll(u, 1, 0)[inner] * cw[0:1] + u[inner] * cw[1:2]
                + pltpu.roll(u, ext - 1, 0)[inner] * cw[2:3] + cb_ref[:, c0:c0 + w])

    for c0 in range(0, hid, chunk):
        w = min(chunk, hid - c0)
        ua = jnp.dot(hb, wup_ref[:, c0:c0 + w], preferred_element_type=F32)
        ub = jnp.dot(hb, wup_ref[:, hid + c0:hid + c0 + w], preferred_element_type=F32)
        act_ref[:, c0:c0 + w] = (_silu(conv(ua, c0, w)) * conv(ub, hid + c0, w)).astype(BF16)
    y = x + g2_ref[0] * jnp.dot(act_ref[...], wdn_ref[...], preferred_element_type=F32)
    if final:
        ms = jnp.mean(y * y, axis=-1, keepdims=True)
        y = y * lax.rsqrt(ms + EPS) * fg_ref[...]
    o_ref[0] = y


def _ffn_call(x, shift, scale, g2, gain, wup, cw, cb, wdn, final_gain, tm):
    b, l, d = x.shape
    bm = shift.shape[0]
    nblk8 = l // SUBLANES
    t8 = tm // SUBLANES
    final = final_gain is not None
    mod_map = (lambda bb, i: (bb, 0, 0)) if bm > 1 else (lambda bb, i: (0, 0, 0))
    tok = pl.BlockSpec((1, tm, d), lambda bb, i: (bb, i, 0))
    full = lambda a: pl.BlockSpec(a.shape, lambda bb, i: (0,) * a.ndim)
    resident = lambda a: pl.BlockSpec(a.shape, lambda bb, i: (0,) * a.ndim, pipeline_mode=pl.Buffered(1))
    halo_prev = pl.BlockSpec((1, SUBLANES, d), lambda bb, i: (bb, jnp.maximum(i * t8 - 1, 0), 0))
    halo_next = pl.BlockSpec((1, SUBLANES, d), lambda bb, i: (bb, jnp.minimum((i + 1) * t8, nblk8 - 1), 0))
    mod = pl.BlockSpec((1, 1, d), mod_map)
    in_specs = [tok, halo_prev, halo_next, mod, mod, mod, full(gain), resident(wup), full(cw), full(cb), resident(wdn)]
    args = [x, x, x, shift, scale, g2, gain, wup, cw, cb, wdn]
    if final:
        in_specs.append(full(final_gain))
        args.append(final_gain)
    return pl.pallas_call(
        functools.partial(_ffn_kernel, final=final, chunk=FFN_CHUNK),
        grid=(b, l // tm),
        in_specs=in_specs,
        out_specs=tok,
        out_shape=jax.ShapeDtypeStruct((b, l, d), F32),
        scratch_shapes=[pltpu.VMEM((tm, wdn.shape[0]), BF16)],
        compiler_params=_cparams("arbitrary", "arbitrary"),
        name="conv_ffn_final" if final else "conv_ffn",
    )(*args)


def _rope_tables(n_tokens):
    t = jnp.arange(n_tokens)
    n_freq = GLA_DK // 4
    inv_freq = ROPE_BASE ** (-jnp.arange(n_freq, dtype=F32) / n_freq)
    row = (t // GRID_W).astype(F32)[:, None] * inv_freq
    col = (t % GRID_W).astype(F32)[:, None] * inv_freq
    ang = jnp.concatenate([row, col], axis=-1)
    cos = jnp.repeat(jnp.cos(ang), 2, axis=-1)
    sin = jnp.repeat(jnp.sin(ang), 2, axis=-1) * jnp.tile(jnp.asarray([-1.0, 1.0], F32), GLA_DK // 2)
    pad = QK_PAD - GLA_QK
    cos = jnp.concatenate([jnp.tile(cos, (1, GLA_HEADS)), jnp.ones((n_tokens, pad), F32)], axis=-1)
    sin = jnp.concatenate([jnp.tile(sin, (1, GLA_HEADS)), jnp.zeros((n_tokens, pad), F32)], axis=-1)
    return cos, sin


def _tile_rows(n, target):
    t = min(n, target)
    assert n % t == 0 and t % GLA_GROUP == 0, (n, t)
    return t


def kernel(x, c, ctx, c_ctx, w_ada, b_ada, norm_mix_g, norm_ffn_g, w_in, gla_wg2_fw, gla_bg_fw, gla_wg2_bw, gla_bg_bw, gla_norm_g, sc_conv_w, sc_conv_b, na_rpb, w_out, ffn_w_up, ffn_conv_w, ffn_conv_b, ffn_w_down, final_norm_g):
    bsz, n_lat, d = x.shape
    n_ctx = ctx.shape[1]
    depth = w_in.shape[0]
    rows = n_lat // GRID_W
    assert n_lat % (NA_Q_ROWS * GRID_W) == 0 and rows >= NA_K_ROWS and bsz + 1 <= 2 * SUBLANES
    tm_lat, tm_ctx = _tile_rows(n_lat, 512), _tile_rows(n_ctx, 512)
    ts_lat = _tile_rows(n_lat, 1024)

    crows = jnp.zeros((2 * SUBLANES, d), F32).at[:bsz].set(c).at[bsz].set(c_ctx)
    mod_all = _ada_call(crows, w_ada, b_ada)

    zpad = lambda n: jnp.zeros((depth, d, n), F32)
    o = np.cumsum([0, GLA_QK, GLA_QK, GLA_V, GLA_LOWRANK, GLA_LOWRANK, GLA_V,
                   SC_WIDTH, SC_WIDTH, SC_WIDTH, NA_W, NA_W, NA_W])
    col = lambda n: w_in[:, :, o[n]:o[n + 1]]
    qpad = zpad(QK_PAD - GLA_QK)
    w_all = jnp.concatenate([col(0), qpad, col(1), qpad, col(2), col(5), col(3), col(4),
                             zpad(GLR_PAD - 2 * GLA_LOWRANK), w_in[:, :, o[6]:o[9]],
                             col(9) * (NA_DH ** -0.5), col(10), col(11)], axis=-1).astype(BF16)
    assert w_all.shape[-1] == INPROJ_WIDTH
    w_out_b = w_out.astype(BF16)
    wo_gla, wo_sc, wo_na = w_out_b[:, :GLA_V], w_out_b[:, GLA_V:GLA_V + SC_WIDTH], w_out_b[:, GLA_V + SC_WIDTH:]
    wup = ffn_w_up.astype(BF16)
    wdn = ffn_w_down.astype(BF16)

    def gate_w(w2, first_row):
        full = jnp.zeros((depth, GLR_PAD, QK_PAD), F32)
        return full.at[:, first_row:first_row + GLA_LOWRANK, :GLA_QK].set(w2).astype(BF16)

    def gate_b(bias):
        return jnp.zeros((depth, 1, QK_PAD), F32).at[:, 0, :GLA_QK].set(bias)

    wg_f, wg_b = gate_w(gla_wg2_fw, 0), gate_w(gla_wg2_bw, GLA_LOWRANK)
    bg_f, bg_b = gate_b(gla_bg_fw), gate_b(gla_bg_bw)
    gla_gain = jnp.tile(gla_norm_g, (1, GLA_HEADS))[:, None, :]
    hv = np.arange(GLA_V) // GLA_DV
    head_mean = jnp.asarray((hv[:, None] == hv[None, :]) / GLA_DV, BF16)
    tabs_f, tabs_b = _gla_tables(False), _gla_tables(True)
    rope_tabs = _rope_tables(n_lat)
    zero_state = jnp.zeros((bsz, GLA_V, QK_PAD // 2), F32)
    na_bias = _na_bias_tables(na_rpb, rows)

    xc = ctx
    for layer in range(depth):
        update_ctx = layer < depth - 1
        mod = mod_all[layer]
        sh1, sc1, g1, sh2, sc2, g2 = [mod[:bsz, n * d:(n + 1) * d][:, None, :] for n in range(6)]
        sh1c, sc1c, g1c, sh2c, sc2c, g2c = [mod[bsz:bsz + 1, n * d:(n + 1) * d][:, None, :] for n in range(6)]
        gain_mix = norm_mix_g[layer][None, :]
        gain_ffn = norm_ffn_g[layer][None, :]

        cq, ck, cv, cr, cglr, cscu, cnaq, cnak, cnav = _inproj_call(
            xc, sh1c, sc1c, gain_mix, w_all[layer], None, tm_ctx)
        lq, lk, lv, lr, lglr, lscu, lnaq, lnak, lnav = _inproj_call(
            x, sh1, sc1, gain_mix, w_all[layer], rope_tabs, tm_lat)

        fin = lambda r, o_f: (r, o_f, gla_gain[layer], head_mean)
        oc_f, state_f = _gla_call(cq, ck, cv, cglr, wg_f[layer], bg_f[layer], tabs_f, zero_state, tm_ctx, False)
        gla_ctx, state_b = _gla_call(cq, ck, cv, cglr, wg_b[layer], bg_b[layer], tabs_b, zero_state, tm_ctx, True,
                                     fin(cr, oc_f))
        ol_f, _ = _gla_call(lq, lk, lv, lglr, wg_f[layer], bg_f[layer], tabs_f, state_f, ts_lat, False)
        gla_lat, _ = _gla_call(lq, lk, lv, lglr, wg_b[layer], bg_b[layer], tabs_b, state_b, ts_lat, True,
                               fin(lr, ol_f))

        cw_sc, cb_sc = sc_conv_w[layer], sc_conv_b[layer][None, :]
        wo = (wo_gla[layer], wo_sc[layer], wo_na[layer])
        x = _na_call(lnaq, lnak, lnav, cnak, cnav, na_bias[layer], gla_lat, lscu, x, g1, *wo, cw_sc, cb_sc)
        ffn = (wup[layer], ffn_conv_w[layer], ffn_conv_b[layer][None, :], wdn[layer])
        x = _ffn_call(x, sh2, sc2, g2, gain_ffn, *ffn,
                      final_norm_g[None, :] if layer == depth - 1 else None, tm_lat)

        if update_ctx:
            na_ctx = _ctx_attn_call(cnaq, cnak, cnav)
            xc = _outproj_call(gla_ctx, na_ctx, cscu, xc, g1c, *wo, cw_sc, cb_sc, tm_ctx)
            xc = _ffn_call(xc, sh2c, sc2c, g2c, gain_ffn, *ffn, None, tm_ctx)
    return x
```

```python
import functools

import numpy as np
import jax
import jax.numpy as jnp
from jax import lax
from jax.experimental import pallas as pl
from jax.experimental.pallas import tpu as pltpu

F32 = jnp.float32
BF16 = jnp.bfloat16

GRID_W = 64
EPS = 1e-6
NEG_INF = -1e30
GLA_HEADS = 6
GLA_DK = 32
GLA_DV = 64
GLA_LOWRANK = 16
GLA_TAU = 16.0
ROPE_BASE = 10000.0
SC_WIDTH = 256
NA_HEADS = 6
NA_DH = 64
NA_WIN_ROWS = 8
NA_WIN_COLS = 16
GLA_QK = GLA_HEADS * GLA_DK
GLA_V = GLA_HEADS * GLA_DV
NA_W = NA_HEADS * NA_DH

LANES = 128
SUBLANES = 8
VMEM_LIMIT_BYTES = 56 * 1024 * 1024

QK_PAD = 2 * LANES
GLR_PAD = LANES
GLA_GROUP = 128
GLA_LEVELS = 4
FFN_CHUNK = 6 * LANES
LATENT_TILE = 1024
HEADS_A = LANES // GLA_DK
GLA_SINGLE_LEVEL_DECAY = 64.0
NA_Q_ROWS = 4
NA_K_ROWS = 12

NT_DIMS = (((1,), (1,)), ((), ()))


def _cparams(*sem):
    return pltpu.CompilerParams(dimension_semantics=sem, vmem_limit_bytes=VMEM_LIMIT_BYTES)


def _silu(t):
    return t / (1.0 + jnp.exp(-t))


def _norm_mod(t, gain, shift, scale):
    ms = jnp.mean(t * t, axis=-1, keepdims=True)
    return (t * lax.rsqrt(ms + EPS) * gain) * (1.0 + scale) + shift


def _ada_kernel(c_ref, w_ref, b_ref, o_ref):
    c = c_ref[...]
    o_ref[0] = jnp.dot(_silu(c), w_ref[0], preferred_element_type=F32,
                       precision=lax.Precision.HIGHEST) + b_ref[0]


def _ada_call(crows, w_ada, b_ada):
    depth, d, n = w_ada.shape
    rows = crows.shape[0]
    tn = 1536
    return pl.pallas_call(
        _ada_kernel,
        grid=(depth, n // tn),
        in_specs=[pl.BlockSpec((rows, d), lambda l, j: (0, 0)),
                  pl.BlockSpec((1, d, tn), lambda l, j: (l, 0, j)),
                  pl.BlockSpec((1, 1, tn), lambda l, j: (l, 0, j))],
        out_specs=pl.BlockSpec((1, rows, tn), lambda l, j: (l, 0, j)),
        out_shape=jax.ShapeDtypeStruct((depth, rows, n), F32),
        compiler_params=_cparams("arbitrary", "arbitrary"),
        name="ada_mod",
    )(crows, w_ada, b_ada.reshape(depth, 1, n))


INPROJ_OUTS = ((QK_PAD, F32), (QK_PAD, F32), (GLA_V, F32), (GLA_V, F32), (GLR_PAD, F32),
               (3 * SC_WIDTH, F32), (NA_W, BF16), (NA_W, BF16), (NA_W, BF16))
INPROJ_STARTS = tuple(int(s) for s in np.cumsum([0] + [w for w, _ in INPROJ_OUTS]))
INPROJ_WIDTH = INPROJ_STARTS[-1]
MXU_COLS = 2 * LANES


def _rotary(t, c, s):
    lane = lax.broadcasted_iota(jnp.int32, t.shape, 1)
    swapped = jnp.where(lane % 2 == 0, pltpu.roll(t, LANES - 1, 1), pltpu.roll(t, 1, 1))
    return t * c + swapped * s


def _inproj_kernel(*refs, rope):
    if rope:
        x_ref, sh_ref, sc_ref, g_ref, cos_ref, sin_ref, w_ref = refs[:7]
        outs = refs[7:]
    else:
        x_ref, sh_ref, sc_ref, g_ref, w_ref = refs[:5]
        outs = refs[5:]
    hb = _norm_mod(x_ref[0], g_ref[...], sh_ref[0], sc_ref[0]).astype(BF16)
    for n0 in range(0, INPROJ_WIDTH, MXU_COLS):
        res = jnp.dot(hb, w_ref[:, n0:n0 + MXU_COLS], preferred_element_type=F32)
        for a in range(n0, n0 + MXU_COLS, LANES):
            idx = max(i for i, s0 in enumerate(INPROJ_STARTS[:-1]) if s0 <= a)
            off = a - INPROJ_STARTS[idx]
            piece = res[:, a - n0:a - n0 + LANES]
            if idx < 2:
                if rope:
                    piece = _rotary(piece, cos_ref[:, off:off + LANES], sin_ref[:, off:off + LANES])
                if idx == 0:
                    piece = piece * (GLA_DK ** -0.5)
            outs[idx][0, :, off:off + LANES] = piece.astype(outs[idx].dtype)


def _inproj_call(x, shift, scale, gain, w_all, rope_tabs, tm):
    b, l, d = x.shape
    bm = shift.shape[0]
    rope = rope_tabs is not None
    mod_map = (lambda i, bb: (bb, 0, 0)) if bm > 1 else (lambda i, bb: (0, 0, 0))
    tok = lambda w: pl.BlockSpec((1, tm, w), lambda i, bb: (bb, i, 0))
    full = lambda a: pl.BlockSpec(a.shape, lambda i, bb: (0,) * a.ndim)
    in_specs = [tok(d), pl.BlockSpec((1, 1, d), mod_map), pl.BlockSpec((1, 1, d), mod_map), full(gain)]
    args = [x, shift, scale, gain]
    if rope:
        in_specs += [pl.BlockSpec((tm, QK_PAD), lambda i, bb: (i, 0))] * 2
        args += list(rope_tabs)
    in_specs.append(full(w_all))
    args.append(w_all)
    return pl.pallas_call(
        functools.partial(_inproj_kernel, rope=rope),
        grid=(l // tm, b),
        in_specs=in_specs,
        out_specs=[tok(w) for w, _ in INPROJ_OUTS],
        out_shape=[jax.ShapeDtypeStruct((b, l, w), dt) for w, dt in INPROJ_OUTS],
        compiler_params=_cparams("arbitrary", "arbitrary"),
        name="in_proj_rope" if rope else "in_proj_ctx",
    )(*args)


def _gla_tables(reverse):
    t = np.arange(GLA_GROUP)
    i, j = t[:, None], t[None, :]
    before = (j >= i) if reverse else (j <= i)
    strictly_after = (j < i) if reverse else (j > i)
    maps = []
    for n in (16, 32, 64, 128):
        maps.append((i // n == j // n) & before)
    for n in (16, 32, 64, 128):
        maps.append((i // n == j // n) & strictly_after)
    masks = [(i // 16 == j // 16) & before]
    for n in (32, 64, 128):
        h = n // 2
        q_late, k_early = (i % n) >= h, (j % n) < h
        if reverse:
            q_late, k_early = (i % n) < h, (j % n) >= h
        masks.append((i // n == j // n) & q_late & k_early)
    cm = np.concatenate(maps, axis=0).astype(np.float32)
    cm2 = np.concatenate([maps[3], maps[7]], axis=0).astype(np.float32)
    lm = np.stack(masks, axis=0).astype(np.float32)
    return jnp.asarray(cm, BF16), jnp.asarray(cm2, BF16), jnp.asarray(lm, F32)


def _gla_kernel(*refs, reverse, finalize, ngroups):
    if finalize:
        (q_ref, k_ref, v_ref, glr_ref, wg_ref, bg_ref, cm_ref, cm2_ref, lm_ref, s0_ref,
         r_ref, of_ref, ng_ref, bd_ref, out_ref, sfin_ref, st_ref, gate_ref) = refs
    else:
        (q_ref, k_ref, v_ref, glr_ref, wg_ref, bg_ref, cm_ref, cm2_ref, lm_ref, s0_ref,
         out_ref, sfin_ref, st_ref, gate_ref) = refs
    step = pl.program_id(1)
    g = GLA_GROUP
    kc = QK_PAD // 2
    vc = HEADS_A * GLA_DV

    @pl.when(step == 0)
    def _():
        st_ref[...] = s0_ref[0]

    z = jnp.dot(glr_ref[0].astype(BF16), wg_ref[...], preferred_element_type=F32) + bg_ref[...]
    gate_all = (jnp.minimum(z, 0.0) - jnp.log(1.0 + jnp.exp(-jnp.abs(z)))) * (1.0 / GLA_TAU)
    gate_ref[...] = gate_all
    total = jnp.sum(gate_all.reshape(ngroups, g, QK_PAD), axis=1)
    single_level = jnp.min(total) > -GLA_SINGLE_LEVEL_DECAY

    k_head = lax.broadcasted_iota(jnp.int32, (g, kc), 1) // GLA_DK
    kmask = [k_head == h for h in range(HEADS_A)]
    va_head = lax.broadcasted_iota(jnp.int32, (g, vc), 1) // GLA_DV
    vb_head = lax.broadcasted_iota(jnp.int32, (g, GLA_V - vc), 1) // GLA_DV
    st_row = lax.broadcasted_iota(jnp.int32, (GLA_V, kc), 0)
    st_col = lax.broadcasted_iota(jnp.int32, (GLA_V, kc), 1) // GLA_DK
    state_mask = st_row // GLA_DV == st_col + jnp.where(st_row >= vc, HEADS_A, 0)
    lmask = [lm_ref[l] > 0.5 for l in range(GLA_LEVELS)]
    causal = functools.reduce(jnp.logical_or, lmask)

    def stack_heads(t, masks, n):
        return jnp.concatenate([jnp.where(masks[h], t, 0.0) for h in range(n)], axis=0).astype(BF16)

    def scores(qs, ks):
        sa = lax.dot_general(qs[:, :kc].astype(BF16), stack_heads(ks[:, :kc], kmask, HEADS_A), NT_DIMS,
                             preferred_element_type=F32)
        sb = lax.dot_general(qs[:, kc:].astype(BF16), stack_heads(ks[:, kc:], kmask, GLA_HEADS - HEADS_A),
                             NT_DIMS, preferred_element_type=F32)
        return jnp.concatenate([sa, sb], axis=1)

    def group(gi, robust):
        gidx = (ngroups - 1 - gi) if reverse else gi
        rows = pl.ds(gidx * g, g)
        q = q_ref[0, rows, :]
        k = k_ref[0, rows, :]
        v = v_ref[0, rows, :]
        gate = gate_ref[rows, :]
        ghi = gate.astype(BF16)
        glo = (gate - ghi.astype(F32)).astype(BF16)
        cmat = cm_ref[...] if robust else cm2_ref[...]
        cs = jnp.dot(cmat, ghi, preferred_element_type=F32) + jnp.dot(cmat, glo, preferred_element_type=F32)
        if robust:
            p16, p32, p64, p128, s16, s32, s64, s128 = [cs[n * g:(n + 1) * g] for n in range(8)]
            q0 = q * jnp.exp(p16)
            a = [scores(q0, k * jnp.exp(-p16)),
                 scores(q0, k * jnp.exp(s16)),
                 scores(q * jnp.exp(p32), k * jnp.exp(s32)),
                 scores(q * jnp.exp(p64), k * jnp.exp(s64))]
            blocks = []
            for h in range(GLA_HEADS):
                sl = slice(h * g, (h + 1) * g)
                blk = jnp.where(lmask[3], a[3][:, sl], 0.0)
                for l in (2, 1, 0):
                    blk = jnp.where(lmask[l], a[l][:, sl], blk)
                blocks.append(blk)
            q4 = q * jnp.exp(p128)
        else:
            p128, s128 = cs[0:g], cs[g:2 * g]
            q4 = q * jnp.exp(p128)
            a = scores(q4, k * jnp.exp(-p128))
            blocks = [jnp.where(causal, a[:, h * g:(h + 1) * g], 0.0) for h in range(GLA_HEADS)]
        amat_a = jnp.concatenate(blocks[:HEADS_A], axis=1).astype(BF16)
        amat_b = jnp.concatenate(blocks[HEADS_A:], axis=1).astype(BF16)
        v_a = stack_heads(v[:, :vc], [va_head == h for h in range(HEADS_A)], HEADS_A)
        v_b = stack_heads(v[:, vc:], [vb_head == h for h in range(GLA_HEADS - HEADS_A)], GLA_HEADS - HEADS_A)
        st = st_ref[...]
        stb = st.astype(BF16)
        q4b = q4.astype(BF16)
        o_a = (jnp.dot(amat_a, v_a, preferred_element_type=F32)
               + lax.dot_general(q4b[:, :kc], stb[:vc], NT_DIMS, preferred_element_type=F32))
        o_b = (jnp.dot(amat_b, v_b, preferred_element_type=F32)
               + lax.dot_general(q4b[:, kc:], stb[vc:], NT_DIMS, preferred_element_type=F32))
        o = jnp.concatenate([o_a, o_b], axis=1)
        vt = v.T.astype(BF16)
        k4 = (k * jnp.exp(s128)).astype(BF16)
        upd = jnp.concatenate([jnp.dot(vt[:vc], k4[:, :kc], preferred_element_type=F32),
                               jnp.dot(vt[vc:], k4[:, kc:], preferred_element_type=F32)], axis=0)
        decay = jnp.exp(p128[0:1] + s128[0:1])
        dec = jnp.concatenate([jnp.broadcast_to(decay[:, :kc], (vc, kc)),
                               jnp.broadcast_to(decay[:, kc:], (GLA_V - vc, kc))], axis=0)
        st_ref[...] = st * dec + jnp.where(state_mask, upd, 0.0)
        if finalize:
            ot = o + of_ref[0, rows, :]
            o2 = ot * ot
            hi = o2.astype(BF16)
            lo = (o2 - hi.astype(F32)).astype(BF16)
            ms = (jnp.dot(hi, bd_ref[...], preferred_element_type=F32)
                  + jnp.dot(lo, bd_ref[...], preferred_element_type=F32))
            y = ot * lax.rsqrt(ms + EPS) * ng_ref[...] * _silu(r_ref[0, rows, :])
            out_ref[0, rows, :] = y.astype(out_ref.dtype)
        else:
            out_ref[0, rows, :] = o

    @pl.when(single_level)
    def _():
        for gi in range(ngroups):
            group(gi, False)

    @pl.when(jnp.logical_not(single_level))
    def _():
        for gi in range(ngroups):
            group(gi, True)

    @pl.when(step == pl.num_programs(1) - 1)
    def _():
        sfin_ref[0] = st_ref[...]


def _gla_call(q, k, v, glr, wg, bg, tabs, s0, ts, reverse, fin=None):
    b, l, _ = q.shape
    nb = l // ts
    finalize = fin is not None
    cm, cm2, lm = tabs
    blk = (lambda bb, i: (bb, nb - 1 - i, 0)) if reverse else (lambda bb, i: (bb, i, 0))
    tok = lambda w: pl.BlockSpec((1, ts, w), blk)
    full = lambda a: pl.BlockSpec(a.shape, lambda bb, i: (0,) * a.ndim)
    state = pl.BlockSpec((1, GLA_V, QK_PAD // 2), lambda bb, i: (bb, 0, 0))
    in_specs = [tok(QK_PAD), tok(QK_PAD), tok(GLA_V), tok(GLR_PAD), full(wg), full(bg), full(cm), full(cm2),
                full(lm), state]
    args = [q, k, v, glr, wg, bg, cm, cm2, lm, s0]
    if finalize:
        r, o_other, ng, bd = fin
        in_specs += [tok(GLA_V), tok(GLA_V), full(ng), full(bd)]
        args += [r, o_other, ng, bd]
    out_dt = BF16 if finalize else F32
    return pl.pallas_call(
        functools.partial(_gla_kernel, reverse=reverse, finalize=finalize, ngroups=ts // GLA_GROUP),
        grid=(b, nb),
        in_specs=in_specs,
        out_specs=[tok(GLA_V), state],
        out_shape=[jax.ShapeDtypeStruct((b, l, GLA_V), out_dt),
                   jax.ShapeDtypeStruct((b, GLA_V, QK_PAD // 2), F32)],
        scratch_shapes=[pltpu.VMEM((GLA_V, QK_PAD // 2), F32), pltpu.VMEM((ts, QK_PAD), F32)],
        compiler_params=_cparams("arbitrary", "arbitrary"),
        name="gla_bwd" if reverse else "gla_fwd",
    )(*args)


def _na_bias_tables(rpb, rows):
    kh = min(NA_WIN_ROWS, rows)
    n_dr, n_dc = 2 * NA_WIN_ROWS - 1, 2 * NA_WIN_COLS - 1
    qr, kr = np.arange(NA_Q_ROWS)[:, None], np.arange(NA_K_ROWS)[None, :]
    row_pick, row_ok = [], []
    for r0, ks in ((0, 0), (NA_WIN_ROWS // 2, 0), (rows - NA_Q_ROWS, rows - NA_K_ROWS)):
        r, r2 = r0 + qr, ks + kr
        rs = np.clip(r - kh // 2, 0, rows - kh)
        row_ok.append((r2 >= rs) & (r2 < rs + kh))
        row_pick.append(np.eye(n_dr, dtype=np.float32)[np.clip(r2 - r + NA_WIN_ROWS - 1, 0, n_dr - 1)])
    row_pick, row_ok = np.stack(row_pick), np.stack(row_ok)
    qc, kc = np.arange(GRID_W)[:, None], np.arange(GRID_W)[None, :]
    cs = np.clip(qc - NA_WIN_COLS // 2, 0, GRID_W - NA_WIN_COLS)
    col_ok = (kc >= cs) & (kc < cs + NA_WIN_COLS)
    col_pick = np.eye(n_dc, dtype=np.float32)[np.clip(kc - qc + NA_WIN_COLS - 1, 0, n_dc - 1)]
    hi = lax.Precision.HIGHEST
    depth = rpb.shape[0]
    pairs = NA_K_ROWS // 2
    by_row = jnp.einsum('lhde,cqkd->lchqke', rpb.astype(F32), row_pick, precision=hi)
    by_pair = by_row.reshape(depth, 3, NA_HEADS, NA_Q_ROWS, pairs, 2 * n_dc)
    pick2 = np.zeros((2, n_dc, GRID_W, 2, GRID_W), np.float32)
    for k2 in range(2):
        pick2[k2, :, :, k2, :] = np.transpose(col_pick, (2, 0, 1))
    pick2 = pick2.reshape(2 * n_dc, GRID_W, 2 * GRID_W)
    bias = jnp.einsum('lchqjf,fxm->lchqjxm', by_pair, pick2, precision=hi)
    valid = (row_ok.reshape(3, NA_Q_ROWS, pairs, 1, 2, 1) & col_ok[None, None, None, :, None, :])
    valid = valid.reshape(3, 1, NA_Q_ROWS, pairs, GRID_W, 2 * GRID_W)
    return jnp.where(valid[None], bias, NEG_INF)


def _scores(qh, parts):
    out = []
    for kh, _, bias in parts:
        sc = lax.dot_general(qh, kh, NT_DIMS, preferred_element_type=F32)
        if bias is not None:
            n_qr, n_pair = sc.shape[0] // GRID_W, sc.shape[1] // (2 * GRID_W)
            sc = jnp.concatenate([jnp.concatenate(
                [sc[qr * GRID_W:(qr + 1) * GRID_W, j * LANES:(j + 1) * LANES] + bias(qr, j) for j in range(n_pair)],
                axis=1) for qr in range(n_qr)], axis=0)
        out.append(sc)
    return out


def _softmax_parts(s):
    m = functools.reduce(jnp.maximum, [jnp.max(t, axis=-1, keepdims=True) for t in s])
    p = [jnp.exp(t - m) for t in s]
    denom = functools.reduce(jnp.add, [jnp.sum(t, axis=-1, keepdims=True) for t in p])
    return [t.astype(BF16) for t in p], denom


def _weighted_values(p, denom, parts):
    o = functools.reduce(jnp.add, [jnp.dot(t, vh, preferred_element_type=F32) for t, (_, vh, _) in zip(p, parts)])
    return o / denom


def _attend_heads(q, parts_of_head, n_heads):
    s, p, outs = {}, {}, {}
    for t in range(n_heads + 2):
        if t < n_heads:
            s[t] = _scores(q[:, t * NA_DH:(t + 1) * NA_DH], parts_of_head(t))
        if 0 <= t - 1 < n_heads:
            p[t - 1] = _softmax_parts(s.pop(t - 1))
        if 0 <= t - 2 < n_heads:
            pp, denom = p.pop(t - 2)
            outs[t - 2] = _weighted_values(pp, denom, parts_of_head(t - 2))
    return jnp.concatenate([outs[h] for h in range(n_heads)], axis=1)


def _mix_residual(gla, na, u, u_prev, u_next, x, g1, wg, ws, wn, cw, cb, first, last):
    w = SC_WIDTH
    tm = u.shape[0]
    cx = u[:, w:2 * w] * u[:, 2 * w:3 * w]
    cx_prev = (u_prev[SUBLANES - 1:SUBLANES, w:2 * w] * u_prev[SUBLANES - 1:SUBLANES, 2 * w:3 * w]
               * jnp.where(first, 0.0, 1.0))
    cx_next = u_next[0:1, w:2 * w] * u_next[0:1, 2 * w:3 * w] * jnp.where(last, 0.0, 1.0)
    row = lax.broadcasted_iota(jnp.int32, cx.shape, 0)
    prev = jnp.where(row == 0, cx_prev, pltpu.roll(cx, 1, 0))
    nxt = jnp.where(row == tm - 1, cx_next, pltpu.roll(cx, tm - 1, 0))
    sc = u[:, 0:w] * (prev * cw[0:1] + cx * cw[1:2] + nxt * cw[2:3] + cb)
    mix = (jnp.dot(gla, wg, preferred_element_type=F32)
           + jnp.dot(sc.astype(BF16), ws, preferred_element_type=F32)
           + jnp.dot(na, wn, preferred_element_type=F32))
    return x + g1 * mix


def _na_kernel(q_ref, k_ref, v_ref, kc_ref, vc_ref, bias_ref, gla_ref, scu_ref, scp_ref, scn_ref, x_ref, g1_ref,
               wg_ref, ws_ref, wn_ref, cw_ref, cb_ref, o_ref, *, rows):
    i = pl.program_id(1)
    ks = jnp.clip(i * NA_Q_ROWS - NA_WIN_ROWS // 2, 0, rows - NA_K_ROWS)
    win = pl.ds(pl.multiple_of(ks * GRID_W, GRID_W), NA_K_ROWS * GRID_W)
    q = q_ref[0]
    kw = k_ref[0, win, :]
    vw = v_ref[0, win, :]
    kc = kc_ref[0]
    vc = vc_ref[0]
    def parts(h):
        sl = slice(h * NA_DH, (h + 1) * NA_DH)
        return [(kw[:, sl], vw[:, sl], lambda qr, j: bias_ref[0, h, qr, j]), (kc[:, sl], vc[:, sl], None)]

    na = _attend_heads(q, parts, NA_HEADS).astype(BF16)
    o_ref[0] = _mix_residual(gla_ref[0], na, scu_ref[0], scp_ref[0], scn_ref[0], x_ref[0], g1_ref[0],
                             wg_ref[...], ws_ref[...], wn_ref[...], cw_ref[...], cb_ref[...],
                             i == 0, i == pl.num_programs(1) - 1)


def _na_call(q, k, v, kc, vc, bias, gla, scu, x, g1, wg, ws, wn, cw, cb):
    b, l, w = q.shape
    d = x.shape[-1]
    c = kc.shape[1]
    rows = l // GRID_W
    nrb = rows // NA_Q_ROWS
    nq = NA_Q_ROWS * GRID_W
    nblk8, t8 = l // SUBLANES, nq // SUBLANES
    seq = lambda n: pl.BlockSpec((1, n, w), lambda bb, i: (bb, 0, 0))
    tok = lambda width: pl.BlockSpec((1, nq, width), lambda bb, i: (bb, i, 0))
    full = lambda a: pl.BlockSpec(a.shape, lambda bb, i: (0,) * a.ndim)
    case = lambda bb, i: (jnp.where(i == 0, 0, jnp.where(i == nrb - 1, 2, 1)),) + (0,) * (bias.ndim - 1)
    halo_prev = pl.BlockSpec((1, SUBLANES, 3 * SC_WIDTH), lambda bb, i: (bb, jnp.maximum(i * t8 - 1, 0), 0))
    halo_next = pl.BlockSpec((1, SUBLANES, 3 * SC_WIDTH), lambda bb, i: (bb, jnp.minimum((i + 1) * t8, nblk8 - 1), 0))
    return pl.pallas_call(
        functools.partial(_na_kernel, rows=rows),
        grid=(b, nrb),
        in_specs=[tok(w), seq(l), seq(l), seq(c), seq(c), pl.BlockSpec((1,) + bias.shape[1:], case),
                  tok(GLA_V), tok(3 * SC_WIDTH), halo_prev, halo_next, tok(d),
                  pl.BlockSpec((1, 1, d), lambda bb, i: (bb, 0, 0)),
                  full(wg), full(ws), full(wn), full(cw), full(cb)],
        out_specs=tok(d),
        out_shape=jax.ShapeDtypeStruct((b, l, d), F32),
        compiler_params=_cparams("arbitrary", "arbitrary"),
        name="na_out_proj",
    )(q, k, v, kc, vc, bias, gla, scu, scu, scu, x, g1, wg, ws, wn, cw, cb)


def _ctx_attn_kernel(q_ref, k_ref, v_ref, o_ref):
    q, k, v = q_ref[0], k_ref[0], v_ref[0]
    def parts(h):
        sl = slice(h * NA_DH, (h + 1) * NA_DH)
        return [(k[:, sl], v[:, sl], None)]

    o_ref[0] = _attend_heads(q, parts, NA_HEADS).astype(o_ref.dtype)


def _ctx_attn_call(q, k, v):
    b, c, w = q.shape
    spec = pl.BlockSpec((1, c, w), lambda bb: (bb, 0, 0))
    return pl.pallas_call(
        _ctx_attn_kernel, grid=(b,), in_specs=[spec, spec, spec], out_specs=spec,
        out_shape=jax.ShapeDtypeStruct((b, c, w), BF16),
        compiler_params=_cparams("arbitrary"), name="ctx_attn",
    )(q, k, v)


def _outproj_kernel(gla_ref, na_ref, scu_ref, scp_ref, scn_ref, x_ref, g1_ref,
                    wg_ref, ws_ref, wn_ref, cw_ref, cb_ref, o_ref):
    i = pl.program_id(1)
    o_ref[0] = _mix_residual(gla_ref[0], na_ref[0], scu_ref[0], scp_ref[0], scn_ref[0], x_ref[0], g1_ref[0],
                             wg_ref[...], ws_ref[...], wn_ref[...], cw_ref[...], cb_ref[...],
                             i == 0, i == pl.num_programs(1) - 1)


def _outproj_call(gla, na, scu, x, g1, wg, ws, wn, cw, cb, tm):
    b, l, d = x.shape
    bm = g1.shape[0]
    nblk8 = l // SUBLANES
    t8 = tm // SUBLANES
    mod_map = (lambda bb, i: (bb, 0, 0)) if bm > 1 else (lambda bb, i: (0, 0, 0))
    tok = lambda w: pl.BlockSpec((1, tm, w), lambda bb, i: (bb, i, 0))
    full = lambda a: pl.BlockSpec(a.shape, lambda bb, i: (0,) * a.ndim)
    halo_prev = pl.BlockSpec((1, SUBLANES, 3 * SC_WIDTH), lambda bb, i: (bb, jnp.maximum(i * t8 - 1, 0), 0))
    halo_next = pl.BlockSpec((1, SUBLANES, 3 * SC_WIDTH), lambda bb, i: (bb, jnp.minimum((i + 1) * t8, nblk8 - 1), 0))
    return pl.pallas_call(
        _outproj_kernel,
        grid=(b, l // tm),
        in_specs=[tok(GLA_V), tok(NA_W), tok(3 * SC_WIDTH), halo_prev, halo_next, tok(d),
                  pl.BlockSpec((1, 1, d), mod_map), full(wg), full(ws), full(wn), full(cw), full(cb)],
        out_specs=tok(d),
        out_shape=jax.ShapeDtypeStruct((b, l, d), F32),
        compiler_params=_cparams("arbitrary", "arbitrary"),
        name="out_proj",
    )(gla, na, scu, scu, scu, x, g1, wg, ws, wn, cw, cb)


def _ffn_kernel(*refs, final, chunk):
    if final:
        (x_ref, xp_ref, xn_ref, sh_ref, sc_ref, g2_ref, ng_ref, wup_ref, cw_ref, cb_ref, wdn_ref, fg_ref,
         o_ref, act_ref) = refs
    else:
        (x_ref, xp_ref, xn_ref, sh_ref, sc_ref, g2_ref, ng_ref, wup_ref, cw_ref, cb_ref, wdn_ref,
         o_ref, act_ref) = refs
    i = pl.program_id(1)
    last = pl.num_programs(1) - 1
    x = x_ref[0]
    tm = x.shape[0]
    hid = wdn_ref.shape[0]
    nm = lambda t: _norm_mod(t, ng_ref[...], sh_ref[0], sc_ref[0])
    hp = nm(xp_ref[0]) * jnp.where(i > 0, 1.0, 0.0)
    hn = nm(xn_ref[0]) * jnp.where(i < last, 1.0, 0.0)
    hb = jnp.concatenate([hp, nm(x), hn], axis=0).astype(BF16)
    ext = tm + 2 * SUBLANES
    inner = slice(SUBLANES, SUBLANES + tm)

    def conv(u, c0, w):
        cw = cw_ref[:, c0:c0 + w]
        return (pltpu.roll(u, 1, 0)[inner] * cw[0:1] + u[inner] * cw[1:2]
                + pltpu.roll(u, ext - 1, 0)[inner] * cw[2:3] + cb_ref[:, c0:c0 + w])

    for c0 in range(0, hid, chunk):
        w = min(chunk, hid - c0)
        ua = jnp.dot(hb, wup_ref[:, c0:c0 + w], preferred_element_type=F32)
        ub = jnp.dot(hb, wup_ref[:, hid + c0:hid + c0 + w], preferred_element_type=F32)
        act_ref[:, c0:c0 + w] = (_silu(conv(ua, c0, w)) * conv(ub, hid + c0, w)).astype(BF16)
    y = x + g2_ref[0] * jnp.dot(act_ref[...], wdn_ref[...], preferred_element_type=F32)
    if final:
        ms = jnp.mean(y * y, axis=-1, keepdims=True)
        y = y * lax.rsqrt(ms + EPS) * fg_ref[...]
    o_ref[0] = y


def _ffn_call(x, shift, scale, g2, gain, wup, cw, cb, wdn, final_gain, tm):
    b, l, d = x.shape
    bm = shift.shape[0]
    nblk8 = l // SUBLANES
    t8 = tm // SUBLANES
    final = final_gain is not None
    mod_map = (lambda bb, i: (bb, 0, 0)) if bm > 1 else (lambda bb, i: (0, 0, 0))
    tok = pl.BlockSpec((1, tm, d), lambda bb, i: (bb, i, 0))
    full = lambda a: pl.BlockSpec(a.shape, lambda bb, i: (0,) * a.ndim)
    resident = lambda a: pl.BlockSpec(a.shape, lambda bb, i: (0,) * a.ndim, pipeline_mode=pl.Buffered(1))
    halo_prev = pl.BlockSpec((1, SUBLANES, d), lambda bb, i: (bb, jnp.maximum(i * t8 - 1, 0), 0))
    halo_next = pl.BlockSpec((1, SUBLANES, d), lambda bb, i: (bb, jnp.minimum((i + 1) * t8, nblk8 - 1), 0))
    mod = pl.BlockSpec((1, 1, d), mod_map)
    in_specs = [tok, halo_prev, halo_next, mod, mod, mod, full(gain), resident(wup), full(cw), full(cb), resident(wdn)]
    args = [x, x, x, shift, scale, g2, gain, wup, cw, cb, wdn]
    if final:
        in_specs.append(full(final_gain))
        args.append(final_gain)
    return pl.pallas_call(
        functools.partial(_ffn_kernel, final=final, chunk=FFN_CHUNK),
        grid=(b, l // tm),
        in_specs=in_specs,
        out_specs=tok,
        out_shape=jax.ShapeDtypeStruct((b, l, d), F32),
        scratch_shapes=[pltpu.VMEM((tm, wdn.shape[0]), BF16)],
        compiler_params=_cparams("arbitrary", "arbitrary"),
        name="conv_ffn_final" if final else "conv_ffn",
    )(*args)


def _rope_tables(n_tokens):
    t = jnp.arange(n_tokens)
    n_freq = GLA_DK // 4
    inv_freq = ROPE_BASE ** (-jnp.arange(n_freq, dtype=F32) / n_freq)
    row = (t // GRID_W).astype(F32)[:, None] * inv_freq
    col = (t % GRID_W).astype(F32)[:, None] * inv_freq
    ang = jnp.concatenate([row, col], axis=-1)
    cos = jnp.repeat(jnp.cos(ang), 2, axis=-1)
    sin = jnp.repeat(jnp.sin(ang), 2, axis=-1) * jnp.tile(jnp.asarray([-1.0, 1.0], F32), GLA_DK // 2)
    pad = QK_PAD - GLA_QK
    cos = jnp.concatenate([jnp.tile(cos, (1, GLA_HEADS)), jnp.ones((n_tokens, pad), F32)], axis=-1)
    sin = jnp.concatenate([jnp.tile(sin, (1, GLA_HEADS)), jnp.zeros((n_tokens, pad), F32)], axis=-1)
    return cos, sin


def _tile_rows(n, target):
    t = min(n, target)
    assert n % t == 0 and t % GLA_GROUP == 0, (n, t)
    return t


def kernel(x, c, ctx, c_ctx, w_ada, b_ada, norm_mix_g, norm_ffn_g, w_in, gla_wg2_fw, gla_bg_fw, gla_wg2_bw, gla_bg_bw, gla_norm_g, sc_conv_w, sc_conv_b, na_rpb, w_out, ffn_w_up, ffn_conv_w, ffn_conv_b, ffn_w_down, final_norm_g):
    bsz, n_lat, d = x.shape
    n_ctx = ctx.shape[1]
    depth = w_in.shape[0]
    rows = n_lat // GRID_W
    assert n_lat % (NA_Q_ROWS * GRID_W) == 0 and rows >= NA_K_ROWS and bsz + 1 <= 2 * SUBLANES
    tm_lat, tm_ctx = _tile_rows(n_lat, LATENT_TILE), _tile_rows(n_ctx, LATENT_TILE)

    crows = jnp.zeros((2 * SUBLANES, d), F32).at[:bsz].set(c).at[bsz].set(c_ctx)
    mod_all = _ada_call(crows, w_ada, b_ada)

    zpad = lambda n: jnp.zeros((depth, d, n), BF16)
    o = np.cumsum([0, GLA_QK, GLA_QK, GLA_V, GLA_LOWRANK, GLA_LOWRANK, GLA_V,
                   SC_WIDTH, SC_WIDTH, SC_WIDTH, NA_W, NA_W, NA_W])
    w_in_b = w_in.astype(BF16)
    col = lambda n: w_in_b[:, :, o[n]:o[n + 1]]
    qpad = zpad(QK_PAD - GLA_QK)
    w_all = jnp.concatenate([col(0), qpad, col(1), qpad, col(2), col(5), col(3), col(4),
                             zpad(GLR_PAD - 2 * GLA_LOWRANK), w_in_b[:, :, o[6]:o[9]],
                             col(9) * jnp.asarray(NA_DH ** -0.5, BF16), w_in_b[:, :, o[10]:o[12]]], axis=-1)
    assert w_all.shape[-1] == INPROJ_WIDTH
    w_out_b = w_out.astype(BF16)
    wo_gla, wo_sc, wo_na = w_out_b[:, :GLA_V], w_out_b[:, GLA_V:GLA_V + SC_WIDTH], w_out_b[:, GLA_V + SC_WIDTH:]
    wup = ffn_w_up.astype(BF16)
    wdn = ffn_w_down.astype(BF16)

    def gate_w(w2, first_row):
        full = jnp.zeros((depth, GLR_PAD, QK_PAD), F32)
        return full.at[:, first_row:first_row + GLA_LOWRANK, :GLA_QK].set(w2).astype(BF16)

    def gate_b(bias):
        return jnp.zeros((depth, 1, QK_PAD), F32).at[:, 0, :GLA_QK].set(bias)

    wg_f, wg_b = gate_w(gla_wg2_fw, 0), gate_w(gla_wg2_bw, GLA_LOWRANK)
    bg_f, bg_b = gate_b(gla_bg_fw), gate_b(gla_bg_bw)
    gla_gain = jnp.tile(gla_norm_g, (1, GLA_HEADS))[:, None, :]
    hv = np.arange(GLA_V) // GLA_DV
    head_mean = jnp.asarray((hv[:, None] == hv[None, :]) / GLA_DV, BF16)
    tabs_f, tabs_b = _gla_tables(False), _gla_tables(True)
    rope_tabs = _rope_tables(n_lat)
    zero_state = jnp.zeros((bsz, GLA_V, QK_PAD // 2), F32)
    na_bias = _na_bias_tables(na_rpb, rows)

    xc = ctx
    for layer in range(depth):
        update_ctx = layer < depth - 1
        mod = mod_all[layer]
        sh1, sc1, g1, sh2, sc2, g2 = [mod[:bsz, n * d:(n + 1) * d][:, None, :] for n in range(6)]
        sh1c, sc1c, g1c, sh2c, sc2c, g2c = [mod[bsz:bsz + 1, n * d:(n + 1) * d][:, None, :] for n in range(6)]
        gain_mix = norm_mix_g[layer][None, :]
        gain_ffn = norm_ffn_g[layer][None, :]

        cq, ck, cv, cr, cglr, cscu, cnaq, cnak, cnav = _inproj_call(
            xc, sh1c, sc1c, gain_mix, w_all[layer], None, tm_ctx)
        lq, lk, lv, lr, lglr, lscu, lnaq, lnak, lnav = _inproj_call(
            x, sh1, sc1, gain_mix, w_all[layer], rope_tabs, tm_lat)

        fin = lambda r, o_f: (r, o_f, gla_gain[layer], head_mean)
        oc_f, state_f = _gla_call(cq, ck, cv, cglr, wg_f[layer], bg_f[layer], tabs_f, zero_state, tm_ctx, False)
        gla_ctx, state_b = _gla_call(cq, ck, cv, cglr, wg_b[layer], bg_b[layer], tabs_b, zero_state, tm_ctx, True,
                                     fin(cr, oc_f))
        ol_f, _ = _gla_call(lq, lk, lv, lglr, wg_f[layer], bg_f[layer], tabs_f, state_f, tm_lat, False)
        gla_lat, _ = _gla_call(lq, lk, lv, lglr, wg_b[layer], bg_b[layer], tabs_b, state_b, tm_lat, True,
                               fin(lr, ol_f))

        cw_sc, cb_sc = sc_conv_w[layer], sc_conv_b[layer][None, :]
        wo = (wo_gla[layer], wo_sc[layer], wo_na[layer])
        x = _na_call(lnaq, lnak, lnav, cnak, cnav, na_bias[layer], gla_lat, lscu, x, g1, *wo, cw_sc, cb_sc)
        ffn = (wup[layer], ffn_conv_w[layer], ffn_conv_b[layer][None, :], wdn[layer])
        x = _ffn_call(x, sh2, sc2, g2, gain_ffn, *ffn,
                      final_norm_g[None, :] if layer == depth - 1 else None, tm_lat)

        if update_ctx:
            na_ctx = _ctx_attn_call(cnaq, cnak, cnav)
            xc = _outproj_call(gla_ctx, na_ctx, cscu, xc, g1c, *wo, cw_sc, cb_sc, tm_ctx)
            xc = _ffn_call(xc, sh2c, sc2c, g2c, gain_ffn, *ffn, None, tm_ctx)
    return x
```

```python
import functools

import numpy as np
import jax
import jax.numpy as jnp
from jax import lax
from jax.experimental import pallas as pl
from jax.experimental.pallas import tpu as pltpu

F32 = jnp.float32
BF16 = jnp.bfloat16

GRID_W = 64
EPS = 1e-6
NEG_INF = -1e30
GLA_HEADS = 6
GLA_DK = 32
GLA_DV = 64
GLA_LOWRANK = 16
GLA_TAU = 16.0
ROPE_BASE = 10000.0
SC_WIDTH = 256
NA_HEADS = 6
NA_DH = 64
NA_WIN_ROWS = 8
NA_WIN_COLS = 16
GLA_QK = GLA_HEADS * GLA_DK
GLA_V = GLA_HEADS * GLA_DV
NA_W = NA_HEADS * NA_DH

LANES = 128
SUBLANES = 8
VMEM_LIMIT_BYTES = 56 * 1024 * 1024

QK_PAD = 2 * LANES
GLR_PAD = LANES
GLA_GROUP = 128
GLA_LEVELS = 4
FFN_CHUNK = 6 * LANES
LATENT_TILE = 1024
HEADS_A = LANES // GLA_DK
GLA_SINGLE_LEVEL_DECAY = 64.0
NA_Q_ROWS = 4
NA_K_ROWS = 12

NT_DIMS = (((1,), (1,)), ((), ()))


def _cparams(*sem):
    return pltpu.CompilerParams(dimension_semantics=sem, vmem_limit_bytes=VMEM_LIMIT_BYTES)


def _silu(t):
    return t / (1.0 + jnp.exp(-t))


def _norm_mod(t, gain, shift, scale):
    ms = jnp.mean(t * t, axis=-1, keepdims=True)
    return (t * lax.rsqrt(ms + EPS) * gain) * (1.0 + scale) + shift


def _ada_kernel(c_ref, w_ref, b_ref, o_ref):
    c = c_ref[...]
    o_ref[0] = jnp.dot(_silu(c), w_ref[0], preferred_element_type=F32,
                       precision=lax.Precision.HIGHEST) + b_ref[0]


def _ada_call(crows, w_ada, b_ada):
    depth, d, n = w_ada.shape
    rows = crows.shape[0]
    tn = 1536
    return pl.pallas_call(
        _ada_kernel,
        grid=(depth, n // tn),
        in_specs=[pl.BlockSpec((rows, d), lambda l, j: (0, 0)),
                  pl.BlockSpec((1, d, tn), lambda l, j: (l, 0, j)),
                  pl.BlockSpec((1, 1, tn), lambda l, j: (l, 0, j))],
        out_specs=pl.BlockSpec((1, rows, tn), lambda l, j: (l, 0, j)),
        out_shape=jax.ShapeDtypeStruct((depth, rows, n), F32),
        compiler_params=_cparams("arbitrary", "arbitrary"),
        name="ada_mod",
    )(crows, w_ada, b_ada.reshape(depth, 1, n))


INPROJ_OUTS = ((QK_PAD, F32), (QK_PAD, F32), (GLA_V, F32), (GLA_V, F32), (GLR_PAD, F32),
               (3 * SC_WIDTH, F32), (NA_W, BF16), (NA_W, BF16), (NA_W, BF16))
INPROJ_STARTS = tuple(int(s) for s in np.cumsum([0] + [w for w, _ in INPROJ_OUTS]))
INPROJ_WIDTH = INPROJ_STARTS[-1]
MXU_COLS = 2 * LANES
NAV_OUT = len(INPROJ_OUTS) - 1


def _rotary(t, c, s):
    lane = lax.broadcasted_iota(jnp.int32, t.shape, 1)
    swapped = jnp.where(lane % 2 == 0, pltpu.roll(t, LANES - 1, 1), pltpu.roll(t, 1, 1))
    return t * c + swapped * s


def _inproj_kernel(*refs, rope):
    if rope:
        x_ref, sh_ref, sc_ref, g_ref, cos_ref, sin_ref, w_ref = refs[:7]
        outs = refs[7:]
    else:
        x_ref, sh_ref, sc_ref, g_ref, w_ref = refs[:5]
        outs = refs[5:]
    navt_ref = outs[-1]
    nav_pieces = []
    hb = _norm_mod(x_ref[0], g_ref[...], sh_ref[0], sc_ref[0]).astype(BF16)
    for n0 in range(0, INPROJ_WIDTH, MXU_COLS):
        res = jnp.dot(hb, w_ref[:, n0:n0 + MXU_COLS], preferred_element_type=F32)
        for a in range(n0, n0 + MXU_COLS, LANES):
            idx = max(i for i, s0 in enumerate(INPROJ_STARTS[:-1]) if s0 <= a)
            off = a - INPROJ_STARTS[idx]
            piece = res[:, a - n0:a - n0 + LANES]
            if idx < 2:
                if rope:
                    piece = _rotary(piece, cos_ref[:, off:off + LANES], sin_ref[:, off:off + LANES])
                if idx == 0:
                    piece = piece * (GLA_DK ** -0.5)
            if idx == NAV_OUT:
                nav_pieces.append(piece)
            else:
                outs[idx][0, :, off:off + LANES] = piece.astype(outs[idx].dtype)
    navt_ref[0] = jnp.concatenate(nav_pieces, axis=1).T.astype(BF16)


def _inproj_call(x, shift, scale, gain, w_all, rope_tabs, tm):
    b, l, d = x.shape
    bm = shift.shape[0]
    rope = rope_tabs is not None
    mod_map = (lambda i, bb: (bb, 0, 0)) if bm > 1 else (lambda i, bb: (0, 0, 0))
    tok = lambda w: pl.BlockSpec((1, tm, w), lambda i, bb: (bb, i, 0))
    full = lambda a: pl.BlockSpec(a.shape, lambda i, bb: (0,) * a.ndim)
    in_specs = [tok(d), pl.BlockSpec((1, 1, d), mod_map), pl.BlockSpec((1, 1, d), mod_map), full(gain)]
    args = [x, shift, scale, gain]
    if rope:
        in_specs += [pl.BlockSpec((tm, QK_PAD), lambda i, bb: (i, 0))] * 2
        args += list(rope_tabs)
    in_specs.append(full(w_all))
    args.append(w_all)
    return pl.pallas_call(
        functools.partial(_inproj_kernel, rope=rope),
        grid=(l // tm, b),
        in_specs=in_specs,
        out_specs=([tok(w) for w, _ in INPROJ_OUTS[:NAV_OUT]]
                   + [pl.BlockSpec((1, NA_W, tm), lambda i, bb: (bb, 0, i))]),
        out_shape=([jax.ShapeDtypeStruct((b, l, w), dt) for w, dt in INPROJ_OUTS[:NAV_OUT]]
                   + [jax.ShapeDtypeStruct((b, NA_W, l), BF16)]),
        compiler_params=_cparams("arbitrary", "arbitrary"),
        name="in_proj_rope" if rope else "in_proj_ctx",
    )(*args)


def _gla_tables(reverse):
    t = np.arange(GLA_GROUP)
    i, j = t[:, None], t[None, :]
    before = (j >= i) if reverse else (j <= i)
    strictly_after = (j < i) if reverse else (j > i)
    maps = []
    for n in (16, 32, 64, 128):
        maps.append((i // n == j // n) & before)
    for n in (16, 32, 64, 128):
        maps.append((i // n == j // n) & strictly_after)
    masks = [(i // 16 == j // 16) & before]
    for n in (32, 64, 128):
        h = n // 2
        q_late, k_early = (i % n) >= h, (j % n) < h
        if reverse:
            q_late, k_early = (i % n) < h, (j % n) >= h
        masks.append((i // n == j // n) & q_late & k_early)
    cm = np.concatenate(maps, axis=0).astype(np.float32)
    cm2 = np.concatenate([maps[3], maps[7]], axis=0).astype(np.float32)
    lm = np.stack(masks, axis=0).astype(np.float32)
    return jnp.asarray(cm, BF16), jnp.asarray(cm2, BF16), jnp.asarray(lm, F32)


def _gla_kernel(*refs, reverse, finalize, ngroups):
    if finalize:
        (q_ref, k_ref, v_ref, glr_ref, wg_ref, bg_ref, cm_ref, cm2_ref, lm_ref, s0_ref,
         r_ref, of_ref, ng_ref, bd_ref, out_ref, sfin_ref, st_ref, gate_ref) = refs
    else:
        (q_ref, k_ref, v_ref, glr_ref, wg_ref, bg_ref, cm_ref, cm2_ref, lm_ref, s0_ref,
         out_ref, sfin_ref, st_ref, gate_ref) = refs
    step = pl.program_id(1)
    g = GLA_GROUP
    kc = QK_PAD // 2
    vc = HEADS_A * GLA_DV

    @pl.when(step == 0)
    def _():
        st_ref[...] = s0_ref[0]

    z = jnp.dot(glr_ref[0].astype(BF16), wg_ref[...], preferred_element_type=F32) + bg_ref[...]
    gate_all = (jnp.minimum(z, 0.0) - jnp.log(1.0 + jnp.exp(-jnp.abs(z)))) * (1.0 / GLA_TAU)
    gate_ref[...] = gate_all
    total = jnp.sum(gate_all.reshape(ngroups, g, QK_PAD), axis=1)
    single_level = jnp.min(total) > -GLA_SINGLE_LEVEL_DECAY

    k_head = lax.broadcasted_iota(jnp.int32, (g, kc), 1) // GLA_DK
    kmask = [k_head == h for h in range(HEADS_A)]
    va_head = lax.broadcasted_iota(jnp.int32, (g, vc), 1) // GLA_DV
    vb_head = lax.broadcasted_iota(jnp.int32, (g, GLA_V - vc), 1) // GLA_DV
    st_row = lax.broadcasted_iota(jnp.int32, (GLA_V, kc), 0)
    st_col = lax.broadcasted_iota(jnp.int32, (GLA_V, kc), 1) // GLA_DK
    state_mask = st_row // GLA_DV == st_col + jnp.where(st_row >= vc, HEADS_A, 0)
    lmask = [lm_ref[l] > 0.5 for l in range(GLA_LEVELS)]
    causal = functools.reduce(jnp.logical_or, lmask)

    def stack_heads(t, masks, n):
        return jnp.concatenate([jnp.where(masks[h], t, 0.0) for h in range(n)], axis=0).astype(BF16)

    def scores(qs, ks):
        sa = lax.dot_general(qs[:, :kc].astype(BF16), stack_heads(ks[:, :kc], kmask, HEADS_A), NT_DIMS,
                             preferred_element_type=F32)
        sb = lax.dot_general(qs[:, kc:].astype(BF16), stack_heads(ks[:, kc:], kmask, GLA_HEADS - HEADS_A),
                             NT_DIMS, preferred_element_type=F32)
        return jnp.concatenate([sa, sb], axis=1)

    def group(gi, robust):
        gidx = (ngroups - 1 - gi) if reverse else gi
        rows = pl.ds(gidx * g, g)
        q = q_ref[0, rows, :]
        k = k_ref[0, rows, :]
        v = v_ref[0, rows, :]
        gate = gate_ref[rows, :]
        ghi = gate.astype(BF16)
        glo = (gate - ghi.astype(F32)).astype(BF16)
        cmat = cm_ref[...] if robust else cm2_ref[...]
        cs = jnp.dot(cmat, ghi, preferred_element_type=F32) + jnp.dot(cmat, glo, preferred_element_type=F32)
        if robust:
            p16, p32, p64, p128, s16, s32, s64, s128 = [cs[n * g:(n + 1) * g] for n in range(8)]
            q0 = q * jnp.exp(p16)
            a = [scores(q0, k * jnp.exp(-p16)),
                 scores(q0, k * jnp.exp(s16)),
                 scores(q * jnp.exp(p32), k * jnp.exp(s32)),
                 scores(q * jnp.exp(p64), k * jnp.exp(s64))]
            blocks = []
            for h in range(GLA_HEADS):
                sl = slice(h * g, (h + 1) * g)
                blk = jnp.where(lmask[3], a[3][:, sl], 0.0)
                for l in (2, 1, 0):
                    blk = jnp.where(lmask[l], a[l][:, sl], blk)
                blocks.append(blk)
            q4 = q * jnp.exp(p128)
        else:
            p128, s128 = cs[0:g], cs[g:2 * g]
            q4 = q * jnp.exp(p128)
            a = scores(q4, k * jnp.exp(-p128))
            blocks = [jnp.where(causal, a[:, h * g:(h + 1) * g], 0.0) for h in range(GLA_HEADS)]
        amat_a = jnp.concatenate(blocks[:HEADS_A], axis=1).astype(BF16)
        amat_b = jnp.concatenate(blocks[HEADS_A:], axis=1).astype(BF16)
        v_a = stack_heads(v[:, :vc], [va_head == h for h in range(HEADS_A)], HEADS_A)
        v_b = stack_heads(v[:, vc:], [vb_head == h for h in range(GLA_HEADS - HEADS_A)], GLA_HEADS - HEADS_A)
        st = st_ref[...]
        stb = st.astype(BF16)
        q4b = q4.astype(BF16)
        o_a = (jnp.dot(amat_a, v_a, preferred_element_type=F32)
               + lax.dot_general(q4b[:, :kc], stb[:vc], NT_DIMS, preferred_element_type=F32))
        o_b = (jnp.dot(amat_b, v_b, preferred_element_type=F32)
               + lax.dot_general(q4b[:, kc:], stb[vc:], NT_DIMS, preferred_element_type=F32))
        o = jnp.concatenate([o_a, o_b], axis=1)
        vt = v.T.astype(BF16)
        k4 = (k * jnp.exp(s128)).astype(BF16)
        upd = jnp.concatenate([jnp.dot(vt[:vc], k4[:, :kc], preferred_element_type=F32),
                               jnp.dot(vt[vc:], k4[:, kc:], preferred_element_type=F32)], axis=0)
        decay = jnp.exp(p128[0:1] + s128[0:1])
        dec = jnp.concatenate([jnp.broadcast_to(decay[:, :kc], (vc, kc)),
                               jnp.broadcast_to(decay[:, kc:], (GLA_V - vc, kc))], axis=0)
        st_ref[...] = st * dec + jnp.where(state_mask, upd, 0.0)
        if finalize:
            ot = o + of_ref[0, rows, :]
            o2 = ot * ot
            hi = o2.astype(BF16)
            lo = (o2 - hi.astype(F32)).astype(BF16)
            ms = (jnp.dot(hi, bd_ref[...], preferred_element_type=F32)
                  + jnp.dot(lo, bd_ref[...], preferred_element_type=F32))
            y = ot * lax.rsqrt(ms + EPS) * ng_ref[...] * _silu(r_ref[0, rows, :])
            out_ref[0, rows, :] = y.astype(out_ref.dtype)
        else:
            out_ref[0, rows, :] = o

    @pl.when(single_level)
    def _():
        for gi in range(ngroups):
            group(gi, False)

    @pl.when(jnp.logical_not(single_level))
    def _():
        for gi in range(ngroups):
            group(gi, True)

    @pl.when(step == pl.num_programs(1) - 1)
    def _():
        sfin_ref[0] = st_ref[...]


def _gla_call(q, k, v, glr, wg, bg, tabs, s0, ts, reverse, fin=None):
    b, l, _ = q.shape
    nb = l // ts
    finalize = fin is not None
    cm, cm2, lm = tabs
    blk = (lambda bb, i: (bb, nb - 1 - i, 0)) if reverse else (lambda bb, i: (bb, i, 0))
    tok = lambda w: pl.BlockSpec((1, ts, w), blk)
    full = lambda a: pl.BlockSpec(a.shape, lambda bb, i: (0,) * a.ndim)
    state = pl.BlockSpec((1, GLA_V, QK_PAD // 2), lambda bb, i: (bb, 0, 0))
    in_specs = [tok(QK_PAD), tok(QK_PAD), tok(GLA_V), tok(GLR_PAD), full(wg), full(bg), full(cm), full(cm2),
                full(lm), state]
    args = [q, k, v, glr, wg, bg, cm, cm2, lm, s0]
    if finalize:
        r, o_other, ng, bd = fin
        in_specs += [tok(GLA_V), tok(GLA_V), full(ng), full(bd)]
        args += [r, o_other, ng, bd]
    out_dt = BF16 if finalize else F32
    return pl.pallas_call(
        functools.partial(_gla_kernel, reverse=reverse, finalize=finalize, ngroups=ts // GLA_GROUP),
        grid=(b, nb),
        in_specs=in_specs,
        out_specs=[tok(GLA_V), state],
        out_shape=[jax.ShapeDtypeStruct((b, l, GLA_V), out_dt),
                   jax.ShapeDtypeStruct((b, GLA_V, QK_PAD // 2), F32)],
        scratch_shapes=[pltpu.VMEM((GLA_V, QK_PAD // 2), F32), pltpu.VMEM((ts, QK_PAD), F32)],
        compiler_params=_cparams("arbitrary", "arbitrary"),
        name="gla_bwd" if reverse else "gla_fwd",
    )(*args)


def _na_bias_tables(rpb, rows):
    kh = min(NA_WIN_ROWS, rows)
    n_dr, n_dc = 2 * NA_WIN_ROWS - 1, 2 * NA_WIN_COLS - 1
    qr, kr = np.arange(NA_Q_ROWS)[:, None], np.arange(NA_K_ROWS)[None, :]
    row_pick, row_ok = [], []
    for r0, ks in ((0, 0), (NA_WIN_ROWS // 2, 0), (rows - NA_Q_ROWS, rows - NA_K_ROWS)):
        r, r2 = r0 + qr, ks + kr
        rs = np.clip(r - kh // 2, 0, rows - kh)
        row_ok.append((r2 >= rs) & (r2 < rs + kh))
        row_pick.append(np.eye(n_dr, dtype=np.float32)[np.clip(r2 - r + NA_WIN_ROWS - 1, 0, n_dr - 1)])
    row_pick, row_ok = np.stack(row_pick), np.stack(row_ok)
    qc, kc = np.arange(GRID_W)[:, None], np.arange(GRID_W)[None, :]
    cs = np.clip(qc - NA_WIN_COLS // 2, 0, GRID_W - NA_WIN_COLS)
    col_ok = (kc >= cs) & (kc < cs + NA_WIN_COLS)
    col_pick = np.eye(n_dc, dtype=np.float32)[np.clip(kc - qc + NA_WIN_COLS - 1, 0, n_dc - 1)]
    hi = lax.Precision.HIGHEST
    depth = rpb.shape[0]
    pairs = NA_Q_ROWS // 2
    by_row = jnp.einsum('lhde,ckqd->lchkqe', rpb.astype(F32), np.transpose(row_pick, (0, 2, 1, 3)), precision=hi)
    by_pair = by_row.reshape(depth, 3, NA_HEADS, NA_K_ROWS, pairs, 2 * n_dc)
    pick2 = np.zeros((2, n_dc, GRID_W, 2, GRID_W), np.float32)
    for q2 in range(2):
        pick2[q2, :, :, q2, :] = np.transpose(col_pick, (2, 1, 0))
    pick2 = pick2.reshape(2 * n_dc, GRID_W, 2 * GRID_W)
    bias = jnp.einsum('lchkjf,fym->lchkjym', by_pair, pick2, precision=hi)
    valid = (np.transpose(row_ok, (0, 2, 1)).reshape(3, NA_K_ROWS, pairs, 1, 2, 1)
             & col_ok.T[None, None, None, :, None, :])
    valid = valid.reshape(3, 1, NA_K_ROWS, pairs, GRID_W, 2 * GRID_W)
    return jnp.where(valid[None], bias, NEG_INF)


def _attend_heads(q, parts_of_pair, n_heads):
    lane = lax.broadcasted_iota(jnp.int32, (1, LANES), 1)
    half = [(lane < NA_DH).astype(BF16), (lane >= NA_DH).astype(BF16)]

    def scores(h):
        j, o = divmod(h, 2)
        qm = q[:, j * LANES:(j + 1) * LANES] * half[o]
        out = []
        for kp, _, bias in parts_of_pair(j):
            st = lax.dot_general(kp, qm, NT_DIMS, preferred_element_type=F32)
            if bias is not None:
                n_kr, n_jq = st.shape[0] // GRID_W, st.shape[1] // LANES
                st = jnp.concatenate([jnp.concatenate(
                    [st[kr * GRID_W:(kr + 1) * GRID_W, jq * LANES:(jq + 1) * LANES] + bias(o, kr, jq)
                     for jq in range(n_jq)], axis=1) for kr in range(n_kr)], axis=0)
            out.append(st)
        return out

    def softmax(s):
        m = functools.reduce(jnp.maximum, [jnp.max(t, axis=0, keepdims=True) for t in s])
        p = [jnp.exp(t - m) for t in s]
        denom = functools.reduce(jnp.add, [jnp.sum(t, axis=0, keepdims=True) for t in p])
        return [t.astype(BF16) for t in p], denom

    def values(h, p, denom):
        j, o = divmod(h, 2)
        acc = functools.reduce(jnp.add, [
            jnp.dot(vt[o * NA_DH:(o + 1) * NA_DH, :], pt, preferred_element_type=F32)
            for pt, (_, vt, _) in zip(p, parts_of_pair(j))])
        return acc / denom

    s, p, outs = {}, {}, {}
    for t in range(n_heads + 3):
        if t < n_heads:
            s[t] = scores(t)
        if 0 <= t - 1 < n_heads:
            p[t - 1] = softmax(s.pop(t - 1))
        if 0 <= t - 3 < n_heads:
            outs[t - 3] = values(t - 3, *p.pop(t - 3))
    return jnp.concatenate([outs[h] for h in range(n_heads)], axis=0).T


def _mix_residual(gla, na, u, u_prev, u_next, x, g1, wg, ws, wn, cw, cb, first, last):
    w = SC_WIDTH
    tm = u.shape[0]
    cx = u[:, w:2 * w] * u[:, 2 * w:3 * w]
    cx_prev = (u_prev[SUBLANES - 1:SUBLANES, w:2 * w] * u_prev[SUBLANES - 1:SUBLANES, 2 * w:3 * w]
               * jnp.where(first, 0.0, 1.0))
    cx_next = u_next[0:1, w:2 * w] * u_next[0:1, 2 * w:3 * w] * jnp.where(last, 0.0, 1.0)
    row = lax.broadcasted_iota(jnp.int32, cx.shape, 0)
    prev = jnp.where(row == 0, cx_prev, pltpu.roll(cx, 1, 0))
    nxt = jnp.where(row == tm - 1, cx_next, pltpu.roll(cx, tm - 1, 0))
    sc = u[:, 0:w] * (prev * cw[0:1] + cx * cw[1:2] + nxt * cw[2:3] + cb)
    mix = (jnp.dot(gla, wg, preferred_element_type=F32)
           + jnp.dot(sc.astype(BF16), ws, preferred_element_type=F32)
           + jnp.dot(na, wn, preferred_element_type=F32))
    return x + g1 * mix


def _na_kernel(q_ref, k_ref, vt_ref, kc_ref, vct_ref, bias_ref, gla_ref, scu_ref, scp_ref, scn_ref, x_ref, g1_ref,
               wg_ref, ws_ref, wn_ref, cw_ref, cb_ref, o_ref, *, rows):
    i = pl.program_id(1)
    ks = jnp.clip(i * NA_Q_ROWS - NA_WIN_ROWS // 2, 0, rows - NA_K_ROWS)
    win = pl.ds(pl.multiple_of(ks * GRID_W, NA_Q_ROWS * GRID_W), NA_K_ROWS * GRID_W)
    kw = k_ref[0, win, :]
    vtw = vt_ref[0, :, win]
    kc = kc_ref[0]
    vct = vct_ref[0]

    def parts(j):
        sl = slice(j * LANES, (j + 1) * LANES)
        return [(kw[:, sl], vtw[sl, :], lambda o, kr, jq: bias_ref[0, 2 * j + o, kr, jq]),
                (kc[:, sl], vct[sl, :], None)]

    na = _attend_heads(q_ref[0], parts, NA_HEADS).astype(BF16)
    o_ref[0] = _mix_residual(gla_ref[0], na, scu_ref[0], scp_ref[0], scn_ref[0], x_ref[0], g1_ref[0],
                             wg_ref[...], ws_ref[...], wn_ref[...], cw_ref[...], cb_ref[...],
                             i == 0, i == pl.num_programs(1) - 1)


def _na_call(q, k, vt, kc, vct, bias, gla, scu, x, g1, wg, ws, wn, cw, cb):
    b, l, w = q.shape
    d = x.shape[-1]
    c = kc.shape[1]
    rows = l // GRID_W
    nrb = rows // NA_Q_ROWS
    nq = NA_Q_ROWS * GRID_W
    nblk8, t8 = l // SUBLANES, nq // SUBLANES
    seq = lambda n: pl.BlockSpec((1, n, w), lambda bb, i: (bb, 0, 0))
    seq_t = lambda n: pl.BlockSpec((1, w, n), lambda bb, i: (bb, 0, 0))
    tok = lambda width: pl.BlockSpec((1, nq, width), lambda bb, i: (bb, i, 0))
    full = lambda a: pl.BlockSpec(a.shape, lambda bb, i: (0,) * a.ndim)
    case = lambda bb, i: (jnp.where(i == 0, 0, jnp.where(i == nrb - 1, 2, 1)),) + (0,) * (bias.ndim - 1)
    halo_prev = pl.BlockSpec((1, SUBLANES, 3 * SC_WIDTH), lambda bb, i: (bb, jnp.maximum(i * t8 - 1, 0), 0))
    halo_next = pl.BlockSpec((1, SUBLANES, 3 * SC_WIDTH), lambda bb, i: (bb, jnp.minimum((i + 1) * t8, nblk8 - 1), 0))
    return pl.pallas_call(
        functools.partial(_na_kernel, rows=rows),
        grid=(b, nrb),
        in_specs=[tok(w), seq(l), seq_t(l), seq(c), seq_t(c), pl.BlockSpec((1,) + bias.shape[1:], case),
                  tok(GLA_V), tok(3 * SC_WIDTH), halo_prev, halo_next, tok(d),
                  pl.BlockSpec((1, 1, d), lambda bb, i: (bb, 0, 0)),
                  full(wg), full(ws), full(wn), full(cw), full(cb)],
        out_specs=tok(d),
        out_shape=jax.ShapeDtypeStruct((b, l, d), F32),
        compiler_params=_cparams("arbitrary", "arbitrary"),
        name="na_out_proj",
    )(q, k, vt, kc, vct, bias, gla, scu, scu, scu, x, g1, wg, ws, wn, cw, cb)


def _ctx_attn_kernel(q_ref, k_ref, vt_ref, o_ref):
    k, vt = k_ref[0], vt_ref[0]

    def parts(j):
        sl = slice(j * LANES, (j + 1) * LANES)
        return [(k[:, sl], vt[sl, :], None)]

    o_ref[0] = _attend_heads(q_ref[0], parts, NA_HEADS).astype(o_ref.dtype)


def _ctx_attn_call(q, k, vt):
    b, c, w = q.shape
    spec = pl.BlockSpec((1, c, w), lambda bb: (bb, 0, 0))
    spec_t = pl.BlockSpec((1, w, c), lambda bb: (bb, 0, 0))
    return pl.pallas_call(
        _ctx_attn_kernel, grid=(b,), in_specs=[spec, spec, spec_t], out_specs=spec,
        out_shape=jax.ShapeDtypeStruct((b, c, w), BF16),
        compiler_params=_cparams("arbitrary"), name="ctx_attn",
    )(q, k, vt)


def _outproj_kernel(gla_ref, na_ref, scu_ref, scp_ref, scn_ref, x_ref, g1_ref,
                    wg_ref, ws_ref, wn_ref, cw_ref, cb_ref, o_ref):
    i = pl.program_id(1)
    o_ref[0] = _mix_residual(gla_ref[0], na_ref[0], scu_ref[0], scp_ref[0], scn_ref[0], x_ref[0], g1_ref[0],
                             wg_ref[...], ws_ref[...], wn_ref[...], cw_ref[...], cb_ref[...],
                             i == 0, i == pl.num_programs(1) - 1)


def _outproj_call(gla, na, scu, x, g1, wg, ws, wn, cw, cb, tm):
    b, l, d = x.shape
    bm = g1.shape[0]
    nblk8 = l // SUBLANES
    t8 = tm // SUBLANES
    mod_map = (lambda bb, i: (bb, 0, 0)) if bm > 1 else (lambda bb, i: (0, 0, 0))
    tok = lambda w: pl.BlockSpec((1, tm, w), lambda bb, i: (bb, i, 0))
    full = lambda a: pl.BlockSpec(a.shape, lambda bb, i: (0,) * a.ndim)
    halo_prev = pl.BlockSpec((1, SUBLANES, 3 * SC_WIDTH), lambda bb, i: (bb, jnp.maximum(i * t8 - 1, 0), 0))
    halo_next = pl.BlockSpec((1, SUBLANES, 3 * SC_WIDTH), lambda bb, i: (bb, jnp.minimum((i + 1) * t8, nblk8 - 1), 0))
    return pl.pallas_call(
        _outproj_kernel,
        grid=(b, l // tm),
        in_specs=[tok(GLA_V), tok(NA_W), tok(3 * SC_WIDTH), halo_prev, halo_next, tok(d),
                  pl.BlockSpec((1, 1, d), mod_map), full(wg), full(ws), full(wn), full(cw), full(cb)],
        out_specs=tok(d),
        out_shape=jax.ShapeDtypeStruct((b, l, d), F32),
        compiler_params=_cparams("arbitrary", "arbitrary"),
        name="out_proj",
    )(gla, na, scu, scu, scu, x, g1, wg, ws, wn, cw, cb)


def _ffn_kernel(*refs, final, chunk):
    if final:
        (x_ref, xp_ref, xn_ref, sh_ref, sc_ref, g2_ref, ng_ref, wup_ref, cw_ref, cb_ref, wdn_ref, fg_ref,
         o_ref, act_ref) = refs
    else:
        (x_ref, xp_ref, xn_ref, sh_ref, sc_ref, g2_ref, ng_ref, wup_ref, cw_ref, cb_ref, wdn_ref,
         o_ref, act_ref) = refs
    i = pl.program_id(1)
    last = pl.num_programs(1) - 1
    x = x_ref[0]
    tm = x.shape[0]
    hid = wdn_ref.shape[0]
    nm = lambda t: _norm_mod(t, ng_ref[...], sh_ref[0], sc_ref[0])
    hp = nm(xp_ref[0]) * jnp.where(i > 0, 1.0, 0.0)
    hn = nm(xn_ref[0]) * jnp.where(i < last, 1.0, 0.0)
    hb = jnp.concatenate([hp, nm(x), hn], axis=0).astype(BF16)
    ext = tm + 2 * SUBLANES
    inner = slice(SUBLANES, SUBLANES + tm)

    def conv(u, c0, w):
        cw = cw_ref[:, c0:c0 + w]
        return (pltpu.roll(u, 1, 0)[inner] * cw[0:1] + u[inner] * cw[1:2]
                + pltpu.roll(u, ext - 1, 0)[inner] * cw[2:3] + cb_ref[:, c0:c0 + w])

    for c0 in range(0, hid, chunk):
        w = min(chunk, hid - c0)
        ua = jnp.dot(hb, wup_ref[:, c0:c0 + w], preferred_element_type=F32)
        ub = jnp.dot(hb, wup_ref[:, hid + c0:hid + c0 + w], preferred_element_type=F32)
        act_ref[:, c0:c0 + w] = (_silu(conv(ua, c0, w)) * conv(ub, hid + c0, w)).astype(BF16)
    y = x + g2_ref[0] * jnp.dot(act_ref[...], wdn_ref[...], preferred_element_type=F32)
    if final:
        ms = jnp.mean(y * y, axis=-1, keepdims=True)
        y = y * lax.rsqrt(ms + EPS) * fg_ref[...]
    o_ref[0] = y


def _ffn_call(x, shift, scale, g2, gain, wup, cw, cb, wdn, final_gain, tm):
    b, l, d = x.shape
    bm = shift.shape[0]
    nblk8 = l // SUBLANES
    t8 = tm // SUBLANES
    final = final_gain is not None
    mod_map = (lambda bb, i: (bb, 0, 0)) if bm > 1 else (lambda bb, i: (0, 0, 0))
    tok = pl.BlockSpec((1, tm, d), lambda bb, i: (bb, i, 0))
    full = lambda a: pl.BlockSpec(a.shape, lambda bb, i: (0,) * a.ndim)
    resident = lambda a: pl.BlockSpec(a.shape, lambda bb, i: (0,) * a.ndim, pipeline_mode=pl.Buffered(1))
    halo_prev = pl.BlockSpec((1, SUBLANES, d), lambda bb, i: (bb, jnp.maximum(i * t8 - 1, 0), 0))
    halo_next = pl.BlockSpec((1, SUBLANES, d), lambda bb, i: (bb, jnp.minimum((i + 1) * t8, nblk8 - 1), 0))
    mod = pl.BlockSpec((1, 1, d), mod_map)
    in_specs = [tok, halo_prev, halo_next, mod, mod, mod, full(gain), resident(wup), full(cw), full(cb), resident(wdn)]
    args = [x, x, x, shift, scale, g2, gain, wup, cw, cb, wdn]
    if final:
        in_specs.append(full(final_gain))
        args.append(final_gain)
    return pl.pallas_call(
        functools.partial(_ffn_kernel, final=final, chunk=FFN_CHUNK),
        grid=(b, l // tm),
        in_specs=in_specs,
        out_specs=tok,
        out_shape=jax.ShapeDtypeStruct((b, l, d), F32),
        scratch_shapes=[pltpu.VMEM((tm, wdn.shape[0]), BF16)],
        compiler_params=_cparams("arbitrary", "arbitrary"),
        name="conv_ffn_final" if final else "conv_ffn",
    )(*args)


def _rope_tables(n_tokens):
    t = jnp.arange(n_tokens)
    n_freq = GLA_DK // 4
    inv_freq = ROPE_BASE ** (-jnp.arange(n_freq, dtype=F32) / n_freq)
    row = (t // GRID_W).astype(F32)[:, None] * inv_freq
    col = (t % GRID_W).astype(F32)[:, None] * inv_freq
    ang = jnp.concatenate([row, col], axis=-1)
    cos = jnp.repeat(jnp.cos(ang), 2, axis=-1)
    sin = jnp.repeat(jnp.sin(ang), 2, axis=-1) * jnp.tile(jnp.asarray([-1.0, 1.0], F32), GLA_DK // 2)
    pad = QK_PAD - GLA_QK
    cos = jnp.concatenate([jnp.tile(cos, (1, GLA_HEADS)), jnp.ones((n_tokens, pad), F32)], axis=-1)
    sin = jnp.concatenate([jnp.tile(sin, (1, GLA_HEADS)), jnp.zeros((n_tokens, pad), F32)], axis=-1)
    return cos, sin


def _tile_rows(n, target):
    t = min(n, target)
    assert n % t == 0 and t % GLA_GROUP == 0, (n, t)
    return t


def kernel(x, c, ctx, c_ctx, w_ada, b_ada, norm_mix_g, norm_ffn_g, w_in, gla_wg2_fw, gla_bg_fw, gla_wg2_bw, gla_bg_bw, gla_norm_g, sc_conv_w, sc_conv_b, na_rpb, w_out, ffn_w_up, ffn_conv_w, ffn_conv_b, ffn_w_down, final_norm_g):
    bsz, n_lat, d = x.shape
    n_ctx = ctx.shape[1]
    depth = w_in.shape[0]
    rows = n_lat // GRID_W
    assert n_lat % (NA_Q_ROWS * GRID_W) == 0 and rows >= NA_K_ROWS and bsz + 1 <= 2 * SUBLANES
    tm_lat, tm_ctx = _tile_rows(n_lat, LATENT_TILE), _tile_rows(n_ctx, LATENT_TILE)

    crows = jnp.zeros((2 * SUBLANES, d), F32).at[:bsz].set(c).at[bsz].set(c_ctx)
    mod_all = _ada_call(crows, w_ada, b_ada)

    zpad = lambda n: jnp.zeros((depth, d, n), BF16)
    o = np.cumsum([0, GLA_QK, GLA_QK, GLA_V, GLA_LOWRANK, GLA_LOWRANK, GLA_V,
                   SC_WIDTH, SC_WIDTH, SC_WIDTH, NA_W, NA_W, NA_W])
    w_in_b = w_in.astype(BF16)
    col = lambda n: w_in_b[:, :, o[n]:o[n + 1]]
    qpad = zpad(QK_PAD - GLA_QK)
    w_all = jnp.concatenate([col(0), qpad, col(1), qpad, col(2), col(5), col(3), col(4),
                             zpad(GLR_PAD - 2 * GLA_LOWRANK), w_in_b[:, :, o[6]:o[9]],
                             col(9) * jnp.asarray(NA_DH ** -0.5, BF16), w_in_b[:, :, o[10]:o[12]]], axis=-1)
    assert w_all.shape[-1] == INPROJ_WIDTH
    w_out_b = w_out.astype(BF16)
    wo_gla, wo_sc, wo_na = w_out_b[:, :GLA_V], w_out_b[:, GLA_V:GLA_V + SC_WIDTH], w_out_b[:, GLA_V + SC_WIDTH:]
    wup = ffn_w_up.astype(BF16)
    wdn = ffn_w_down.astype(BF16)

    def gate_w(w2, first_row):
        full = jnp.zeros((depth, GLR_PAD, QK_PAD), F32)
        return full.at[:, first_row:first_row + GLA_LOWRANK, :GLA_QK].set(w2).astype(BF16)

    def gate_b(bias):
        return jnp.zeros((depth, 1, QK_PAD), F32).at[:, 0, :GLA_QK].set(bias)

    wg_f, wg_b = gate_w(gla_wg2_fw, 0), gate_w(gla_wg2_bw, GLA_LOWRANK)
    bg_f, bg_b = gate_b(gla_bg_fw), gate_b(gla_bg_bw)
    gla_gain = jnp.tile(gla_norm_g, (1, GLA_HEADS))[:, None, :]
    hv = np.arange(GLA_V) // GLA_DV
    head_mean = jnp.asarray((hv[:, None] == hv[None, :]) / GLA_DV, BF16)
    tabs_f, tabs_b = _gla_tables(False), _gla_tables(True)
    rope_tabs = _rope_tables(n_lat)
    zero_state = jnp.zeros((bsz, GLA_V, QK_PAD // 2), F32)
    na_bias = _na_bias_tables(na_rpb, rows)

    xc = ctx
    for layer in range(depth):
        update_ctx = layer < depth - 1
        mod = mod_all[layer]
        sh1, sc1, g1, sh2, sc2, g2 = [mod[:bsz, n * d:(n + 1) * d][:, None, :] for n in range(6)]
        sh1c, sc1c, g1c, sh2c, sc2c, g2c = [mod[bsz:bsz + 1, n * d:(n + 1) * d][:, None, :] for n in range(6)]
        gain_mix = norm_mix_g[layer][None, :]
        gain_ffn = norm_ffn_g[layer][None, :]

        cq, ck, cv, cr, cglr, cscu, cnaq, cnak, cnavt = _inproj_call(
            xc, sh1c, sc1c, gain_mix, w_all[layer], None, tm_ctx)
        lq, lk, lv, lr, lglr, lscu, lnaq, lnak, lnavt = _inproj_call(
            x, sh1, sc1, gain_mix, w_all[layer], rope_tabs, tm_lat)

        fin = lambda r, o_f: (r, o_f, gla_gain[layer], head_mean)
        oc_f, state_f = _gla_call(cq, ck, cv, cglr, wg_f[layer], bg_f[layer], tabs_f, zero_state, tm_ctx, False)
        gla_ctx, state_b = _gla_call(cq, ck, cv, cglr, wg_b[layer], bg_b[layer], tabs_b, zero_state, tm_ctx, True,
                                     fin(cr, oc_f))
        ol_f, _ = _gla_call(lq, lk, lv, lglr, wg_f[layer], bg_f[layer], tabs_f, state_f, tm_lat, False)
        gla_lat, _ = _gla_call(lq, lk, lv, lglr, wg_b[layer], bg_b[layer], tabs_b, state_b, tm_lat, True,
                               fin(lr, ol_f))

        cw_sc, cb_sc = sc_conv_w[layer], sc_conv_b[layer][None, :]
        wo = (wo_gla[layer], wo_sc[layer], wo_na[layer])
        x = _na_call(lnaq, lnak, lnavt, cnak, cnavt, na_bias[layer], gla_lat, lscu, x, g1, *wo, cw_sc, cb_sc)
        ffn = (wup[layer], ffn_conv_w[layer], ffn_conv_b[layer][None, :], wdn[layer])
        x = _ffn_call(x, sh2, sc2, g2, gain_ffn, *ffn,
                      final_norm_g[None, :] if layer == depth - 1 else None, tm_lat)

        if update_ctx:
            na_ctx = _ctx_attn_call(cnaq, cnak, cnavt)
            xc = _outproj_call(gla_ctx, na_ctx, cscu, xc, g1c, *wo, cw_sc, cb_sc, tm_ctx)
            xc = _ffn_call(xc, sh2c, sc2c, g2c, gain_ffn, *ffn, None, tm_ctx)
    return x
```

```python
import functools

import numpy as np
import jax
import jax.numpy as jnp
from jax import lax
from jax.experimental import pallas as pl
from jax.experimental.pallas import tpu as pltpu

F32 = jnp.float32
BF16 = jnp.bfloat16

GRID_W = 64
EPS = 1e-6
NEG_INF = -1e30
GLA_HEADS = 6
GLA_DK = 32
GLA_DV = 64
GLA_LOWRANK = 16
GLA_TAU = 16.0
ROPE_BASE = 10000.0
SC_WIDTH = 256
NA_HEADS = 6
NA_DH = 64
NA_WIN_ROWS = 8
NA_WIN_COLS = 16
GLA_QK = GLA_HEADS * GLA_DK
GLA_V = GLA_HEADS * GLA_DV
NA_W = NA_HEADS * NA_DH

LANES = 128
SUBLANES = 8
VMEM_LIMIT_BYTES = 56 * 1024 * 1024

QK_PAD = 2 * LANES
GLR_PAD = LANES
GLA_GROUP = 128
GLA_LEVELS = 4
FFN_CHUNK = 6 * LANES
LATENT_TILE = 1024
HEADS_A = LANES // GLA_DK
GLA_SINGLE_LEVEL_DECAY = 64.0
NA_Q_ROWS = 4
NA_K_ROWS = 12

NT_DIMS = (((1,), (1,)), ((), ()))


def _cparams(*sem):
    return pltpu.CompilerParams(dimension_semantics=sem, vmem_limit_bytes=VMEM_LIMIT_BYTES)


def _silu(t):
    return t / (1.0 + jnp.exp(-t))


def _norm_mod(t, gain, shift, scale):
    ms = jnp.mean(t * t, axis=-1, keepdims=True)
    return (t * lax.rsqrt(ms + EPS) * gain) * (1.0 + scale) + shift


def _ada_kernel(c_ref, w_ref, b_ref, o_ref):
    c = c_ref[...]
    o_ref[0] = jnp.dot(_silu(c), w_ref[0], preferred_element_type=F32,
                       precision=lax.Precision.HIGHEST) + b_ref[0]


def _ada_call(crows, w_ada, b_ada):
    depth, d, n = w_ada.shape
    rows = crows.shape[0]
    tn = 1536
    return pl.pallas_call(
        _ada_kernel,
        grid=(depth, n // tn),
        in_specs=[pl.BlockSpec((rows, d), lambda l, j: (0, 0)),
                  pl.BlockSpec((1, d, tn), lambda l, j: (l, 0, j)),
                  pl.BlockSpec((1, 1, tn), lambda l, j: (l, 0, j))],
        out_specs=pl.BlockSpec((1, rows, tn), lambda l, j: (l, 0, j)),
        out_shape=jax.ShapeDtypeStruct((depth, rows, n), F32),
        compiler_params=_cparams("arbitrary", "arbitrary"),
        name="ada_mod",
    )(crows, w_ada, b_ada.reshape(depth, 1, n))


INPROJ_OUTS = ((QK_PAD, F32), (QK_PAD, F32), (GLA_V, F32), (GLA_V, F32), (GLR_PAD, F32),
               (3 * SC_WIDTH, F32), (NA_W, BF16), (NA_W, BF16), (NA_W, BF16))
INPROJ_STARTS = tuple(int(s) for s in np.cumsum([0] + [w for w, _ in INPROJ_OUTS]))
INPROJ_WIDTH = INPROJ_STARTS[-1]
MXU_COLS = 2 * LANES
NAV_OUT = len(INPROJ_OUTS) - 1


def _rotary(t, c, s):
    lane = lax.broadcasted_iota(jnp.int32, t.shape, 1)
    swapped = jnp.where(lane % 2 == 0, pltpu.roll(t, LANES - 1, 1), pltpu.roll(t, 1, 1))
    return t * c + swapped * s


def _inproj_kernel(*refs, rope):
    if rope:
        x_ref, sh_ref, sc_ref, g_ref, cos_ref, sin_ref, w_ref = refs[:7]
        outs = refs[7:]
    else:
        x_ref, sh_ref, sc_ref, g_ref, w_ref = refs[:5]
        outs = refs[5:]
    navt_ref = outs[-1]
    nav_pieces = []
    hb = _norm_mod(x_ref[0], g_ref[...], sh_ref[0], sc_ref[0]).astype(BF16)
    for n0 in range(0, INPROJ_WIDTH, MXU_COLS):
        res = jnp.dot(hb, w_ref[:, n0:n0 + MXU_COLS], preferred_element_type=F32)
        for a in range(n0, n0 + MXU_COLS, LANES):
            idx = max(i for i, s0 in enumerate(INPROJ_STARTS[:-1]) if s0 <= a)
            off = a - INPROJ_STARTS[idx]
            piece = res[:, a - n0:a - n0 + LANES]
            if idx < 2:
                if rope:
                    piece = _rotary(piece, cos_ref[:, off:off + LANES], sin_ref[:, off:off + LANES])
                if idx == 0:
                    piece = piece * (GLA_DK ** -0.5)
            if idx == NAV_OUT:
                nav_pieces.append(piece)
            else:
                outs[idx][0, :, off:off + LANES] = piece.astype(outs[idx].dtype)
    navt_ref[0] = jnp.concatenate(nav_pieces, axis=1).T.astype(BF16)


def _inproj_call(x, shift, scale, gain, w_all, rope_tabs, tm):
    b, l, d = x.shape
    bm = shift.shape[0]
    rope = rope_tabs is not None
    mod_map = (lambda i, bb: (bb, 0, 0)) if bm > 1 else (lambda i, bb: (0, 0, 0))
    tok = lambda w: pl.BlockSpec((1, tm, w), lambda i, bb: (bb, i, 0))
    full = lambda a: pl.BlockSpec(a.shape, lambda i, bb: (0,) * a.ndim)
    in_specs = [tok(d), pl.BlockSpec((1, 1, d), mod_map), pl.BlockSpec((1, 1, d), mod_map), full(gain)]
    args = [x, shift, scale, gain]
    if rope:
        in_specs += [pl.BlockSpec((tm, QK_PAD), lambda i, bb: (i, 0))] * 2
        args += list(rope_tabs)
    in_specs.append(full(w_all))
    args.append(w_all)
    return pl.pallas_call(
        functools.partial(_inproj_kernel, rope=rope),
        grid=(l // tm, b),
        in_specs=in_specs,
        out_specs=([tok(w) for w, _ in INPROJ_OUTS[:NAV_OUT]]
                   + [pl.BlockSpec((1, NA_W, tm), lambda i, bb: (bb, 0, i))]),
        out_shape=([jax.ShapeDtypeStruct((b, l, w), dt) for w, dt in INPROJ_OUTS[:NAV_OUT]]
                   + [jax.ShapeDtypeStruct((b, NA_W, l), BF16)]),
        compiler_params=_cparams("arbitrary", "arbitrary"),
        name="in_proj_rope" if rope else "in_proj_ctx",
    )(*args)


def _gla_tables(reverse):
    t = np.arange(GLA_GROUP)
    i, j = t[:, None], t[None, :]
    before = (j >= i) if reverse else (j <= i)
    strictly_after = (j < i) if reverse else (j > i)
    maps = []
    for n in (16, 32, 64, 128):
        maps.append((i // n == j // n) & before)
    for n in (16, 32, 64, 128):
        maps.append((i // n == j // n) & strictly_after)
    masks = [(i // 16 == j // 16) & before]
    for n in (32, 64, 128):
        h = n // 2
        q_late, k_early = (i % n) >= h, (j % n) < h
        if reverse:
            q_late, k_early = (i % n) < h, (j % n) >= h
        masks.append((i // n == j // n) & q_late & k_early)
    cm = np.concatenate(maps, axis=0).astype(np.float32)
    cm2 = np.concatenate([maps[3], maps[7]], axis=0).astype(np.float32)
    lm = np.stack(masks, axis=0).astype(np.float32)
    return jnp.asarray(cm, BF16), jnp.asarray(cm2, BF16), jnp.asarray(lm, F32)


def _gla_kernel(*refs, reverse, finalize, ngroups):
    if finalize:
        (q_ref, k_ref, v_ref, glr_ref, wg_ref, bg_ref, cm_ref, cm2_ref, lm_ref, s0_ref,
         r_ref, of_ref, ng_ref, bd_ref, out_ref, sfin_ref, st_ref, gate_ref) = refs
    else:
        (q_ref, k_ref, v_ref, glr_ref, wg_ref, bg_ref, cm_ref, cm2_ref, lm_ref, s0_ref,
         out_ref, sfin_ref, st_ref, gate_ref) = refs
    step = pl.program_id(1)
    g = GLA_GROUP
    kc = QK_PAD // 2
    vc = HEADS_A * GLA_DV

    @pl.when(step == 0)
    def _():
        st_ref[...] = s0_ref[0]

    z = jnp.dot(glr_ref[0].astype(BF16), wg_ref[...], preferred_element_type=F32) + bg_ref[...]
    gate_all = (jnp.minimum(z, 0.0) - jnp.log(1.0 + jnp.exp(-jnp.abs(z)))) * (1.0 / GLA_TAU)
    gate_ref[...] = gate_all
    total = jnp.sum(gate_all.reshape(ngroups, g, QK_PAD), axis=1)
    single_level = jnp.min(total) > -GLA_SINGLE_LEVEL_DECAY

    k_head = lax.broadcasted_iota(jnp.int32, (g, kc), 1) // GLA_DK
    kmask = [k_head == h for h in range(HEADS_A)]
    va_head = lax.broadcasted_iota(jnp.int32, (g, vc), 1) // GLA_DV
    vb_head = lax.broadcasted_iota(jnp.int32, (g, GLA_V - vc), 1) // GLA_DV
    st_row = lax.broadcasted_iota(jnp.int32, (kc, GLA_V), 0) // GLA_DK
    st_col = lax.broadcasted_iota(jnp.int32, (kc, GLA_V), 1)
    state_mask = st_col // GLA_DV == st_row + jnp.where(st_col >= vc, HEADS_A, 0)
    lmask = [lm_ref[l] > 0.5 for l in range(GLA_LEVELS)]
    causal = functools.reduce(jnp.logical_or, lmask)

    def stack_heads(t, masks, n):
        return jnp.concatenate([jnp.where(masks[h], t, 0.0) for h in range(n)], axis=0).astype(BF16)

    def scores(qs, ks):
        sa = lax.dot_general(qs[:, :kc].astype(BF16), stack_heads(ks[:, :kc], kmask, HEADS_A), NT_DIMS,
                             preferred_element_type=F32)
        sb = lax.dot_general(qs[:, kc:].astype(BF16), stack_heads(ks[:, kc:], kmask, GLA_HEADS - HEADS_A),
                             NT_DIMS, preferred_element_type=F32)
        return jnp.concatenate([sa, sb], axis=1)

    def score_stage(gi, robust):
        gidx = (ngroups - 1 - gi) if reverse else gi
        rows = pl.ds(gidx * g, g)
        q = q_ref[0, rows, :]
        k = k_ref[0, rows, :]
        gate = gate_ref[rows, :]
        ghi = gate.astype(BF16)
        glo = (gate - ghi.astype(F32)).astype(BF16)
        cmat = cm_ref[...] if robust else cm2_ref[0:g, :]
        cs = jnp.dot(cmat, ghi, preferred_element_type=F32) + jnp.dot(cmat, glo, preferred_element_type=F32)
        if robust:
            p16, p32, p64, p128, s16, s32, s64, s128 = [cs[n * g:(n + 1) * g] for n in range(8)]
            q0 = q * jnp.exp(p16)
            a = [scores(q0, k * jnp.exp(-p16)),
                 scores(q0, k * jnp.exp(s16)),
                 scores(q * jnp.exp(p32), k * jnp.exp(s32)),
                 scores(q * jnp.exp(p64), k * jnp.exp(s64))]
            q4 = q * jnp.exp(p128)
            total = p128[0:1] + s128[0:1]
        else:
            p128 = cs
            total = jnp.sum(gate, axis=0, keepdims=True)
            s128 = total - p128
            q4 = q * jnp.exp(p128)
            a = scores(q4, k * jnp.exp(-p128))
        k4t = (k * jnp.exp(s128)).T.astype(BF16)
        return rows, a, q4.astype(BF16), k4t, total

    def value_stage(carry, robust):
        rows, a, q4b, k4t, total = carry
        v = v_ref[0, rows, :]
        if robust:
            blocks = []
            for h in range(GLA_HEADS):
                sl = slice(h * g, (h + 1) * g)
                blk = jnp.where(lmask[3], a[3][:, sl], 0.0)
                for l in (2, 1, 0):
                    blk = jnp.where(lmask[l], a[l][:, sl], blk)
                blocks.append(blk)
        else:
            blocks = [jnp.where(causal, a[:, h * g:(h + 1) * g], 0.0) for h in range(GLA_HEADS)]
        amat_a = jnp.concatenate(blocks[:HEADS_A], axis=1).astype(BF16)
        amat_b = jnp.concatenate(blocks[HEADS_A:], axis=1).astype(BF16)
        v_a = stack_heads(v[:, :vc], [va_head == h for h in range(HEADS_A)], HEADS_A)
        v_b = stack_heads(v[:, vc:], [vb_head == h for h in range(GLA_HEADS - HEADS_A)], GLA_HEADS - HEADS_A)
        st = st_ref[...]
        stb = st.astype(BF16)
        o_a = (jnp.dot(amat_a, v_a, preferred_element_type=F32)
               + jnp.dot(q4b[:, :kc], stb[:, :vc], preferred_element_type=F32))
        o_b = (jnp.dot(amat_b, v_b, preferred_element_type=F32)
               + jnp.dot(q4b[:, kc:], stb[:, vc:], preferred_element_type=F32))
        o = jnp.concatenate([o_a, o_b], axis=1)
        vb = v.astype(BF16)
        upd = jnp.concatenate([jnp.dot(k4t[:kc], vb[:, :vc], preferred_element_type=F32),
                               jnp.dot(k4t[kc:], vb[:, vc:], preferred_element_type=F32)], axis=1)
        dcol = jnp.exp(jnp.broadcast_to(total, (g, 2 * kc)).T[:, 0:1])
        dec = jnp.concatenate([jnp.broadcast_to(dcol[:kc], (kc, vc)),
                               jnp.broadcast_to(dcol[kc:], (kc, GLA_V - vc))], axis=1)
        st_ref[...] = st * dec + jnp.where(state_mask, upd, 0.0)
        if finalize:
            ot = o + of_ref[0, rows, :]
            o2 = ot * ot
            hi = o2.astype(BF16)
            lo = (o2 - hi.astype(F32)).astype(BF16)
            ms = (jnp.dot(hi, bd_ref[...], preferred_element_type=F32)
                  + jnp.dot(lo, bd_ref[...], preferred_element_type=F32))
            y = ot * lax.rsqrt(ms + EPS) * ng_ref[...] * _silu(r_ref[0, rows, :])
            out_ref[0, rows, :] = y.astype(out_ref.dtype)
        else:
            out_ref[0, rows, :] = o

    def run(robust):
        pending = None
        for gi in range(ngroups + 1):
            upcoming = score_stage(gi, robust) if gi < ngroups else None
            if pending is not None:
                value_stage(pending, robust)
            pending = upcoming

    @pl.when(single_level)
    def _():
        run(False)

    @pl.when(jnp.logical_not(single_level))
    def _():
        run(True)

    @pl.when(step == pl.num_programs(1) - 1)
    def _():
        sfin_ref[0] = st_ref[...]


def _gla_call(q, k, v, glr, wg, bg, tabs, s0, ts, reverse, fin=None):
    b, l, _ = q.shape
    nb = l // ts
    finalize = fin is not None
    cm, cm2, lm = tabs
    blk = (lambda bb, i: (bb, nb - 1 - i, 0)) if reverse else (lambda bb, i: (bb, i, 0))
    tok = lambda w: pl.BlockSpec((1, ts, w), blk)
    full = lambda a: pl.BlockSpec(a.shape, lambda bb, i: (0,) * a.ndim)
    state = pl.BlockSpec((1, QK_PAD // 2, GLA_V), lambda bb, i: (bb, 0, 0))
    in_specs = [tok(QK_PAD), tok(QK_PAD), tok(GLA_V), tok(GLR_PAD), full(wg), full(bg), full(cm), full(cm2),
                full(lm), state]
    args = [q, k, v, glr, wg, bg, cm, cm2, lm, s0]
    if finalize:
        r, o_other, ng, bd = fin
        in_specs += [tok(GLA_V), tok(GLA_V), full(ng), full(bd)]
        args += [r, o_other, ng, bd]
    out_dt = BF16 if finalize else F32
    return pl.pallas_call(
        functools.partial(_gla_kernel, reverse=reverse, finalize=finalize, ngroups=ts // GLA_GROUP),
        grid=(b, nb),
        in_specs=in_specs,
        out_specs=[tok(GLA_V), state],
        out_shape=[jax.ShapeDtypeStruct((b, l, GLA_V), out_dt),
                   jax.ShapeDtypeStruct((b, QK_PAD // 2, GLA_V), F32)],
        scratch_shapes=[pltpu.VMEM((QK_PAD // 2, GLA_V), F32), pltpu.VMEM((ts, QK_PAD), F32)],
        compiler_params=_cparams("arbitrary", "arbitrary"),
        name="gla_bwd" if reverse else "gla_fwd",
    )(*args)


def _na_bias_tables(rpb, rows):
    kh = min(NA_WIN_ROWS, rows)
    n_dr, n_dc = 2 * NA_WIN_ROWS - 1, 2 * NA_WIN_COLS - 1
    qr, kr = np.arange(NA_Q_ROWS)[:, None], np.arange(NA_K_ROWS)[None, :]
    row_pick, row_ok = [], []
    for r0, ks in ((0, 0), (NA_WIN_ROWS // 2, 0), (rows - NA_Q_ROWS, rows - NA_K_ROWS)):
        r, r2 = r0 + qr, ks + kr
        rs = np.clip(r - kh // 2, 0, rows - kh)
        row_ok.append((r2 >= rs) & (r2 < rs + kh))
        row_pick.append(np.eye(n_dr, dtype=np.float32)[np.clip(r2 - r + NA_WIN_ROWS - 1, 0, n_dr - 1)])
    row_pick, row_ok = np.stack(row_pick), np.stack(row_ok)
    qc, kc = np.arange(GRID_W)[:, None], np.arange(GRID_W)[None, :]
    cs = np.clip(qc - NA_WIN_COLS // 2, 0, GRID_W - NA_WIN_COLS)
    col_ok = (kc >= cs) & (kc < cs + NA_WIN_COLS)
    col_pick = np.eye(n_dc, dtype=np.float32)[np.clip(kc - qc + NA_WIN_COLS - 1, 0, n_dc - 1)]
    hi = lax.Precision.HIGHEST
    depth = rpb.shape[0]
    pairs = NA_Q_ROWS // 2
    by_row = jnp.einsum('lhde,ckqd->lchkqe', rpb.astype(F32), np.transpose(row_pick, (0, 2, 1, 3)), precision=hi)
    by_pair = by_row.reshape(depth, 3, NA_HEADS, NA_K_ROWS, pairs, 2 * n_dc)
    pick2 = np.zeros((2, n_dc, GRID_W, 2, GRID_W), np.float32)
    for q2 in range(2):
        pick2[q2, :, :, q2, :] = np.transpose(col_pick, (2, 1, 0))
    pick2 = pick2.reshape(2 * n_dc, GRID_W, 2 * GRID_W)
    bias = jnp.einsum('lchkjf,fym->lchkjym', by_pair, pick2, precision=hi)
    valid = (np.transpose(row_ok, (0, 2, 1)).reshape(3, NA_K_ROWS, pairs, 1, 2, 1)
             & col_ok.T[None, None, None, :, None, :])
    valid = valid.reshape(3, 1, NA_K_ROWS, pairs, GRID_W, 2 * GRID_W)
    return jnp.where(valid[None], bias, NEG_INF)


def _attend_heads(q, parts_of_pair, n_heads):
    lane = lax.broadcasted_iota(jnp.int32, (1, LANES), 1)
    half = [(lane < NA_DH).astype(BF16), (lane >= NA_DH).astype(BF16)]

    def scores(h):
        j, o = divmod(h, 2)
        qm = q[:, j * LANES:(j + 1) * LANES] * half[o]
        out = []
        for kp, _, bias in parts_of_pair(j):
            st = lax.dot_general(kp, qm, NT_DIMS, preferred_element_type=F32)
            if bias is not None:
                n_kr, n_jq = st.shape[0] // GRID_W, st.shape[1] // LANES
                st = jnp.concatenate([jnp.concatenate(
                    [st[kr * GRID_W:(kr + 1) * GRID_W, jq * LANES:(jq + 1) * LANES] + bias(o, kr, jq)
                     for jq in range(n_jq)], axis=1) for kr in range(n_kr)], axis=0)
            out.append(st)
        return out

    def softmax(s):
        m = functools.reduce(jnp.maximum, [jnp.max(t, axis=0, keepdims=True) for t in s])
        p = [jnp.exp(t - m) for t in s]
        denom = functools.reduce(jnp.add, [jnp.sum(t, axis=0, keepdims=True) for t in p])
        return [t.astype(BF16) for t in p], denom

    def values(h, p, denom):
        j, o = divmod(h, 2)
        acc = functools.reduce(jnp.add, [
            jnp.dot(vt[o * NA_DH:(o + 1) * NA_DH, :], pt, preferred_element_type=F32)
            for pt, (_, vt, _) in zip(p, parts_of_pair(j))])
        return acc / denom

    s, p, outs = {}, {}, {}
    for t in range(n_heads + 3):
        if t < n_heads:
            s[t] = scores(t)
        if 0 <= t - 1 < n_heads:
            p[t - 1] = softmax(s.pop(t - 1))
        if 0 <= t - 3 < n_heads:
            outs[t - 3] = values(t - 3, *p.pop(t - 3))
    return jnp.concatenate([outs[h] for h in range(n_heads)], axis=0).T


def _mix_residual(gla, na, u, u_prev, u_next, x, g1, wg, ws, wn, cw, cb, first, last):
    w = SC_WIDTH
    tm = u.shape[0]
    cx = u[:, w:2 * w] * u[:, 2 * w:3 * w]
    cx_prev = (u_prev[SUBLANES - 1:SUBLANES, w:2 * w] * u_prev[SUBLANES - 1:SUBLANES, 2 * w:3 * w]
               * jnp.where(first, 0.0, 1.0))
    cx_next = u_next[0:1, w:2 * w] * u_next[0:1, 2 * w:3 * w] * jnp.where(last, 0.0, 1.0)
    row = lax.broadcasted_iota(jnp.int32, cx.shape, 0)
    prev = jnp.where(row == 0, cx_prev, pltpu.roll(cx, 1, 0))
    nxt = jnp.where(row == tm - 1, cx_next, pltpu.roll(cx, tm - 1, 0))
    sc = u[:, 0:w] * (prev * cw[0:1] + cx * cw[1:2] + nxt * cw[2:3] + cb)
    mix = (jnp.dot(gla, wg, preferred_element_type=F32)
           + jnp.dot(sc.astype(BF16), ws, preferred_element_type=F32)
           + jnp.dot(na, wn, preferred_element_type=F32))
    return x + g1 * mix


def _na_kernel(q_ref, k_ref, vt_ref, kc_ref, vct_ref, bias_ref, gla_ref, scu_ref, scp_ref, scn_ref, x_ref, g1_ref,
               wg_ref, ws_ref, wn_ref, cw_ref, cb_ref, o_ref, *, rows):
    i = pl.program_id(1)
    ks = jnp.clip(i * NA_Q_ROWS - NA_WIN_ROWS // 2, 0, rows - NA_K_ROWS)
    win = pl.ds(pl.multiple_of(ks * GRID_W, NA_Q_ROWS * GRID_W), NA_K_ROWS * GRID_W)
    kw = k_ref[0, win, :]
    vtw = vt_ref[0, :, win]
    kc = kc_ref[0]
    vct = vct_ref[0]

    def parts(j):
        sl = slice(j * LANES, (j + 1) * LANES)
        return [(kw[:, sl], vtw[sl, :], lambda o, kr, jq: bias_ref[0, 2 * j + o, kr, jq]),
                (kc[:, sl], vct[sl, :], None)]

    na = _attend_heads(q_ref[0], parts, NA_HEADS).astype(BF16)
    o_ref[0] = _mix_residual(gla_ref[0], na, scu_ref[0], scp_ref[0], scn_ref[0], x_ref[0], g1_ref[0],
                             wg_ref[...], ws_ref[...], wn_ref[...], cw_ref[...], cb_ref[...],
                             i == 0, i == pl.num_programs(1) - 1)


def _na_call(q, k, vt, kc, vct, bias, gla, scu, x, g1, wg, ws, wn, cw, cb):
    b, l, w = q.shape
    d = x.shape[-1]
    c = kc.shape[1]
    rows = l // GRID_W
    nrb = rows // NA_Q_ROWS
    nq = NA_Q_ROWS * GRID_W
    nblk8, t8 = l // SUBLANES, nq // SUBLANES
    seq = lambda n: pl.BlockSpec((1, n, w), lambda bb, i: (bb, 0, 0))
    seq_t = lambda n: pl.BlockSpec((1, w, n), lambda bb, i: (bb, 0, 0))
    tok = lambda width: pl.BlockSpec((1, nq, width), lambda bb, i: (bb, i, 0))
    full = lambda a: pl.BlockSpec(a.shape, lambda bb, i: (0,) * a.ndim)
    case = lambda bb, i: (jnp.where(i == 0, 0, jnp.where(i == nrb - 1, 2, 1)),) + (0,) * (bias.ndim - 1)
    halo_prev = pl.BlockSpec((1, SUBLANES, 3 * SC_WIDTH), lambda bb, i: (bb, jnp.maximum(i * t8 - 1, 0), 0))
    halo_next = pl.BlockSpec((1, SUBLANES, 3 * SC_WIDTH), lambda bb, i: (bb, jnp.minimum((i + 1) * t8, nblk8 - 1), 0))
    return pl.pallas_call(
        functools.partial(_na_kernel, rows=rows),
        grid=(b, nrb),
        in_specs=[tok(w), seq(l), seq_t(l), seq(c), seq_t(c), pl.BlockSpec((1,) + bias.shape[1:], case),
                  tok(GLA_V), tok(3 * SC_WIDTH), halo_prev, halo_next, tok(d),
                  pl.BlockSpec((1, 1, d), lambda bb, i: (bb, 0, 0)),
                  full(wg), full(ws), full(wn), full(cw), full(cb)],
        out_specs=tok(d),
        out_shape=jax.ShapeDtypeStruct((b, l, d), F32),
        compiler_params=_cparams("arbitrary", "arbitrary"),
        name="na_out_proj",
    )(q, k, vt, kc, vct, bias, gla, scu, scu, scu, x, g1, wg, ws, wn, cw, cb)


def _ctx_attn_kernel(q_ref, k_ref, vt_ref, o_ref):
    k, vt = k_ref[0], vt_ref[0]

    def parts(j):
        sl = slice(j * LANES, (j + 1) * LANES)
        return [(k[:, sl], vt[sl, :], None)]

    o_ref[0] = _attend_heads(q_ref[0], parts, NA_HEADS).astype(o_ref.dtype)


def _ctx_attn_call(q, k, vt):
    b, c, w = q.shape
    spec = pl.BlockSpec((1, c, w), lambda bb: (bb, 0, 0))
    spec_t = pl.BlockSpec((1, w, c), lambda bb: (bb, 0, 0))
    return pl.pallas_call(
        _ctx_attn_kernel, grid=(b,), in_specs=[spec, spec, spec_t], out_specs=spec,
        out_shape=jax.ShapeDtypeStruct((b, c, w), BF16),
        compiler_params=_cparams("arbitrary"), name="ctx_attn",
    )(q, k, vt)


def _outproj_kernel(gla_ref, na_ref, scu_ref, scp_ref, scn_ref, x_ref, g1_ref,
                    wg_ref, ws_ref, wn_ref, cw_ref, cb_ref, o_ref):
    i = pl.program_id(1)
    o_ref[0] = _mix_residual(gla_ref[0], na_ref[0], scu_ref[0], scp_ref[0], scn_ref[0], x_ref[0], g1_ref[0],
                             wg_ref[...], ws_ref[...], wn_ref[...], cw_ref[...], cb_ref[...],
                             i == 0, i == pl.num_programs(1) - 1)


def _outproj_call(gla, na, scu, x, g1, wg, ws, wn, cw, cb, tm):
    b, l, d = x.shape
    bm = g1.shape[0]
    nblk8 = l // SUBLANES
    t8 = tm // SUBLANES
    mod_map = (lambda bb, i: (bb, 0, 0)) if bm > 1 else (lambda bb, i: (0, 0, 0))
    tok = lambda w: pl.BlockSpec((1, tm, w), lambda bb, i: (bb, i, 0))
    full = lambda a: pl.BlockSpec(a.shape, lambda bb, i: (0,) * a.ndim)
    halo_prev = pl.BlockSpec((1, SUBLANES, 3 * SC_WIDTH), lambda bb, i: (bb, jnp.maximum(i * t8 - 1, 0), 0))
    halo_next = pl.BlockSpec((1, SUBLANES, 3 * SC_WIDTH), lambda bb, i: (bb, jnp.minimum((i + 1) * t8, nblk8 - 1), 0))
    return pl.pallas_call(
        _outproj_kernel,
        grid=(b, l // tm),
        in_specs=[tok(GLA_V), tok(NA_W), tok(3 * SC_WIDTH), halo_prev, halo_next, tok(d),
                  pl.BlockSpec((1, 1, d), mod_map), full(wg), full(ws), full(wn), full(cw), full(cb)],
        out_specs=tok(d),
        out_shape=jax.ShapeDtypeStruct((b, l, d), F32),
        compiler_params=_cparams("arbitrary", "arbitrary"),
        name="out_proj",
    )(gla, na, scu, scu, scu, x, g1, wg, ws, wn, cw, cb)


def _ffn_kernel(*refs, final, chunk):
    if final:
        (x_ref, xp_ref, xn_ref, sh_ref, sc_ref, g2_ref, ng_ref, wup_ref, cw_ref, cb_ref, wdn_ref, fg_ref,
         o_ref, act_ref) = refs
    else:
        (x_ref, xp_ref, xn_ref, sh_ref, sc_ref, g2_ref, ng_ref, wup_ref, cw_ref, cb_ref, wdn_ref,
         o_ref, act_ref) = refs
    i = pl.program_id(1)
    last = pl.num_programs(1) - 1
    x = x_ref[0]
    tm = x.shape[0]
    hid = wdn_ref.shape[0]
    nm = lambda t: _norm_mod(t, ng_ref[...], sh_ref[0], sc_ref[0])
    hp = nm(xp_ref[0]) * jnp.where(i > 0, 1.0, 0.0)
    hn = nm(xn_ref[0]) * jnp.where(i < last, 1.0, 0.0)
    hb = jnp.concatenate([hp, nm(x), hn], axis=0).astype(BF16)
    ext = tm + 2 * SUBLANES
    inner = slice(SUBLANES, SUBLANES + tm)

    def conv(u, c0, w):
        cw = cw_ref[:, c0:c0 + w]
        return (pltpu.roll(u, 1, 0)[inner] * cw[0:1] + u[inner] * cw[1:2]
                + pltpu.roll(u, ext - 1, 0)[inner] * cw[2:3] + cb_ref[:, c0:c0 + w])

    for c0 in range(0, hid, chunk):
        w = min(chunk, hid - c0)
        ua = jnp.dot(hb, wup_ref[:, c0:c0 + w], preferred_element_type=F32)
        ub = jnp.dot(hb, wup_ref[:, hid + c0:hid + c0 + w], preferred_element_type=F32)
        act_ref[:, c0:c0 + w] = (_silu(conv(ua, c0, w)) * conv(ub, hid + c0, w)).astype(BF16)
    y = x + g2_ref[0] * jnp.dot(act_ref[...], wdn_ref[...], preferred_element_type=F32)
    if final:
        ms = jnp.mean(y * y, axis=-1, keepdims=True)
        y = y * lax.rsqrt(ms + EPS) * fg_ref[...]
    o_ref[0] = y


def _ffn_call(x, shift, scale, g2, gain, wup, cw, cb, wdn, final_gain, tm):
    b, l, d = x.shape
    bm = shift.shape[0]
    nblk8 = l // SUBLANES
    t8 = tm // SUBLANES
    final = final_gain is not None
    mod_map = (lambda bb, i: (bb, 0, 0)) if bm > 1 else (lambda bb, i: (0, 0, 0))
    tok = pl.BlockSpec((1, tm, d), lambda bb, i: (bb, i, 0))
    full = lambda a: pl.BlockSpec(a.shape, lambda bb, i: (0,) * a.ndim)
    resident = lambda a: pl.BlockSpec(a.shape, lambda bb, i: (0,) * a.ndim, pipeline_mode=pl.Buffered(1))
    halo_prev = pl.BlockSpec((1, SUBLANES, d), lambda bb, i: (bb, jnp.maximum(i * t8 - 1, 0), 0))
    halo_next = pl.BlockSpec((1, SUBLANES, d), lambda bb, i: (bb, jnp.minimum((i + 1) * t8, nblk8 - 1), 0))
    mod = pl.BlockSpec((1, 1, d), mod_map)
    in_specs = [tok, halo_prev, halo_next, mod, mod, mod, full(gain), resident(wup), full(cw), full(cb), resident(wdn)]
    args = [x, x, x, shift, scale, g2, gain, wup, cw, cb, wdn]
    if final:
        in_specs.append(full(final_gain))
        args.append(final_gain)
    return pl.pallas_call(
        functools.partial(_ffn_kernel, final=final, chunk=FFN_CHUNK),
        grid=(b, l // tm),
        in_specs=in_specs,
        out_specs=tok,
        out_shape=jax.ShapeDtypeStruct((b, l, d), F32),
        scratch_shapes=[pltpu.VMEM((tm, wdn.shape[0]), BF16)],
        compiler_params=_cparams("arbitrary", "arbitrary"),
        name="conv_ffn_final" if final else "conv_ffn",
    )(*args)


def _rope_tables(n_tokens):
    t = jnp.arange(n_tokens)
    n_freq = GLA_DK // 4
    inv_freq = ROPE_BASE ** (-jnp.arange(n_freq, dtype=F32) / n_freq)
    row = (t // GRID_W).astype(F32)[:, None] * inv_freq
    col = (t % GRID_W).astype(F32)[:, None] * inv_freq
    ang = jnp.concatenate([row, col], axis=-1)
    cos = jnp.repeat(jnp.cos(ang), 2, axis=-1)
    sin = jnp.repeat(jnp.sin(ang), 2, axis=-1) * jnp.tile(jnp.asarray([-1.0, 1.0], F32), GLA_DK // 2)
    pad = QK_PAD - GLA_QK
    cos = jnp.concatenate([jnp.tile(cos, (1, GLA_HEADS)), jnp.ones((n_tokens, pad), F32)], axis=-1)
    sin = jnp.concatenate([jnp.tile(sin, (1, GLA_HEADS)), jnp.zeros((n_tokens, pad), F32)], axis=-1)
    return cos, sin


def _tile_rows(n, target):
    t = min(n, target)
    assert n % t == 0 and t % GLA_GROUP == 0, (n, t)
    return t


def kernel(x, c, ctx, c_ctx, w_ada, b_ada, norm_mix_g, norm_ffn_g, w_in, gla_wg2_fw, gla_bg_fw, gla_wg2_bw, gla_bg_bw, gla_norm_g, sc_conv_w, sc_conv_b, na_rpb, w_out, ffn_w_up, ffn_conv_w, ffn_conv_b, ffn_w_down, final_norm_g):
    bsz, n_lat, d = x.shape
    n_ctx = ctx.shape[1]
    depth = w_in.shape[0]
    rows = n_lat // GRID_W
    assert n_lat % (NA_Q_ROWS * GRID_W) == 0 and rows >= NA_K_ROWS and bsz + 1 <= 2 * SUBLANES
    tm_lat, tm_ctx = _tile_rows(n_lat, LATENT_TILE), _tile_rows(n_ctx, LATENT_TILE)

    crows = jnp.zeros((2 * SUBLANES, d), F32).at[:bsz].set(c).at[bsz].set(c_ctx)
    mod_all = _ada_call(crows, w_ada, b_ada)

    zpad = lambda n: jnp.zeros((depth, d, n), BF16)
    o = np.cumsum([0, GLA_QK, GLA_QK, GLA_V, GLA_LOWRANK, GLA_LOWRANK, GLA_V,
                   SC_WIDTH, SC_WIDTH, SC_WIDTH, NA_W, NA_W, NA_W])
    w_in_b = w_in.astype(BF16)
    col = lambda n: w_in_b[:, :, o[n]:o[n + 1]]
    qpad = zpad(QK_PAD - GLA_QK)
    w_all = jnp.concatenate([col(0), qpad, col(1), qpad, col(2), col(5), col(3), col(4),
                             zpad(GLR_PAD - 2 * GLA_LOWRANK), w_in_b[:, :, o[6]:o[9]],
                             col(9) * jnp.asarray(NA_DH ** -0.5, BF16), w_in_b[:, :, o[10]:o[12]]], axis=-1)
    assert w_all.shape[-1] == INPROJ_WIDTH
    w_out_b = w_out.astype(BF16)
    wo_gla, wo_sc, wo_na = w_out_b[:, :GLA_V], w_out_b[:, GLA_V:GLA_V + SC_WIDTH], w_out_b[:, GLA_V + SC_WIDTH:]
    wup = ffn_w_up.astype(BF16)
    wdn = ffn_w_down.astype(BF16)

    def gate_w(w2, first_row):
        full = jnp.zeros((depth, GLR_PAD, QK_PAD), F32)
        return full.at[:, first_row:first_row + GLA_LOWRANK, :GLA_QK].set(w2).astype(BF16)

    def gate_b(bias):
        return jnp.zeros((depth, 1, QK_PAD), F32).at[:, 0, :GLA_QK].set(bias)

    wg_f, wg_b = gate_w(gla_wg2_fw, 0), gate_w(gla_wg2_bw, GLA_LOWRANK)
    bg_f, bg_b = gate_b(gla_bg_fw), gate_b(gla_bg_bw)
    gla_gain = jnp.tile(gla_norm_g, (1, GLA_HEADS))[:, None, :]
    hv = np.arange(GLA_V) // GLA_DV
    head_mean = jnp.asarray((hv[:, None] == hv[None, :]) / GLA_DV, BF16)
    tabs_f, tabs_b = _gla_tables(False), _gla_tables(True)
    rope_tabs = _rope_tables(n_lat)
    zero_state = jnp.zeros((bsz, QK_PAD // 2, GLA_V), F32)
    na_bias = _na_bias_tables(na_rpb, rows)

    xc = ctx
    for layer in range(depth):
        update_ctx = layer < depth - 1
        mod = mod_all[layer]
        sh1, sc1, g1, sh2, sc2, g2 = [mod[:bsz, n * d:(n + 1) * d][:, None, :] for n in range(6)]
        sh1c, sc1c, g1c, sh2c, sc2c, g2c = [mod[bsz:bsz + 1, n * d:(n + 1) * d][:, None, :] for n in range(6)]
        gain_mix = norm_mix_g[layer][None, :]
        gain_ffn = norm_ffn_g[layer][None, :]

        cq, ck, cv, cr, cglr, cscu, cnaq, cnak, cnavt = _inproj_call(
            xc, sh1c, sc1c, gain_mix, w_all[layer], None, tm_ctx)
        lq, lk, lv, lr, lglr, lscu, lnaq, lnak, lnavt = _inproj_call(
            x, sh1, sc1, gain_mix, w_all[layer], rope_tabs, tm_lat)

        fin = lambda r, o_f: (r, o_f, gla_gain[layer], head_mean)
        oc_f, state_f = _gla_call(cq, ck, cv, cglr, wg_f[layer], bg_f[layer], tabs_f, zero_state, tm_ctx, False)
        gla_ctx, state_b = _gla_call(cq, ck, cv, cglr, wg_b[layer], bg_b[layer], tabs_b, zero_state, tm_ctx, True,
                                     fin(cr, oc_f))
        ol_f, _ = _gla_call(lq, lk, lv, lglr, wg_f[layer], bg_f[layer], tabs_f, state_f, tm_lat, False)
        gla_lat, _ = _gla_call(lq, lk, lv, lglr, wg_b[layer], bg_b[layer], tabs_b, state_b, tm_lat, True,
                               fin(lr, ol_f))

        cw_sc, cb_sc = sc_conv_w[layer], sc_conv_b[layer][None, :]
        wo = (wo_gla[layer], wo_sc[layer], wo_na[layer])
        x = _na_call(lnaq, lnak, lnavt, cnak, cnavt, na_bias[layer], gla_lat, lscu, x, g1, *wo, cw_sc, cb_sc)
        ffn = (wup[layer], ffn_conv_w[layer], ffn_conv_b[layer][None, :], wdn[layer])
        x = _ffn_call(x, sh2, sc2, g2, gain_ffn, *ffn,
                      final_norm_g[None, :] if layer == depth - 1 else None, tm_lat)

        if update_ctx:
            na_ctx = _ctx_attn_call(cnaq, cnak, cnavt)
            xc = _outproj_call(gla_ctx, na_ctx, cscu, xc, g1c, *wo, cw_sc, cb_sc, tm_ctx)
            xc = _ffn_call(xc, sh2c, sc2c, g2c, gain_ffn, *ffn, None, tm_ctx)
    return x
```

```python
import functools

import numpy as np
import jax
import jax.numpy as jnp
from jax import lax
from jax.experimental import pallas as pl
from jax.experimental.pallas import tpu as pltpu

F32 = jnp.float32
BF16 = jnp.bfloat16

GRID_W = 64
EPS = 1e-6
NEG_INF = -1e30
GLA_HEADS = 6
GLA_DK = 32
GLA_DV = 64
GLA_LOWRANK = 16
GLA_TAU = 16.0
ROPE_BASE = 10000.0
SC_WIDTH = 256
NA_HEADS = 6
NA_DH = 64
NA_WIN_ROWS = 8
NA_WIN_COLS = 16
GLA_QK = GLA_HEADS * GLA_DK
GLA_V = GLA_HEADS * GLA_DV
NA_W = NA_HEADS * NA_DH

LANES = 128
SUBLANES = 8
VMEM_LIMIT_BYTES = 56 * 1024 * 1024

QK_PAD = 2 * LANES
GLR_PAD = LANES
GLA_GROUP = 128
GLA_BASE = 16
GLA_LEVELS = 4
FFN_CHUNK = 6 * LANES
LATENT_TILE = 1024
GLA_TILE = 2048
HEADS_A = LANES // GLA_DK
GLA_SINGLE_LEVEL_DECAY = 64.0
ADA_COLS = 12 * LANES
NA_Q_ROWS = 4
NA_K_ROWS = 12

NT_DIMS = (((1,), (1,)), ((), ()))


def _cparams(*sem):
    return pltpu.CompilerParams(dimension_semantics=sem, vmem_limit_bytes=VMEM_LIMIT_BYTES)


def _silu(t):
    return t / (1.0 + jnp.exp(-t))


def _norm_mod(t, gain, shift, scale):
    ms = jnp.mean(t * t, axis=-1, keepdims=True)
    return (t * lax.rsqrt(ms + EPS) * gain) * (1.0 + scale) + shift


def _ada_kernel(c_ref, w_ref, b_ref, o_ref):
    c = c_ref[...]
    o_ref[0] = jnp.dot(_silu(c), w_ref[0], preferred_element_type=F32,
                       precision=lax.Precision.HIGHEST) + b_ref[0]


def _ada_call(crows, w_ada, b_ada):
    depth, d, n = w_ada.shape
    rows = crows.shape[0]
    tn = ADA_COLS
    assert n % tn == 0
    return pl.pallas_call(
        _ada_kernel,
        grid=(depth, n // tn),
        in_specs=[pl.BlockSpec((rows, d), lambda l, j: (0, 0)),
                  pl.BlockSpec((1, d, tn), lambda l, j: (l, 0, j)),
                  pl.BlockSpec((1, 1, tn), lambda l, j: (l, 0, j))],
        out_specs=pl.BlockSpec((1, rows, tn), lambda l, j: (l, 0, j)),
        out_shape=jax.ShapeDtypeStruct((depth, rows, n), F32),
        compiler_params=_cparams("arbitrary", "arbitrary"),
        name="ada_mod",
    )(crows, w_ada, b_ada.reshape(depth, 1, n))


INPROJ_OUTS = ((QK_PAD, F32), (QK_PAD, F32), (GLA_V, F32), (GLA_V, F32), (GLR_PAD, F32),
               (3 * SC_WIDTH, F32), (NA_W, BF16), (NA_W, BF16), (NA_W, BF16))
INPROJ_STARTS = tuple(int(s) for s in np.cumsum([0] + [w for w, _ in INPROJ_OUTS]))
INPROJ_WIDTH = INPROJ_STARTS[-1]
MXU_COLS = 2 * LANES
NAV_OUT = len(INPROJ_OUTS) - 1


def _rotary(t, c, s):
    lane = lax.broadcasted_iota(jnp.int32, t.shape, 1)
    swapped = jnp.where(lane % 2 == 0, pltpu.roll(t, LANES - 1, 1), pltpu.roll(t, 1, 1))
    return t * c + swapped * s


def _inproj_kernel(*refs, rope):
    if rope:
        x_ref, sh_ref, sc_ref, g_ref, cos_ref, sin_ref, w_ref = refs[:7]
        outs = refs[7:]
    else:
        x_ref, sh_ref, sc_ref, g_ref, w_ref = refs[:5]
        outs = refs[5:]
    navt_ref = outs[-1]
    nav_pieces = []
    hb = _norm_mod(x_ref[0], g_ref[...], sh_ref[0], sc_ref[0]).astype(BF16)
    for n0 in range(0, INPROJ_WIDTH, MXU_COLS):
        res = jnp.dot(hb, w_ref[:, n0:n0 + MXU_COLS], preferred_element_type=F32)
        for a in range(n0, n0 + MXU_COLS, LANES):
            idx = max(i for i, s0 in enumerate(INPROJ_STARTS[:-1]) if s0 <= a)
            off = a - INPROJ_STARTS[idx]
            piece = res[:, a - n0:a - n0 + LANES]
            if idx < 2:
                if rope:
                    piece = _rotary(piece, cos_ref[:, off:off + LANES], sin_ref[:, off:off + LANES])
                if idx == 0:
                    piece = piece * (GLA_DK ** -0.5)
            if idx == NAV_OUT:
                nav_pieces.append(piece)
            else:
                outs[idx][0, :, off:off + LANES] = piece.astype(outs[idx].dtype)
    navt_ref[0] = jnp.concatenate(nav_pieces, axis=1).T.astype(BF16)


def _inproj_call(x, shift, scale, gain, w_all, rope_tabs, tm):
    b, l, d = x.shape
    bm = shift.shape[0]
    rope = rope_tabs is not None
    mod_map = (lambda i, bb: (bb, 0, 0)) if bm > 1 else (lambda i, bb: (0, 0, 0))
    tok = lambda w: pl.BlockSpec((1, tm, w), lambda i, bb: (bb, i, 0))
    full = lambda a: pl.BlockSpec(a.shape, lambda i, bb: (0,) * a.ndim)
    in_specs = [tok(d), pl.BlockSpec((1, 1, d), mod_map), pl.BlockSpec((1, 1, d), mod_map), full(gain)]
    args = [x, shift, scale, gain]
    if rope:
        in_specs += [pl.BlockSpec((tm, QK_PAD), lambda i, bb: (i, 0))] * 2
        args += list(rope_tabs)
    in_specs.append(full(w_all))
    args.append(w_all)
    return pl.pallas_call(
        functools.partial(_inproj_kernel, rope=rope),
        grid=(l // tm, b),
        in_specs=in_specs,
        out_specs=([tok(w) for w, _ in INPROJ_OUTS[:NAV_OUT]]
                   + [pl.BlockSpec((1, NA_W, tm), lambda i, bb: (bb, 0, i))]),
        out_shape=([jax.ShapeDtypeStruct((b, l, w), dt) for w, dt in INPROJ_OUTS[:NAV_OUT]]
                   + [jax.ShapeDtypeStruct((b, NA_W, l), BF16)]),
        compiler_params=_cparams("arbitrary", "arbitrary"),
        name="in_proj_rope" if rope else "in_proj_ctx",
    )(*args)


def _gla_tables(reverse):
    t = np.arange(GLA_GROUP)
    i, j = t[:, None], t[None, :]
    before = (j >= i) if reverse else (j <= i)
    strictly_after = (j < i) if reverse else (j > i)
    sizes = [GLA_BASE << level for level in range(GLA_LEVELS)]
    assert sizes[-1] == GLA_GROUP
    maps = []
    for n in sizes:
        maps.append((i // n == j // n) & before)
    for n in sizes:
        maps.append((i // n == j // n) & strictly_after)
    masks = [(i // GLA_BASE == j // GLA_BASE) & before]
    for n in sizes[1:]:
        h = n // 2
        q_late, k_early = (i % n) >= h, (j % n) < h
        if reverse:
            q_late, k_early = (i % n) < h, (j % n) >= h
        masks.append((i // n == j // n) & q_late & k_early)
    cm = np.concatenate(maps, axis=0).astype(np.float32)
    cm2 = np.concatenate([maps[GLA_LEVELS - 1], maps[-1]], axis=0).astype(np.float32)
    lm = np.stack(masks, axis=0).astype(np.float32)
    return jnp.asarray(cm, BF16), jnp.asarray(cm2, BF16), jnp.asarray(lm, F32)


def _gla_kernel(*refs, reverse, finalize, ngroups):
    if finalize:
        (q_ref, k_ref, v_ref, glr_ref, wg_ref, bg_ref, cm_ref, cm2_ref, lm_ref, s0_ref,
         r_ref, of_ref, ng_ref, bd_ref, out_ref, sfin_ref, st_ref, gate_ref) = refs
    else:
        (q_ref, k_ref, v_ref, glr_ref, wg_ref, bg_ref, cm_ref, cm2_ref, lm_ref, s0_ref,
         out_ref, sfin_ref, st_ref, gate_ref) = refs
    step = pl.program_id(1)
    g = GLA_GROUP
    kc = QK_PAD // 2
    vc = HEADS_A * GLA_DV

    @pl.when(step == 0)
    def _():
        st_ref[...] = s0_ref[0]

    z = jnp.dot(glr_ref[0].astype(BF16), wg_ref[...], preferred_element_type=F32) + bg_ref[...]
    gate_all = (jnp.minimum(z, 0.0) - jnp.log(1.0 + jnp.exp(-jnp.abs(z)))) * (1.0 / GLA_TAU)
    gate_ref[...] = gate_all
    total = jnp.sum(gate_all.reshape(ngroups, g, QK_PAD), axis=1)
    single_level = jnp.min(total) > -GLA_SINGLE_LEVEL_DECAY

    k_head = lax.broadcasted_iota(jnp.int32, (g, kc), 1) // GLA_DK
    kmask = [k_head == h for h in range(HEADS_A)]
    va_head = lax.broadcasted_iota(jnp.int32, (g, vc), 1) // GLA_DV
    vb_head = lax.broadcasted_iota(jnp.int32, (g, GLA_V - vc), 1) // GLA_DV
    st_row = lax.broadcasted_iota(jnp.int32, (kc, GLA_V), 0) // GLA_DK
    st_col = lax.broadcasted_iota(jnp.int32, (kc, GLA_V), 1)
    state_mask = st_col // GLA_DV == st_row + jnp.where(st_col >= vc, HEADS_A, 0)
    lmask = [lm_ref[l] > 0.5 for l in range(GLA_LEVELS)]
    causal = functools.reduce(jnp.logical_or, lmask)

    def stack_heads(t, masks, n):
        return jnp.concatenate([jnp.where(masks[h], t, 0.0) for h in range(n)], axis=0).astype(BF16)

    def scores(qs, ks):
        sa = lax.dot_general(qs[:, :kc].astype(BF16), stack_heads(ks[:, :kc], kmask, HEADS_A), NT_DIMS,
                             preferred_element_type=F32)
        sb = lax.dot_general(qs[:, kc:].astype(BF16), stack_heads(ks[:, kc:], kmask, GLA_HEADS - HEADS_A),
                             NT_DIMS, preferred_element_type=F32)
        return jnp.concatenate([sa, sb], axis=1)

    def score_stage(gi, robust):
        gidx = (ngroups - 1 - gi) if reverse else gi
        rows = pl.ds(gidx * g, g)
        q = q_ref[0, rows, :]
        k = k_ref[0, rows, :]
        gate = gate_ref[rows, :]
        ghi = gate.astype(BF16)
        glo = (gate - ghi.astype(F32)).astype(BF16)
        cmat = cm_ref[...] if robust else cm2_ref[0:g, :]
        cs = jnp.dot(cmat, ghi, preferred_element_type=F32) + jnp.dot(cmat, glo, preferred_element_type=F32)
        if robust:
            p16, p32, p64, p128, s16, s32, s64, s128 = [cs[n * g:(n + 1) * g] for n in range(8)]
            q0 = q * jnp.exp(p16)
            a = [scores(q0, k * jnp.exp(-p16)),
                 scores(q0, k * jnp.exp(s16)),
                 scores(q * jnp.exp(p32), k * jnp.exp(s32)),
                 scores(q * jnp.exp(p64), k * jnp.exp(s64))]
            q4 = q * jnp.exp(p128)
            total = p128[0:1] + s128[0:1]
        else:
            p128 = cs
            total = jnp.sum(gate, axis=0, keepdims=True)
            s128 = total - p128
            q4 = q * jnp.exp(p128)
            a = scores(q4, k * jnp.exp(-p128))
        k4t = (k * jnp.exp(s128)).T.astype(BF16)
        return rows, a, q4.astype(BF16), k4t, total

    def value_stage(carry, robust):
        rows, a, q4b, k4t, total = carry
        v = v_ref[0, rows, :]
        if robust:
            blocks = []
            for h in range(GLA_HEADS):
                sl = slice(h * g, (h + 1) * g)
                blk = jnp.where(lmask[3], a[3][:, sl], 0.0)
                for l in (2, 1, 0):
                    blk = jnp.where(lmask[l], a[l][:, sl], blk)
                blocks.append(blk)
        else:
            blocks = [jnp.where(causal, a[:, h * g:(h + 1) * g], 0.0) for h in range(GLA_HEADS)]
        amat_a = jnp.concatenate(blocks[:HEADS_A], axis=1).astype(BF16)
        amat_b = jnp.concatenate(blocks[HEADS_A:], axis=1).astype(BF16)
        v_a = stack_heads(v[:, :vc], [va_head == h for h in range(HEADS_A)], HEADS_A)
        v_b = stack_heads(v[:, vc:], [vb_head == h for h in range(GLA_HEADS - HEADS_A)], GLA_HEADS - HEADS_A)
        st = st_ref[...]
        stb = st.astype(BF16)
        o_a = (jnp.dot(amat_a, v_a, preferred_element_type=F32)
               + jnp.dot(q4b[:, :kc], stb[:, :vc], preferred_element_type=F32))
        o_b = (jnp.dot(amat_b, v_b, preferred_element_type=F32)
               + jnp.dot(q4b[:, kc:], stb[:, vc:], preferred_element_type=F32))
        o = jnp.concatenate([o_a, o_b], axis=1)
        vb = v.astype(BF16)
        upd = jnp.concatenate([jnp.dot(k4t[:kc], vb[:, :vc], preferred_element_type=F32),
                               jnp.dot(k4t[kc:], vb[:, vc:], preferred_element_type=F32)], axis=1)
        dcol = jnp.exp(jnp.broadcast_to(total, (g, 2 * kc)).T[:, 0:1])
        dec = jnp.concatenate([jnp.broadcast_to(dcol[:kc], (kc, vc)),
                               jnp.broadcast_to(dcol[kc:], (kc, GLA_V - vc))], axis=1)
        st_ref[...] = st * dec + jnp.where(state_mask, upd, 0.0)
        if finalize:
            ot = o + of_ref[0, rows, :]
            o2 = ot * ot
            hi = o2.astype(BF16)
            lo = (o2 - hi.astype(F32)).astype(BF16)
            ms = (jnp.dot(hi, bd_ref[...], preferred_element_type=F32)
                  + jnp.dot(lo, bd_ref[...], preferred_element_type=F32))
            y = ot * lax.rsqrt(ms + EPS) * ng_ref[...] * _silu(r_ref[0, rows, :])
            out_ref[0, rows, :] = y.astype(out_ref.dtype)
        else:
            out_ref[0, rows, :] = o

    def run(robust):
        pending = None
        for gi in range(ngroups + 1):
            upcoming = score_stage(gi, robust) if gi < ngroups else None
            if pending is not None:
                value_stage(pending, robust)
            pending = upcoming

    @pl.when(single_level)
    def _():
        run(False)

    @pl.when(jnp.logical_not(single_level))
    def _():
        run(True)

    @pl.when(step == pl.num_programs(1) - 1)
    def _():
        sfin_ref[0] = st_ref[...]


def _gla_call(q, k, v, glr, wg, bg, tabs, s0, ts, reverse, fin=None):
    b, l, _ = q.shape
    nb = l // ts
    finalize = fin is not None
    cm, cm2, lm = tabs
    blk = (lambda bb, i: (bb, nb - 1 - i, 0)) if reverse else (lambda bb, i: (bb, i, 0))
    tok = lambda w: pl.BlockSpec((1, ts, w), blk)
    full = lambda a: pl.BlockSpec(a.shape, lambda bb, i: (0,) * a.ndim)
    state = pl.BlockSpec((1, QK_PAD // 2, GLA_V), lambda bb, i: (bb, 0, 0))
    in_specs = [tok(QK_PAD), tok(QK_PAD), tok(GLA_V), tok(GLR_PAD), full(wg), full(bg), full(cm), full(cm2),
                full(lm), state]
    args = [q, k, v, glr, wg, bg, cm, cm2, lm, s0]
    if finalize:
        r, o_other, ng, bd = fin
        in_specs += [tok(GLA_V), tok(GLA_V), full(ng), full(bd)]
        args += [r, o_other, ng, bd]
    out_dt = BF16 if finalize else F32
    return pl.pallas_call(
        functools.partial(_gla_kernel, reverse=reverse, finalize=finalize, ngroups=ts // GLA_GROUP),
        grid=(b, nb),
        in_specs=in_specs,
        out_specs=[tok(GLA_V), state],
        out_shape=[jax.ShapeDtypeStruct((b, l, GLA_V), out_dt),
                   jax.ShapeDtypeStruct((b, QK_PAD // 2, GLA_V), F32)],
        scratch_shapes=[pltpu.VMEM((QK_PAD // 2, GLA_V), F32), pltpu.VMEM((ts, QK_PAD), F32)],
        compiler_params=_cparams("arbitrary", "arbitrary"),
        name="gla_bwd" if reverse else "gla_fwd",
    )(*args)


def _na_bias_tables(rpb, rows):
    kh = min(NA_WIN_ROWS, rows)
    n_dr, n_dc = 2 * NA_WIN_ROWS - 1, 2 * NA_WIN_COLS - 1
    qr, kr = np.arange(NA_Q_ROWS)[:, None], np.arange(NA_K_ROWS)[None, :]
    row_pick, row_ok = [], []
    for r0, ks in ((0, 0), (NA_WIN_ROWS // 2, 0), (rows - NA_Q_ROWS, rows - NA_K_ROWS)):
        r, r2 = r0 + qr, ks + kr
        rs = np.clip(r - kh // 2, 0, rows - kh)
        row_ok.append((r2 >= rs) & (r2 < rs + kh))
        row_pick.append(np.eye(n_dr, dtype=np.float32)[np.clip(r2 - r + NA_WIN_ROWS - 1, 0, n_dr - 1)])
    row_pick, row_ok = np.stack(row_pick), np.stack(row_ok)
    qc, kc = np.arange(GRID_W)[:, None], np.arange(GRID_W)[None, :]
    cs = np.clip(qc - NA_WIN_COLS // 2, 0, GRID_W - NA_WIN_COLS)
    col_ok = (kc >= cs) & (kc < cs + NA_WIN_COLS)
    col_pick = np.eye(n_dc, dtype=np.float32)[np.clip(kc - qc + NA_WIN_COLS - 1, 0, n_dc - 1)]
    hi = lax.Precision.HIGHEST
    depth = rpb.shape[0]
    pairs = NA_Q_ROWS // 2
    by_row = jnp.einsum('lhde,ckqd->lchkqe', rpb.astype(F32), np.transpose(row_pick, (0, 2, 1, 3)), precision=hi)
    by_pair = by_row.reshape(depth, 3, NA_HEADS, NA_K_ROWS, pairs, 2 * n_dc)
    pick2 = np.zeros((2, n_dc, GRID_W, 2, GRID_W), np.float32)
    for q2 in range(2):
        pick2[q2, :, :, q2, :] = np.transpose(col_pick, (2, 1, 0))
    pick2 = pick2.reshape(2 * n_dc, GRID_W, 2 * GRID_W)
    bias = jnp.einsum('lchkjf,fym->lchkjym', by_pair, pick2, precision=hi)
    valid = (np.transpose(row_ok, (0, 2, 1)).reshape(3, NA_K_ROWS, pairs, 1, 2, 1)
             & col_ok.T[None, None, None, :, None, :])
    valid = valid.reshape(3, 1, NA_K_ROWS, pairs, GRID_W, 2 * GRID_W)
    return jnp.where(valid[None], bias, NEG_INF)


def _attend_heads(q, parts_of_pair, n_heads):
    lane = lax.broadcasted_iota(jnp.int32, (1, LANES), 1)
    half = [(lane < NA_DH).astype(BF16), (lane >= NA_DH).astype(BF16)]

    def scores(h):
        j, o = divmod(h, 2)
        qm = q[:, j * LANES:(j + 1) * LANES] * half[o]
        out = []
        for kp, _, bias in parts_of_pair(j):
            st = lax.dot_general(kp, qm, NT_DIMS, preferred_element_type=F32)
            if bias is not None:
                n_kr, n_jq = st.shape[0] // GRID_W, st.shape[1] // LANES
                st = jnp.concatenate([jnp.concatenate(
                    [st[kr * GRID_W:(kr + 1) * GRID_W, jq * LANES:(jq + 1) * LANES] + bias(o, kr, jq)
                     for jq in range(n_jq)], axis=1) for kr in range(n_kr)], axis=0)
            out.append(st)
        return out

    def softmax(s):
        m = functools.reduce(jnp.maximum, [jnp.max(t, axis=0, keepdims=True) for t in s])
        p = [jnp.exp(t - m) for t in s]
        denom = functools.reduce(jnp.add, [jnp.sum(t, axis=0, keepdims=True) for t in p])
        return [t.astype(BF16) for t in p], denom

    def values(h, p, denom):
        j, o = divmod(h, 2)
        acc = functools.reduce(jnp.add, [
            jnp.dot(vt[o * NA_DH:(o + 1) * NA_DH, :], pt, preferred_element_type=F32)
            for pt, (_, vt, _) in zip(p, parts_of_pair(j))])
        return acc / denom

    s, p, outs = {}, {}, {}
    for t in range(n_heads + 3):
        if t < n_heads:
            s[t] = scores(t)
        if 0 <= t - 1 < n_heads:
            p[t - 1] = softmax(s.pop(t - 1))
        if 0 <= t - 3 < n_heads:
            outs[t - 3] = values(t - 3, *p.pop(t - 3))
    return jnp.concatenate([outs[h] for h in range(n_heads)], axis=0).T


def _mix_residual(gla, na, u, u_prev, u_next, x, g1, wg, ws, wn, cw, cb, first, last):
    w = SC_WIDTH
    tm = u.shape[0]
    cx = u[:, w:2 * w] * u[:, 2 * w:3 * w]
    cx_prev = (u_prev[SUBLANES - 1:SUBLANES, w:2 * w] * u_prev[SUBLANES - 1:SUBLANES, 2 * w:3 * w]
               * jnp.where(first, 0.0, 1.0))
    cx_next = u_next[0:1, w:2 * w] * u_next[0:1, 2 * w:3 * w] * jnp.where(last, 0.0, 1.0)
    row = lax.broadcasted_iota(jnp.int32, cx.shape, 0)
    prev = jnp.where(row == 0, cx_prev, pltpu.roll(cx, 1, 0))
    nxt = jnp.where(row == tm - 1, cx_next, pltpu.roll(cx, tm - 1, 0))
    sc = u[:, 0:w] * (prev * cw[0:1] + cx * cw[1:2] + nxt * cw[2:3] + cb)
    mix = (jnp.dot(gla, wg, preferred_element_type=F32)
           + jnp.dot(sc.astype(BF16), ws, preferred_element_type=F32)
           + jnp.dot(na, wn, preferred_element_type=F32))
    return x + g1 * mix


def _na_kernel(q_ref, k_ref, vt_ref, kc_ref, vct_ref, bias_ref, gla_ref, scu_ref, scp_ref, scn_ref, x_ref, g1_ref,
               wg_ref, ws_ref, wn_ref, cw_ref, cb_ref, o_ref, *, rows):
    i = pl.program_id(1)
    ks = jnp.clip(i * NA_Q_ROWS - NA_WIN_ROWS // 2, 0, rows - NA_K_ROWS)
    win = pl.ds(pl.multiple_of(ks * GRID_W, NA_Q_ROWS * GRID_W), NA_K_ROWS * GRID_W)
    kw = k_ref[0, win, :]
    vtw = vt_ref[0, :, win]
    kc = kc_ref[0]
    vct = vct_ref[0]

    def parts(j):
        sl = slice(j * LANES, (j + 1) * LANES)
        return [(kw[:, sl], vtw[sl, :], lambda o, kr, jq: bias_ref[0, 2 * j + o, kr, jq]),
                (kc[:, sl], vct[sl, :], None)]

    na = _attend_heads(q_ref[0], parts, NA_HEADS).astype(BF16)
    o_ref[0] = _mix_residual(gla_ref[0], na, scu_ref[0], scp_ref[0], scn_ref[0], x_ref[0], g1_ref[0],
                             wg_ref[...], ws_ref[...], wn_ref[...], cw_ref[...], cb_ref[...],
                             i == 0, i == pl.num_programs(1) - 1)


def _na_call(q, k, vt, kc, vct, bias, gla, scu, x, g1, wg, ws, wn, cw, cb):
    b, l, w = q.shape
    d = x.shape[-1]
    c = kc.shape[1]
    rows = l // GRID_W
    nrb = rows // NA_Q_ROWS
    nq = NA_Q_ROWS * GRID_W
    nblk8, t8 = l // SUBLANES, nq // SUBLANES
    seq = lambda n: pl.BlockSpec((1, n, w), lambda bb, i: (bb, 0, 0))
    seq_t = lambda n: pl.BlockSpec((1, w, n), lambda bb, i: (bb, 0, 0))
    tok = lambda width: pl.BlockSpec((1, nq, width), lambda bb, i: (bb, i, 0))
    full = lambda a: pl.BlockSpec(a.shape, lambda bb, i: (0,) * a.ndim)
    case = lambda bb, i: (jnp.where(i == 0, 0, jnp.where(i == nrb - 1, 2, 1)),) + (0,) * (bias.ndim - 1)
    halo_prev = pl.BlockSpec((1, SUBLANES, 3 * SC_WIDTH), lambda bb, i: (bb, jnp.maximum(i * t8 - 1, 0), 0))
    halo_next = pl.BlockSpec((1, SUBLANES, 3 * SC_WIDTH), lambda bb, i: (bb, jnp.minimum((i + 1) * t8, nblk8 - 1), 0))
    return pl.pallas_call(
        functools.partial(_na_kernel, rows=rows),
        grid=(b, nrb),
        in_specs=[tok(w), seq(l), seq_t(l), seq(c), seq_t(c), pl.BlockSpec((1,) + bias.shape[1:], case),
                  tok(GLA_V), tok(3 * SC_WIDTH), halo_prev, halo_next, tok(d),
                  pl.BlockSpec((1, 1, d), lambda bb, i: (bb, 0, 0)),
                  full(wg), full(ws), full(wn), full(cw), full(cb)],
        out_specs=tok(d),
        out_shape=jax.ShapeDtypeStruct((b, l, d), F32),
        compiler_params=_cparams("arbitrary", "arbitrary"),
        name="na_out_proj",
    )(q, k, vt, kc, vct, bias, gla, scu, scu, scu, x, g1, wg, ws, wn, cw, cb)


def _ctx_attn_kernel(q_ref, k_ref, vt_ref, o_ref):
    k, vt = k_ref[0], vt_ref[0]

    def parts(j):
        sl = slice(j * LANES, (j + 1) * LANES)
        return [(k[:, sl], vt[sl, :], None)]

    o_ref[0] = _attend_heads(q_ref[0], parts, NA_HEADS).astype(o_ref.dtype)


def _ctx_attn_call(q, k, vt):
    b, c, w = q.shape
    spec = pl.BlockSpec((1, c, w), lambda bb: (bb, 0, 0))
    spec_t = pl.BlockSpec((1, w, c), lambda bb: (bb, 0, 0))
    return pl.pallas_call(
        _ctx_attn_kernel, grid=(b,), in_specs=[spec, spec, spec_t], out_specs=spec,
        out_shape=jax.ShapeDtypeStruct((b, c, w), BF16),
        compiler_params=_cparams("arbitrary"), name="ctx_attn",
    )(q, k, vt)


def _outproj_kernel(gla_ref, na_ref, scu_ref, scp_ref, scn_ref, x_ref, g1_ref,
                    wg_ref, ws_ref, wn_ref, cw_ref, cb_ref, o_ref):
    i = pl.program_id(1)
    o_ref[0] = _mix_residual(gla_ref[0], na_ref[0], scu_ref[0], scp_ref[0], scn_ref[0], x_ref[0], g1_ref[0],
                             wg_ref[...], ws_ref[...], wn_ref[...], cw_ref[...], cb_ref[...],
                             i == 0, i == pl.num_programs(1) - 1)


def _outproj_call(gla, na, scu, x, g1, wg, ws, wn, cw, cb, tm):
    b, l, d = x.shape
    bm = g1.shape[0]
    nblk8 = l // SUBLANES
    t8 = tm // SUBLANES
    mod_map = (lambda bb, i: (bb, 0, 0)) if bm > 1 else (lambda bb, i: (0, 0, 0))
    tok = lambda w: pl.BlockSpec((1, tm, w), lambda bb, i: (bb, i, 0))
    full = lambda a: pl.BlockSpec(a.shape, lambda bb, i: (0,) * a.ndim)
    halo_prev = pl.BlockSpec((1, SUBLANES, 3 * SC_WIDTH), lambda bb, i: (bb, jnp.maximum(i * t8 - 1, 0), 0))
    halo_next = pl.BlockSpec((1, SUBLANES, 3 * SC_WIDTH), lambda bb, i: (bb, jnp.minimum((i + 1) * t8, nblk8 - 1), 0))
    return pl.pallas_call(
        _outproj_kernel,
        grid=(b, l // tm),
        in_specs=[tok(GLA_V), tok(NA_W), tok(3 * SC_WIDTH), halo_prev, halo_next, tok(d),
                  pl.BlockSpec((1, 1, d), mod_map), full(wg), full(ws), full(wn), full(cw), full(cb)],
        out_specs=tok(d),
        out_shape=jax.ShapeDtypeStruct((b, l, d), F32),
        compiler_params=_cparams("arbitrary", "arbitrary"),
        name="out_proj",
    )(gla, na, scu, scu, scu, x, g1, wg, ws, wn, cw, cb)


def _ffn_kernel(*refs, final, chunk):
    if final:
        (x_ref, xp_ref, xn_ref, sh_ref, sc_ref, g2_ref, ng_ref, wup_ref, cw_ref, cb_ref, wdn_ref, fg_ref,
         o_ref, act_ref) = refs
    else:
        (x_ref, xp_ref, xn_ref, sh_ref, sc_ref, g2_ref, ng_ref, wup_ref, cw_ref, cb_ref, wdn_ref,
         o_ref, act_ref) = refs
    i = pl.program_id(1)
    last = pl.num_programs(1) - 1
    x = x_ref[0]
    tm = x.shape[0]
    hid = wdn_ref.shape[0]
    nm = lambda t: _norm_mod(t, ng_ref[...], sh_ref[0], sc_ref[0])
    hp = nm(xp_ref[0]) * jnp.where(i > 0, 1.0, 0.0)
    hn = nm(xn_ref[0]) * jnp.where(i < last, 1.0, 0.0)
    hb = jnp.concatenate([hp, nm(x), hn], axis=0).astype(BF16)
    ext = tm + 2 * SUBLANES
    inner = slice(SUBLANES, SUBLANES + tm)

    def conv(u, c0, w):
        cw = cw_ref[:, c0:c0 + w]
        return (pltpu.roll(u, 1, 0)[inner] * cw[0:1] + u[inner] * cw[1:2]
                + pltpu.roll(u, ext - 1, 0)[inner] * cw[2:3] + cb_ref[:, c0:c0 + w])

    for c0 in range(0, hid, chunk):
        w = min(chunk, hid - c0)
        ua = jnp.dot(hb, wup_ref[:, c0:c0 + w], preferred_element_type=F32)
        ub = jnp.dot(hb, wup_ref[:, hid + c0:hid + c0 + w], preferred_element_type=F32)
        act_ref[:, c0:c0 + w] = (_silu(conv(ua, c0, w)) * conv(ub, hid + c0, w)).astype(BF16)
    y = x + g2_ref[0] * jnp.dot(act_ref[...], wdn_ref[...], preferred_element_type=F32)
    if final:
        ms = jnp.mean(y * y, axis=-1, keepdims=True)
        y = y * lax.rsqrt(ms + EPS) * fg_ref[...]
    o_ref[0] = y


def _ffn_call(x, shift, scale, g2, gain, wup, cw, cb, wdn, final_gain, tm):
    b, l, d = x.shape
    bm = shift.shape[0]
    nblk8 = l // SUBLANES
    t8 = tm // SUBLANES
    final = final_gain is not None
    mod_map = (lambda bb, i: (bb, 0, 0)) if bm > 1 else (lambda bb, i: (0, 0, 0))
    tok = pl.BlockSpec((1, tm, d), lambda bb, i: (bb, i, 0))
    full = lambda a: pl.BlockSpec(a.shape, lambda bb, i: (0,) * a.ndim)
    resident = lambda a: pl.BlockSpec(a.shape, lambda bb, i: (0,) * a.ndim, pipeline_mode=pl.Buffered(1))
    halo_prev = pl.BlockSpec((1, SUBLANES, d), lambda bb, i: (bb, jnp.maximum(i * t8 - 1, 0), 0))
    halo_next = pl.BlockSpec((1, SUBLANES, d), lambda bb, i: (bb, jnp.minimum((i + 1) * t8, nblk8 - 1), 0))
    mod = pl.BlockSpec((1, 1, d), mod_map)
    in_specs = [tok, halo_prev, halo_next, mod, mod, mod, full(gain), resident(wup), full(cw), full(cb), resident(wdn)]
    args = [x, x, x, shift, scale, g2, gain, wup, cw, cb, wdn]
    if final:
        in_specs.append(full(final_gain))
        args.append(final_gain)
    return pl.pallas_call(
        functools.partial(_ffn_kernel, final=final, chunk=FFN_CHUNK),
        grid=(b, l // tm),
        in_specs=in_specs,
        out_specs=tok,
        out_shape=jax.ShapeDtypeStruct((b, l, d), F32),
        scratch_shapes=[pltpu.VMEM((tm, wdn.shape[0]), BF16)],
        compiler_params=_cparams("arbitrary", "arbitrary"),
        name="conv_ffn_final" if final else "conv_ffn",
    )(*args)


def _rope_tables(n_tokens):
    t = jnp.arange(n_tokens)
    n_freq = GLA_DK // 4
    inv_freq = ROPE_BASE ** (-jnp.arange(n_freq, dtype=F32) / n_freq)
    row = (t // GRID_W).astype(F32)[:, None] * inv_freq
    col = (t % GRID_W).astype(F32)[:, None] * inv_freq
    ang = jnp.concatenate([row, col], axis=-1)
    cos = jnp.repeat(jnp.cos(ang), 2, axis=-1)
    sin = jnp.repeat(jnp.sin(ang), 2, axis=-1) * jnp.tile(jnp.asarray([-1.0, 1.0], F32), GLA_DK // 2)
    pad = QK_PAD - GLA_QK
    cos = jnp.concatenate([jnp.tile(cos, (1, GLA_HEADS)), jnp.ones((n_tokens, pad), F32)], axis=-1)
    sin = jnp.concatenate([jnp.tile(sin, (1, GLA_HEADS)), jnp.zeros((n_tokens, pad), F32)], axis=-1)
    return cos, sin


def _tile_rows(n, target):
    t = min(n, target)
    assert n % t == 0 and t % GLA_GROUP == 0, (n, t)
    return t


def kernel(x, c, ctx, c_ctx, w_ada, b_ada, norm_mix_g, norm_ffn_g, w_in, gla_wg2_fw, gla_bg_fw, gla_wg2_bw, gla_bg_bw, gla_norm_g, sc_conv_w, sc_conv_b, na_rpb, w_out, ffn_w_up, ffn_conv_w, ffn_conv_b, ffn_w_down, final_norm_g):
    bsz, n_lat, d = x.shape
    n_ctx = ctx.shape[1]
    depth = w_in.shape[0]
    rows = n_lat // GRID_W
    assert n_lat % (NA_Q_ROWS * GRID_W) == 0 and rows >= NA_K_ROWS and bsz + 1 <= 2 * SUBLANES
    tm_lat, tm_ctx = _tile_rows(n_lat, LATENT_TILE), _tile_rows(n_ctx, LATENT_TILE)
    ts_lat = _tile_rows(n_lat, GLA_TILE)

    crows = jnp.zeros((2 * SUBLANES, d), F32).at[:bsz].set(c).at[bsz].set(c_ctx)
    mod_all = _ada_call(crows, w_ada, b_ada)

    zpad = lambda n: jnp.zeros((depth, d, n), BF16)
    o = np.cumsum([0, GLA_QK, GLA_QK, GLA_V, GLA_LOWRANK, GLA_LOWRANK, GLA_V,
                   SC_WIDTH, SC_WIDTH, SC_WIDTH, NA_W, NA_W, NA_W])
    w_in_b = w_in.astype(BF16)
    col = lambda n: w_in_b[:, :, o[n]:o[n + 1]]
    qpad = zpad(QK_PAD - GLA_QK)
    w_all = jnp.concatenate([col(0), qpad, col(1), qpad, col(2), col(5), col(3), col(4),
                             zpad(GLR_PAD - 2 * GLA_LOWRANK), w_in_b[:, :, o[6]:o[9]],
                             col(9) * jnp.asarray(NA_DH ** -0.5, BF16), w_in_b[:, :, o[10]:o[12]]], axis=-1)
    assert w_all.shape[-1] == INPROJ_WIDTH
    w_out_b = w_out.astype(BF16)
    wo_gla, wo_sc, wo_na = w_out_b[:, :GLA_V], w_out_b[:, GLA_V:GLA_V + SC_WIDTH], w_out_b[:, GLA_V + SC_WIDTH:]
    wup = ffn_w_up.astype(BF16)
    wdn = ffn_w_down.astype(BF16)

    def gate_w(w2, first_row):
        full = jnp.zeros((depth, GLR_PAD, QK_PAD), F32)
        return full.at[:, first_row:first_row + GLA_LOWRANK, :GLA_QK].set(w2).astype(BF16)

    def gate_b(bias):
        return jnp.zeros((depth, 1, QK_PAD), F32).at[:, 0, :GLA_QK].set(bias)

    wg_f, wg_b = gate_w(gla_wg2_fw, 0), gate_w(gla_wg2_bw, GLA_LOWRANK)
    bg_f, bg_b = gate_b(gla_bg_fw), gate_b(gla_bg_bw)
    gla_gain = jnp.tile(gla_norm_g, (1, GLA_HEADS))[:, None, :]
    hv = np.arange(GLA_V) // GLA_DV
    head_mean = jnp.asarray((hv[:, None] == hv[None, :]) / GLA_DV, BF16)
    tabs_f, tabs_b = _gla_tables(False), _gla_tables(True)
    rope_tabs = _rope_tables(n_lat)
    zero_state = jnp.zeros((bsz, QK_PAD // 2, GLA_V), F32)
    na_bias = _na_bias_tables(na_rpb, rows)

    xc = ctx
    for layer in range(depth):
        update_ctx = layer < depth - 1
        mod = mod_all[layer]
        sh1, sc1, g1, sh2, sc2, g2 = [mod[:bsz, n * d:(n + 1) * d][:, None, :] for n in range(6)]
        sh1c, sc1c, g1c, sh2c, sc2c, g2c = [mod[bsz:bsz + 1, n * d:(n + 1) * d][:, None, :] for n in range(6)]
        gain_mix = norm_mix_g[layer][None, :]
        gain_ffn = norm_ffn_g[layer][None, :]

        cq, ck, cv, cr, cglr, cscu, cnaq, cnak, cnavt = _inproj_call(
            xc, sh1c, sc1c, gain_mix, w_all[layer], None, tm_ctx)
        lq, lk, lv, lr, lglr, lscu, lnaq, lnak, lnavt = _inproj_call(
            x, sh1, sc1, gain_mix, w_all[layer], rope_tabs, tm_lat)

        fin = lambda r, o_f: (r, o_f, gla_gain[layer], head_mean)
        oc_f, state_f = _gla_call(cq, ck, cv, cglr, wg_f[layer], bg_f[layer], tabs_f, zero_state, tm_ctx, False)
        gla_ctx, state_b = _gla_call(cq, ck, cv, cglr, wg_b[layer], bg_b[layer], tabs_b, zero_state, tm_ctx, True,
                                     fin(cr, oc_f))
        ol_f, _ = _gla_call(lq, lk, lv, lglr, wg_f[layer], bg_f[layer], tabs_f, state_f, ts_lat, False)
        gla_lat, _ = _gla_call(lq, lk, lv, lglr, wg_b[layer], bg_b[layer], tabs_b, state_b, ts_lat, True,
                               fin(lr, ol_f))

        cw_sc, cb_sc = sc_conv_w[layer], sc_conv_b[layer][None, :]
        wo = (wo_gla[layer], wo_sc[layer], wo_na[layer])
        x = _na_call(lnaq, lnak, lnavt, cnak, cnavt, na_bias[layer], gla_lat, lscu, x, g1, *wo, cw_sc, cb_sc)
        ffn = (wup[layer], ffn_conv_w[layer], ffn_conv_b[layer][None, :], wdn[layer])
        x = _ffn_call(x, sh2, sc2, g2, gain_ffn, *ffn,
                      final_norm_g[None, :] if layer == depth - 1 else None, tm_lat)

        if update_ctx:
            na_ctx = _ctx_attn_call(cnaq, cnak, cnavt)
            xc = _outproj_call(gla_ctx, na_ctx, cscu, xc, g1c, *wo, cw_sc, cb_sc, tm_ctx)
            xc = _ffn_call(xc, sh2c, sc2c, g2c, gain_ffn, *ffn, None, tm_ctx)
    return x
```

```python
import functools

import numpy as np
import jax
import jax.numpy as jnp
from jax import lax
from jax.experimental import pallas as pl
from jax.experimental.pallas import tpu as pltpu

F32 = jnp.float32
BF16 = jnp.bfloat16

GRID_W = 64
EPS = 1e-6
NEG_INF = -1e30
GLA_HEADS = 6
GLA_DK = 32
GLA_DV = 64
GLA_LOWRANK = 16
GLA_TAU = 16.0
ROPE_BASE = 10000.0
SC_WIDTH = 256
NA_HEADS = 6
NA_DH = 64
NA_WIN_ROWS = 8
NA_WIN_COLS = 16
GLA_QK = GLA_HEADS * GLA_DK
GLA_V = GLA_HEADS * GLA_DV
NA_W = NA_HEADS * NA_DH

LANES = 128
SUBLANES = 8
VMEM_LIMIT_BYTES = 56 * 1024 * 1024

QK_PAD = 2 * LANES
GLR_PAD = LANES
GLA_GROUP = 128
GLA_BASE = 16
GLA_LEVELS = 4
FFN_CHUNK = 6 * LANES
LATENT_TILE = 1024
GLA_TILE = 2048
HEADS_A = LANES // GLA_DK
GLA_SINGLE_LEVEL_DECAY = 64.0
ADA_COLS = 12 * LANES
NA_Q_ROWS = 4
NA_K_ROWS = 12

NT_DIMS = (((1,), (1,)), ((), ()))


def _cparams(*sem):
    return pltpu.CompilerParams(dimension_semantics=sem, vmem_limit_bytes=VMEM_LIMIT_BYTES)


def _silu(t):
    return t / (1.0 + jnp.exp(-t))


def _norm_mod(t, gain, shift, scale):
    ms = jnp.mean(t * t, axis=-1, keepdims=True)
    return (t * lax.rsqrt(ms + EPS) * gain) * (1.0 + scale) + shift


def _ada_kernel(c_ref, w_ref, b_ref, o_ref):
    c = c_ref[...]
    o_ref[0] = jnp.dot(_silu(c), w_ref[0], preferred_element_type=F32,
                       precision=lax.Precision.HIGHEST) + b_ref[0]


def _ada_call(crows, w_ada, b_ada):
    depth, d, n = w_ada.shape
    rows = crows.shape[0]
    tn = ADA_COLS
    assert n % tn == 0
    return pl.pallas_call(
        _ada_kernel,
        grid=(depth, n // tn),
        in_specs=[pl.BlockSpec((rows, d), lambda l, j: (0, 0)),
                  pl.BlockSpec((1, d, tn), lambda l, j: (l, 0, j)),
                  pl.BlockSpec((1, 1, tn), lambda l, j: (l, 0, j))],
        out_specs=pl.BlockSpec((1, rows, tn), lambda l, j: (l, 0, j)),
        out_shape=jax.ShapeDtypeStruct((depth, rows, n), F32),
        compiler_params=_cparams("arbitrary", "arbitrary"),
        name="ada_mod",
    )(crows, w_ada, b_ada.reshape(depth, 1, n))


INPROJ_OUTS = ((QK_PAD, F32), (QK_PAD, F32), (GLA_V, F32), (GLA_V, F32), (GLR_PAD, F32),
               (3 * SC_WIDTH, F32), (NA_W, BF16), (NA_W, BF16), (NA_W, BF16))
INPROJ_STARTS = tuple(int(s) for s in np.cumsum([0] + [w for w, _ in INPROJ_OUTS]))
INPROJ_WIDTH = INPROJ_STARTS[-1]
MXU_COLS = 2 * LANES
NAV_OUT = len(INPROJ_OUTS) - 1


def _rotary(t, c, s):
    lane = lax.broadcasted_iota(jnp.int32, t.shape, 1)
    swapped = jnp.where(lane % 2 == 0, pltpu.roll(t, LANES - 1, 1), pltpu.roll(t, 1, 1))
    return t * c + swapped * s


def _inproj_kernel(*refs, rope):
    if rope:
        x_ref, sh_ref, sc_ref, g_ref, cos_ref, sin_ref, w_ref = refs[:7]
        outs = refs[7:]
    else:
        x_ref, sh_ref, sc_ref, g_ref, w_ref = refs[:5]
        outs = refs[5:]
    navt_ref = outs[-1]
    nav_pieces = []
    hb = _norm_mod(x_ref[0], g_ref[...], sh_ref[0], sc_ref[0]).astype(BF16)
    for n0 in range(0, INPROJ_WIDTH, MXU_COLS):
        res = jnp.dot(hb, w_ref[:, n0:n0 + MXU_COLS], preferred_element_type=F32)
        for a in range(n0, n0 + MXU_COLS, LANES):
            idx = max(i for i, s0 in enumerate(INPROJ_STARTS[:-1]) if s0 <= a)
            off = a - INPROJ_STARTS[idx]
            piece = res[:, a - n0:a - n0 + LANES]
            if idx < 2:
                if rope:
                    piece = _rotary(piece, cos_ref[:, off:off + LANES], sin_ref[:, off:off + LANES])
                if idx == 0:
                    piece = piece * (GLA_DK ** -0.5)
            if idx == NAV_OUT:
                nav_pieces.append(piece)
            else:
                outs[idx][0, :, off:off + LANES] = piece.astype(outs[idx].dtype)
    navt_ref[0] = jnp.concatenate(nav_pieces, axis=1).T.astype(BF16)


def _inproj_call(x, shift, scale, gain, w_all, rope_tabs, tm):
    b, l, d = x.shape
    bm = shift.shape[0]
    rope = rope_tabs is not None
    mod_map = (lambda i, bb: (bb, 0, 0)) if bm > 1 else (lambda i, bb: (0, 0, 0))
    tok = lambda w: pl.BlockSpec((1, tm, w), lambda i, bb: (bb, i, 0))
    full = lambda a: pl.BlockSpec(a.shape, lambda i, bb: (0,) * a.ndim)
    in_specs = [tok(d), pl.BlockSpec((1, 1, d), mod_map), pl.BlockSpec((1, 1, d), mod_map), full(gain)]
    args = [x, shift, scale, gain]
    if rope:
        in_specs += [pl.BlockSpec((tm, QK_PAD), lambda i, bb: (i, 0))] * 2
        args += list(rope_tabs)
    in_specs.append(full(w_all))
    args.append(w_all)
    return pl.pallas_call(
        functools.partial(_inproj_kernel, rope=rope),
        grid=(l // tm, b),
        in_specs=in_specs,
        out_specs=([tok(w) for w, _ in INPROJ_OUTS[:NAV_OUT]]
                   + [pl.BlockSpec((1, NA_W, tm), lambda i, bb: (bb, 0, i))]),
        out_shape=([jax.ShapeDtypeStruct((b, l, w), dt) for w, dt in INPROJ_OUTS[:NAV_OUT]]
                   + [jax.ShapeDtypeStruct((b, NA_W, l), BF16)]),
        compiler_params=_cparams("arbitrary", "arbitrary"),
        name="in_proj_rope" if rope else "in_proj_ctx",
    )(*args)


def _gla_tables(reverse):
    t = np.arange(GLA_GROUP)
    i, j = t[:, None], t[None, :]
    before = (j >= i) if reverse else (j <= i)
    strictly_after = (j < i) if reverse else (j > i)
    sizes = [GLA_BASE << level for level in range(GLA_LEVELS)]
    assert sizes[-1] == GLA_GROUP
    maps = []
    for n in sizes:
        maps.append((i // n == j // n) & before)
    for n in sizes:
        maps.append((i // n == j // n) & strictly_after)
    masks = [(i // GLA_BASE == j // GLA_BASE) & before]
    for n in sizes[1:]:
        h = n // 2
        q_late, k_early = (i % n) >= h, (j % n) < h
        if reverse:
            q_late, k_early = (i % n) < h, (j % n) >= h
        masks.append((i // n == j // n) & q_late & k_early)
    cm = np.concatenate(maps, axis=0).astype(np.float32)
    cm2 = np.concatenate([maps[GLA_LEVELS - 1], maps[-1]], axis=0).astype(np.float32)
    lm = np.stack(masks, axis=0).astype(np.float32)
    return jnp.asarray(cm, BF16), jnp.asarray(cm2, BF16), jnp.asarray(lm, F32)


def _gla_kernel(*refs, reverse, finalize, ngroups):
    if finalize:
        (q_ref, k_ref, v_ref, glr_ref, wg_ref, bg_ref, cm_ref, cm2_ref, lm_ref, s0_ref,
         r_ref, of_ref, ng_ref, bd_ref, out_ref, sfin_ref, st_ref, gate_ref) = refs
    else:
        (q_ref, k_ref, v_ref, glr_ref, wg_ref, bg_ref, cm_ref, cm2_ref, lm_ref, s0_ref,
         out_ref, sfin_ref, st_ref, gate_ref) = refs
    step = pl.program_id(1)
    g = GLA_GROUP
    kc = QK_PAD // 2
    vc = HEADS_A * GLA_DV

    @pl.when(step == 0)
    def _():
        st_ref[...] = s0_ref[0]

    z = jnp.dot(glr_ref[0].astype(BF16), wg_ref[...], preferred_element_type=F32) + bg_ref[...]
    gate_all = (jnp.minimum(z, 0.0) - jnp.log(1.0 + jnp.exp(-jnp.abs(z)))) * (1.0 / GLA_TAU)
    gate_ref[...] = gate_all
    total = jnp.sum(gate_all.reshape(ngroups, g, QK_PAD), axis=1)
    single_level = jnp.min(total) > -GLA_SINGLE_LEVEL_DECAY

    k_head = lax.broadcasted_iota(jnp.int32, (g, kc), 1) // GLA_DK
    kmask = [k_head == h for h in range(HEADS_A)]
    va_head = lax.broadcasted_iota(jnp.int32, (g, vc), 1) // GLA_DV
    vb_head = lax.broadcasted_iota(jnp.int32, (g, GLA_V - vc), 1) // GLA_DV
    st_row = lax.broadcasted_iota(jnp.int32, (kc, GLA_V), 0) // GLA_DK
    st_col = lax.broadcasted_iota(jnp.int32, (kc, GLA_V), 1)
    state_mask = st_col // GLA_DV == st_row + jnp.where(st_col >= vc, HEADS_A, 0)
    lmask = [lm_ref[l] > 0.5 for l in range(GLA_LEVELS)]
    causal = functools.reduce(jnp.logical_or, lmask)

    def stack_heads(t, masks, n):
        return jnp.concatenate([jnp.where(masks[h], t, 0.0) for h in range(n)], axis=0).astype(BF16)

    def scores(qs, ks):
        sa = lax.dot_general(qs[:, :kc].astype(BF16), stack_heads(ks[:, :kc], kmask, HEADS_A), NT_DIMS,
                             preferred_element_type=F32)
        sb = lax.dot_general(qs[:, kc:].astype(BF16), stack_heads(ks[:, kc:], kmask, GLA_HEADS - HEADS_A),
                             NT_DIMS, preferred_element_type=F32)
        return jnp.concatenate([sa, sb], axis=1)

    def score_stage(gi, robust):
        gidx = (ngroups - 1 - gi) if reverse else gi
        rows = pl.ds(gidx * g, g)
        q = q_ref[0, rows, :]
        k = k_ref[0, rows, :]
        gate = gate_ref[rows, :]
        ghi = gate.astype(BF16)
        glo = (gate - ghi.astype(F32)).astype(BF16)
        cmat = cm_ref[...] if robust else cm2_ref[0:g, :]
        cs = jnp.dot(cmat, ghi, preferred_element_type=F32) + jnp.dot(cmat, glo, preferred_element_type=F32)
        if robust:
            p16, p32, p64, p128, s16, s32, s64, s128 = [cs[n * g:(n + 1) * g] for n in range(8)]
            q0 = q * jnp.exp(p16)
            a = [scores(q0, k * jnp.exp(-p16)),
                 scores(q0, k * jnp.exp(s16)),
                 scores(q * jnp.exp(p32), k * jnp.exp(s32)),
                 scores(q * jnp.exp(p64), k * jnp.exp(s64))]
            q4 = q * jnp.exp(p128)
            total = p128[0:1] + s128[0:1]
        else:
            p128 = cs
            total = jnp.sum(gate, axis=0, keepdims=True)
            s128 = total - p128
            q4 = q * jnp.exp(p128)
            a = scores(q4, k * jnp.exp(-p128))
        k4t = (k * jnp.exp(s128)).T.astype(BF16)
        return rows, a, q4.astype(BF16), k4t, total

    def value_stage(carry, robust):
        rows, a, q4b, k4t, total = carry
        v = v_ref[0, rows, :]
        if robust:
            blocks = []
            for h in range(GLA_HEADS):
                sl = slice(h * g, (h + 1) * g)
                blk = jnp.where(lmask[3], a[3][:, sl], 0.0)
                for l in (2, 1, 0):
                    blk = jnp.where(lmask[l], a[l][:, sl], blk)
                blocks.append(blk)
        else:
            blocks = [jnp.where(causal, a[:, h * g:(h + 1) * g], 0.0) for h in range(GLA_HEADS)]
        amat_a = jnp.concatenate(blocks[:HEADS_A], axis=1).astype(BF16)
        amat_b = jnp.concatenate(blocks[HEADS_A:], axis=1).astype(BF16)
        v_a = stack_heads(v[:, :vc], [va_head == h for h in range(HEADS_A)], HEADS_A)
        v_b = stack_heads(v[:, vc:], [vb_head == h for h in range(GLA_HEADS - HEADS_A)], GLA_HEADS - HEADS_A)
        st = st_ref[...]
        stb = st.astype(BF16)
        o_a = (jnp.dot(amat_a, v_a, preferred_element_type=F32)
               + jnp.dot(q4b[:, :kc], stb[:, :vc], preferred_element_type=F32))
        o_b = (jnp.dot(amat_b, v_b, preferred_element_type=F32)
               + jnp.dot(q4b[:, kc:], stb[:, vc:], preferred_element_type=F32))
        o = jnp.concatenate([o_a, o_b], axis=1)
        vb = v.astype(BF16)
        upd = jnp.concatenate([jnp.dot(k4t[:kc], vb[:, :vc], preferred_element_type=F32),
                               jnp.dot(k4t[kc:], vb[:, vc:], preferred_element_type=F32)], axis=1)
        dcol = jnp.exp(jnp.broadcast_to(total, (g, 2 * kc)).T[:, 0:1])
        dec = jnp.concatenate([jnp.broadcast_to(dcol[:kc], (kc, vc)),
                               jnp.broadcast_to(dcol[kc:], (kc, GLA_V - vc))], axis=1)
        st_ref[...] = st * dec + jnp.where(state_mask, upd, 0.0)
        if finalize:
            ot = o + of_ref[0, rows, :]
            ms = jnp.dot((ot * ot).astype(BF16), bd_ref[...], preferred_element_type=F32)
            y = ot * lax.rsqrt(ms + EPS) * ng_ref[...] * _silu(r_ref[0, rows, :])
            out_ref[0, rows, :] = y.astype(out_ref.dtype)
        else:
            out_ref[0, rows, :] = o

    def run(robust):
        pending = None
        for gi in range(ngroups + 1):
            upcoming = score_stage(gi, robust) if gi < ngroups else None
            if pending is not None:
                value_stage(pending, robust)
            pending = upcoming

    @pl.when(single_level)
    def _():
        run(False)

    @pl.when(jnp.logical_not(single_level))
    def _():
        run(True)

    @pl.when(step == pl.num_programs(1) - 1)
    def _():
        sfin_ref[0] = st_ref[...]


def _gla_call(q, k, v, glr, wg, bg, tabs, s0, ts, reverse, fin=None):
    b, l, _ = q.shape
    nb = l // ts
    finalize = fin is not None
    cm, cm2, lm = tabs
    blk = (lambda bb, i: (bb, nb - 1 - i, 0)) if reverse else (lambda bb, i: (bb, i, 0))
    tok = lambda w: pl.BlockSpec((1, ts, w), blk)
    full = lambda a: pl.BlockSpec(a.shape, lambda bb, i: (0,) * a.ndim)
    state = pl.BlockSpec((1, QK_PAD // 2, GLA_V), lambda bb, i: (bb, 0, 0))
    in_specs = [tok(QK_PAD), tok(QK_PAD), tok(GLA_V), tok(GLR_PAD), full(wg), full(bg), full(cm), full(cm2),
                full(lm), state]
    args = [q, k, v, glr, wg, bg, cm, cm2, lm, s0]
    if finalize:
        r, o_other, ng, bd = fin
        in_specs += [tok(GLA_V), tok(GLA_V), full(ng), full(bd)]
        args += [r, o_other, ng, bd]
    out_dt = BF16 if finalize else F32
    return pl.pallas_call(
        functools.partial(_gla_kernel, reverse=reverse, finalize=finalize, ngroups=ts // GLA_GROUP),
        grid=(b, nb),
        in_specs=in_specs,
        out_specs=[tok(GLA_V), state],
        out_shape=[jax.ShapeDtypeStruct((b, l, GLA_V), out_dt),
                   jax.ShapeDtypeStruct((b, QK_PAD // 2, GLA_V), F32)],
        scratch_shapes=[pltpu.VMEM((QK_PAD // 2, GLA_V), F32), pltpu.VMEM((ts, QK_PAD), F32)],
        compiler_params=_cparams("arbitrary", "arbitrary"),
        name="gla_bwd" if reverse else "gla_fwd",
    )(*args)


def _na_bias_tables(rpb, rows):
    kh = min(NA_WIN_ROWS, rows)
    n_dr, n_dc = 2 * NA_WIN_ROWS - 1, 2 * NA_WIN_COLS - 1
    qr, kr = np.arange(NA_Q_ROWS)[:, None], np.arange(NA_K_ROWS)[None, :]
    row_pick, row_ok = [], []
    for r0, ks in ((0, 0), (NA_WIN_ROWS // 2, 0), (rows - NA_Q_ROWS, rows - NA_K_ROWS)):
        r, r2 = r0 + qr, ks + kr
        rs = np.clip(r - kh // 2, 0, rows - kh)
        row_ok.append((r2 >= rs) & (r2 < rs + kh))
        row_pick.append(np.eye(n_dr, dtype=np.float32)[np.clip(r2 - r + NA_WIN_ROWS - 1, 0, n_dr - 1)])
    row_pick, row_ok = np.stack(row_pick), np.stack(row_ok)
    qc, kc = np.arange(GRID_W)[:, None], np.arange(GRID_W)[None, :]
    cs = np.clip(qc - NA_WIN_COLS // 2, 0, GRID_W - NA_WIN_COLS)
    col_ok = (kc >= cs) & (kc < cs + NA_WIN_COLS)
    col_pick = np.eye(n_dc, dtype=np.float32)[np.clip(kc - qc + NA_WIN_COLS - 1, 0, n_dc - 1)]
    hi = lax.Precision.HIGHEST
    depth = rpb.shape[0]
    pairs = NA_Q_ROWS // 2
    by_row = jnp.einsum('lhde,ckqd->lchkqe', rpb.astype(F32), np.transpose(row_pick, (0, 2, 1, 3)), precision=hi)
    by_pair = by_row.reshape(depth, 3, NA_HEADS, NA_K_ROWS, pairs, 2 * n_dc)
    pick2 = np.zeros((2, n_dc, GRID_W, 2, GRID_W), np.float32)
    for q2 in range(2):
        pick2[q2, :, :, q2, :] = np.transpose(col_pick, (2, 1, 0))
    pick2 = pick2.reshape(2 * n_dc, GRID_W, 2 * GRID_W)
    bias = jnp.einsum('lchkjf,fym->lchkjym', by_pair, pick2, precision=hi)
    valid = (np.transpose(row_ok, (0, 2, 1)).reshape(3, NA_K_ROWS, pairs, 1, 2, 1)
             & col_ok.T[None, None, None, :, None, :])
    valid = valid.reshape(3, 1, NA_K_ROWS, pairs, GRID_W, 2 * GRID_W)
    return jnp.where(valid[None], bias, NEG_INF)


def _attend_heads(q, parts_of_pair, n_heads):
    lane = lax.broadcasted_iota(jnp.int32, (1, LANES), 1)
    half = [(lane < NA_DH).astype(BF16), (lane >= NA_DH).astype(BF16)]

    def scores(h):
        j, o = divmod(h, 2)
        qm = q[:, j * LANES:(j + 1) * LANES] * half[o]
        out = []
        for kp, _, bias in parts_of_pair(j):
            st = lax.dot_general(kp, qm, NT_DIMS, preferred_element_type=F32)
            if bias is not None:
                n_kr, n_jq = st.shape[0] // GRID_W, st.shape[1] // LANES
                st = jnp.concatenate([jnp.concatenate(
                    [st[kr * GRID_W:(kr + 1) * GRID_W, jq * LANES:(jq + 1) * LANES] + bias(o, kr, jq)
                     for jq in range(n_jq)], axis=1) for kr in range(n_kr)], axis=0)
            out.append(st)
        return out

    def softmax(s):
        m = functools.reduce(jnp.maximum, [jnp.max(t, axis=0, keepdims=True) for t in s])
        p = [jnp.exp(t - m) for t in s]
        denom = functools.reduce(jnp.add, [jnp.sum(t, axis=0, keepdims=True) for t in p])
        return [t.astype(BF16) for t in p], denom

    def values(h, p, denom):
        j, o = divmod(h, 2)
        acc = functools.reduce(jnp.add, [
            jnp.dot(vt[o * NA_DH:(o + 1) * NA_DH, :], pt, preferred_element_type=F32)
            for pt, (_, vt, _) in zip(p, parts_of_pair(j))])
        return acc / denom

    s, p, outs = {}, {}, {}
    for t in range(n_heads + 3):
        if t < n_heads:
            s[t] = scores(t)
        if 0 <= t - 1 < n_heads:
            p[t - 1] = softmax(s.pop(t - 1))
        if 0 <= t - 3 < n_heads:
            outs[t - 3] = values(t - 3, *p.pop(t - 3))
    return jnp.concatenate([outs[h] for h in range(n_heads)], axis=0).T


def _mix_residual(gla, na, u, u_prev, u_next, x, g1, wg, ws, wn, cw, cb, first, last):
    w = SC_WIDTH
    tm = u.shape[0]
    cx = u[:, w:2 * w] * u[:, 2 * w:3 * w]
    cx_prev = (u_prev[SUBLANES - 1:SUBLANES, w:2 * w] * u_prev[SUBLANES - 1:SUBLANES, 2 * w:3 * w]
               * jnp.where(first, 0.0, 1.0))
    cx_next = u_next[0:1, w:2 * w] * u_next[0:1, 2 * w:3 * w] * jnp.where(last, 0.0, 1.0)
    row = lax.broadcasted_iota(jnp.int32, cx.shape, 0)
    prev = jnp.where(row == 0, cx_prev, pltpu.roll(cx, 1, 0))
    nxt = jnp.where(row == tm - 1, cx_next, pltpu.roll(cx, tm - 1, 0))
    sc = u[:, 0:w] * (prev * cw[0:1] + cx * cw[1:2] + nxt * cw[2:3] + cb)
    mix = (jnp.dot(gla, wg, preferred_element_type=F32)
           + jnp.dot(sc.astype(BF16), ws, preferred_element_type=F32)
           + jnp.dot(na, wn, preferred_element_type=F32))
    return x + g1 * mix


def _na_kernel(q_ref, k_ref, vt_ref, kc_ref, vct_ref, bias_ref, gla_ref, scu_ref, scp_ref, scn_ref, x_ref, g1_ref,
               wg_ref, ws_ref, wn_ref, cw_ref, cb_ref, o_ref, *, rows):
    i = pl.program_id(1)
    ks = jnp.clip(i * NA_Q_ROWS - NA_WIN_ROWS // 2, 0, rows - NA_K_ROWS)
    win = pl.ds(pl.multiple_of(ks * GRID_W, NA_Q_ROWS * GRID_W), NA_K_ROWS * GRID_W)
    kw = k_ref[0, win, :]
    vtw = vt_ref[0, :, win]
    kc = kc_ref[0]
    vct = vct_ref[0]

    def parts(j):
        sl = slice(j * LANES, (j + 1) * LANES)
        return [(kw[:, sl], vtw[sl, :], lambda o, kr, jq: bias_ref[0, 2 * j + o, kr, jq]),
                (kc[:, sl], vct[sl, :], None)]

    na = _attend_heads(q_ref[0], parts, NA_HEADS).astype(BF16)
    o_ref[0] = _mix_residual(gla_ref[0], na, scu_ref[0], scp_ref[0], scn_ref[0], x_ref[0], g1_ref[0],
                             wg_ref[...], ws_ref[...], wn_ref[...], cw_ref[...], cb_ref[...],
                             i == 0, i == pl.num_programs(1) - 1)


def _na_call(q, k, vt, kc, vct, bias, gla, scu, x, g1, wg, ws, wn, cw, cb):
    b, l, w = q.shape
    d = x.shape[-1]
    c = kc.shape[1]
    rows = l // GRID_W
    nrb = rows // NA_Q_ROWS
    nq = NA_Q_ROWS * GRID_W
    nblk8, t8 = l // SUBLANES, nq // SUBLANES
    seq = lambda n: pl.BlockSpec((1, n, w), lambda bb, i: (bb, 0, 0))
    seq_t = lambda n: pl.BlockSpec((1, w, n), lambda bb, i: (bb, 0, 0))
    tok = lambda width: pl.BlockSpec((1, nq, width), lambda bb, i: (bb, i, 0))
    full = lambda a: pl.BlockSpec(a.shape, lambda bb, i: (0,) * a.ndim)
    case = lambda bb, i: (jnp.where(i == 0, 0, jnp.where(i == nrb - 1, 2, 1)),) + (0,) * (bias.ndim - 1)
    halo_prev = pl.BlockSpec((1, SUBLANES, 3 * SC_WIDTH), lambda bb, i: (bb, jnp.maximum(i * t8 - 1, 0), 0))
    halo_next = pl.BlockSpec((1, SUBLANES, 3 * SC_WIDTH), lambda bb, i: (bb, jnp.minimum((i + 1) * t8, nblk8 - 1), 0))
    return pl.pallas_call(
        functools.partial(_na_kernel, rows=rows),
        grid=(b, nrb),
        in_specs=[tok(w), seq(l), seq_t(l), seq(c), seq_t(c), pl.BlockSpec((1,) + bias.shape[1:], case),
                  tok(GLA_V), tok(3 * SC_WIDTH), halo_prev, halo_next, tok(d),
                  pl.BlockSpec((1, 1, d), lambda bb, i: (bb, 0, 0)),
                  full(wg), full(ws), full(wn), full(cw), full(cb)],
        out_specs=tok(d),
        out_shape=jax.ShapeDtypeStruct((b, l, d), F32),
        compiler_params=_cparams("arbitrary", "arbitrary"),
        name="na_out_proj",
    )(q, k, vt, kc, vct, bias, gla, scu, scu, scu, x, g1, wg, ws, wn, cw, cb)


def _ctx_attn_kernel(q_ref, k_ref, vt_ref, o_ref):
    k, vt = k_ref[0], vt_ref[0]

    def parts(j):
        sl = slice(j * LANES, (j + 1) * LANES)
        return [(k[:, sl], vt[sl, :], None)]

    o_ref[0] = _attend_heads(q_ref[0], parts, NA_HEADS).astype(o_ref.dtype)


def _ctx_attn_call(q, k, vt):
    b, c, w = q.shape
    spec = pl.BlockSpec((1, c, w), lambda bb: (bb, 0, 0))
    spec_t = pl.BlockSpec((1, w, c), lambda bb: (bb, 0, 0))
    return pl.pallas_call(
        _ctx_attn_kernel, grid=(b,), in_specs=[spec, spec, spec_t], out_specs=spec,
        out_shape=jax.ShapeDtypeStruct((b, c, w), BF16),
        compiler_params=_cparams("arbitrary"), name="ctx_attn",
    )(q, k, vt)


def _outproj_kernel(gla_ref, na_ref, scu_ref, scp_ref, scn_ref, x_ref, g1_ref,
                    wg_ref, ws_ref, wn_ref, cw_ref, cb_ref, o_ref):
    i = pl.program_id(1)
    o_ref[0] = _mix_residual(gla_ref[0], na_ref[0], scu_ref[0], scp_ref[0], scn_ref[0], x_ref[0], g1_ref[0],
                             wg_ref[...], ws_ref[...], wn_ref[...], cw_ref[...], cb_ref[...],
                             i == 0, i == pl.num_programs(1) - 1)


def _outproj_call(gla, na, scu, x, g1, wg, ws, wn, cw, cb, tm):
    b, l, d = x.shape
    bm = g1.shape[0]
    nblk8 = l // SUBLANES
    t8 = tm // SUBLANES
    mod_map = (lambda bb, i: (bb, 0, 0)) if bm > 1 else (lambda bb, i: (0, 0, 0))
    tok = lambda w: pl.BlockSpec((1, tm, w), lambda bb, i: (bb, i, 0))
    full = lambda a: pl.BlockSpec(a.shape, lambda bb, i: (0,) * a.ndim)
    halo_prev = pl.BlockSpec((1, SUBLANES, 3 * SC_WIDTH), lambda bb, i: (bb, jnp.maximum(i * t8 - 1, 0), 0))
    halo_next = pl.BlockSpec((1, SUBLANES, 3 * SC_WIDTH), lambda bb, i: (bb, jnp.minimum((i + 1) * t8, nblk8 - 1), 0))
    return pl.pallas_call(
        _outproj_kernel,
        grid=(b, l // tm),
        in_specs=[tok(GLA_V), tok(NA_W), tok(3 * SC_WIDTH), halo_prev, halo_next, tok(d),
                  pl.BlockSpec((1, 1, d), mod_map), full(wg), full(ws), full(wn), full(cw), full(cb)],
        out_specs=tok(d),
        out_shape=jax.ShapeDtypeStruct((b, l, d), F32),
        compiler_params=_cparams("arbitrary", "arbitrary"),
        name="out_proj",
    )(gla, na, scu, scu, scu, x, g1, wg, ws, wn, cw, cb)


def _ffn_kernel(*refs, final, chunk):
    if final:
        (x_ref, xp_ref, xn_ref, sh_ref, sc_ref, g2_ref, ng_ref, wup_ref, cw_ref, cb_ref, wdn_ref, fg_ref,
         o_ref, act_ref) = refs
    else:
        (x_ref, xp_ref, xn_ref, sh_ref, sc_ref, g2_ref, ng_ref, wup_ref, cw_ref, cb_ref, wdn_ref,
         o_ref, act_ref) = refs
    i = pl.program_id(1)
    last = pl.num_programs(1) - 1
    x = x_ref[0]
    tm = x.shape[0]
    hid = wdn_ref.shape[0]
    nm = lambda t: _norm_mod(t, ng_ref[...], sh_ref[0], sc_ref[0])
    hp = nm(xp_ref[0]) * jnp.where(i > 0, 1.0, 0.0)
    hn = nm(xn_ref[0]) * jnp.where(i < last, 1.0, 0.0)
    hb = jnp.concatenate([hp, nm(x), hn], axis=0).astype(BF16)
    ext = tm + 2 * SUBLANES
    inner = slice(SUBLANES, SUBLANES + tm)

    def conv(u, c0, w):
        cw = cw_ref[:, c0:c0 + w]
        return (pltpu.roll(u, 1, 0)[inner] * cw[0:1] + u[inner] * cw[1:2]
                + pltpu.roll(u, ext - 1, 0)[inner] * cw[2:3] + cb_ref[:, c0:c0 + w])

    for c0 in range(0, hid, chunk):
        w = min(chunk, hid - c0)
        ua = jnp.dot(hb, wup_ref[:, c0:c0 + w], preferred_element_type=F32)
        ub = jnp.dot(hb, wup_ref[:, hid + c0:hid + c0 + w], preferred_element_type=F32)
        act_ref[:, c0:c0 + w] = (_silu(conv(ua, c0, w)) * conv(ub, hid + c0, w)).astype(BF16)
    y = x + g2_ref[0] * jnp.dot(act_ref[...], wdn_ref[...], preferred_element_type=F32)
    if final:
        ms = jnp.mean(y * y, axis=-1, keepdims=True)
        y = y * lax.rsqrt(ms + EPS) * fg_ref[...]
    o_ref[0] = y


def _ffn_call(x, shift, scale, g2, gain, wup, cw, cb, wdn, final_gain, tm):
    b, l, d = x.shape
    bm = shift.shape[0]
    nblk8 = l // SUBLANES
    t8 = tm // SUBLANES
    final = final_gain is not None
    mod_map = (lambda bb, i: (bb, 0, 0)) if bm > 1 else (lambda bb, i: (0, 0, 0))
    tok = pl.BlockSpec((1, tm, d), lambda bb, i: (bb, i, 0))
    full = lambda a: pl.BlockSpec(a.shape, lambda bb, i: (0,) * a.ndim)
    resident = lambda a: pl.BlockSpec(a.shape, lambda bb, i: (0,) * a.ndim, pipeline_mode=pl.Buffered(1))
    halo_prev = pl.BlockSpec((1, SUBLANES, d), lambda bb, i: (bb, jnp.maximum(i * t8 - 1, 0), 0))
    halo_next = pl.BlockSpec((1, SUBLANES, d), lambda bb, i: (bb, jnp.minimum((i + 1) * t8, nblk8 - 1), 0))
    mod = pl.BlockSpec((1, 1, d), mod_map)
    in_specs = [tok, halo_prev, halo_next, mod, mod, mod, full(gain), resident(wup), full(cw), full(cb), resident(wdn)]
    args = [x, x, x, shift, scale, g2, gain, wup, cw, cb, wdn]
    if final:
        in_specs.append(full(final_gain))
        args.append(final_gain)
    return pl.pallas_call(
        functools.partial(_ffn_kernel, final=final, chunk=FFN_CHUNK),
        grid=(b, l // tm),
        in_specs=in_specs,
        out_specs=tok,
        out_shape=jax.ShapeDtypeStruct((b, l, d), F32),
        scratch_shapes=[pltpu.VMEM((tm, wdn.shape[0]), BF16)],
        compiler_params=_cparams("arbitrary", "arbitrary"),
        name="conv_ffn_final" if final else "conv_ffn",
    )(*args)


def _rope_tables(n_tokens):
    t = jnp.arange(n_tokens)
    n_freq = GLA_DK // 4
    inv_freq = ROPE_BASE ** (-jnp.arange(n_freq, dtype=F32) / n_freq)
    row = (t // GRID_W).astype(F32)[:, None] * inv_freq
    col = (t % GRID_W).astype(F32)[:, None] * inv_freq
    ang = jnp.concatenate([row, col], axis=-1)
    cos = jnp.repeat(jnp.cos(ang), 2, axis=-1)
    sin = jnp.repeat(jnp.sin(ang), 2, axis=-1) * jnp.tile(jnp.asarray([-1.0, 1.0], F32), GLA_DK // 2)
    pad = QK_PAD - GLA_QK
    cos = jnp.concatenate([jnp.tile(cos, (1, GLA_HEADS)), jnp.ones((n_tokens, pad), F32)], axis=-1)
    sin = jnp.concatenate([jnp.tile(sin, (1, GLA_HEADS)), jnp.zeros((n_tokens, pad), F32)], axis=-1)
    return cos, sin


def _tile_rows(n, target):
    t = min(n, target)
    assert n % t == 0 and t % GLA_GROUP == 0, (n, t)
    return t


def kernel(x, c, ctx, c_ctx, w_ada, b_ada, norm_mix_g, norm_ffn_g, w_in, gla_wg2_fw, gla_bg_fw, gla_wg2_bw, gla_bg_bw, gla_norm_g, sc_conv_w, sc_conv_b, na_rpb, w_out, ffn_w_up, ffn_conv_w, ffn_conv_b, ffn_w_down, final_norm_g):
    bsz, n_lat, d = x.shape
    n_ctx = ctx.shape[1]
    depth = w_in.shape[0]
    rows = n_lat // GRID_W
    assert n_lat % (NA_Q_ROWS * GRID_W) == 0 and rows >= NA_K_ROWS and bsz + 1 <= 2 * SUBLANES
    tm_lat, tm_ctx = _tile_rows(n_lat, LATENT_TILE), _tile_rows(n_ctx, LATENT_TILE)
    ts_lat = _tile_rows(n_lat, GLA_TILE)

    crows = jnp.zeros((2 * SUBLANES, d), F32).at[:bsz].set(c).at[bsz].set(c_ctx)
    mod_all = _ada_call(crows, w_ada, b_ada)

    zpad = lambda n: jnp.zeros((depth, d, n), BF16)
    o = np.cumsum([0, GLA_QK, GLA_QK, GLA_V, GLA_LOWRANK, GLA_LOWRANK, GLA_V,
                   SC_WIDTH, SC_WIDTH, SC_WIDTH, NA_W, NA_W, NA_W])
    w_in_b = w_in.astype(BF16)
    col = lambda n: w_in_b[:, :, o[n]:o[n + 1]]
    qpad = zpad(QK_PAD - GLA_QK)
    w_all = jnp.concatenate([col(0), qpad, col(1), qpad, col(2), col(5), col(3), col(4),
                             zpad(GLR_PAD - 2 * GLA_LOWRANK), w_in_b[:, :, o[6]:o[9]],
                             col(9) * jnp.asarray(NA_DH ** -0.5, BF16), w_in_b[:, :, o[10]:o[12]]], axis=-1)
    assert w_all.shape[-1] == INPROJ_WIDTH
    w_out_b = w_out.astype(BF16)
    wo_gla, wo_sc, wo_na = w_out_b[:, :GLA_V], w_out_b[:, GLA_V:GLA_V + SC_WIDTH], w_out_b[:, GLA_V + SC_WIDTH:]
    wup = ffn_w_up.astype(BF16)
    wdn = ffn_w_down.astype(BF16)

    def gate_w(w2, first_row):
        full = jnp.zeros((depth, GLR_PAD, QK_PAD), F32)
        return full.at[:, first_row:first_row + GLA_LOWRANK, :GLA_QK].set(w2).astype(BF16)

    def gate_b(bias):
        return jnp.zeros((depth, 1, QK_PAD), F32).at[:, 0, :GLA_QK].set(bias)

    wg_f, wg_b = gate_w(gla_wg2_fw, 0), gate_w(gla_wg2_bw, GLA_LOWRANK)
    bg_f, bg_b = gate_b(gla_bg_fw), gate_b(gla_bg_bw)
    gla_gain = jnp.tile(gla_norm_g, (1, GLA_HEADS))[:, None, :]
    hv = np.arange(GLA_V) // GLA_DV
    head_mean = jnp.asarray((hv[:, None] == hv[None, :]) / GLA_DV, BF16)
    tabs_f, tabs_b = _gla_tables(False), _gla_tables(True)
    rope_tabs = _rope_tables(n_lat)
    zero_state = jnp.zeros((bsz, QK_PAD // 2, GLA_V), F32)
    na_bias = _na_bias_tables(na_rpb, rows)

    xc = ctx
    for layer in range(depth):
        update_ctx = layer < depth - 1
        mod = mod_all[layer]
        sh1, sc1, g1, sh2, sc2, g2 = [mod[:bsz, n * d:(n + 1) * d][:, None, :] for n in range(6)]
        sh1c, sc1c, g1c, sh2c, sc2c, g2c = [mod[bsz:bsz + 1, n * d:(n + 1) * d][:, None, :] for n in range(6)]
        gain_mix = norm_mix_g[layer][None, :]
        gain_ffn = norm_ffn_g[layer][None, :]

        cq, ck, cv, cr, cglr, cscu, cnaq, cnak, cnavt = _inproj_call(
            xc, sh1c, sc1c, gain_mix, w_all[layer], None, tm_ctx)
        lq, lk, lv, lr, lglr, lscu, lnaq, lnak, lnavt = _inproj_call(
            x, sh1, sc1, gain_mix, w_all[layer], rope_tabs, tm_lat)

        fin = lambda r, o_f: (r, o_f, gla_gain[layer], head_mean)
        oc_f, state_f = _gla_call(cq, ck, cv, cglr, wg_f[layer], bg_f[layer], tabs_f, zero_state, tm_ctx, False)
        gla_ctx, state_b = _gla_call(cq, ck, cv, cglr, wg_b[layer], bg_b[layer], tabs_b, zero_state, tm_ctx, True,
                                     fin(cr, oc_f))
        ol_f, _ = _gla_call(lq, lk, lv, lglr, wg_f[layer], bg_f[layer], tabs_f, state_f, ts_lat, False)
        gla_lat, _ = _gla_call(lq, lk, lv, lglr, wg_b[layer], bg_b[layer], tabs_b, state_b, ts_lat, True,
                               fin(lr, ol_f))

        cw_sc, cb_sc = sc_conv_w[layer], sc_conv_b[layer][None, :]
        wo = (wo_gla[layer], wo_sc[layer], wo_na[layer])
        x = _na_call(lnaq, lnak, lnavt, cnak, cnavt, na_bias[layer], gla_lat, lscu, x, g1, *wo, cw_sc, cb_sc)
        ffn = (wup[layer], ffn_conv_w[layer], ffn_conv_b[layer][None, :], wdn[layer])
        x = _ffn_call(x, sh2, sc2, g2, gain_ffn, *ffn,
                      final_norm_g[None, :] if layer == depth - 1 else None, tm_lat)

        if update_ctx:
            na_ctx = _ctx_attn_call(cnaq, cnak, cnavt)
            xc = _outproj_call(gla_ctx, na_ctx, cscu, xc, g1c, *wo, cw_sc, cb_sc, tm_ctx)
            xc = _ffn_call(xc, sh2c, sc2c, g2c, gain_ffn, *ffn, None, tm_ctx)
    return x
```

```python
import functools

import numpy as np
import jax
import jax.numpy as jnp
from jax import lax
from jax.experimental import pallas as pl
from jax.experimental.pallas import tpu as pltpu

F32 = jnp.float32
BF16 = jnp.bfloat16

GRID_W = 64
EPS = 1e-6
NEG_INF = -1e30
GLA_HEADS = 6
GLA_DK = 32
GLA_DV = 64
GLA_LOWRANK = 16
GLA_TAU = 16.0
ROPE_BASE = 10000.0
SC_WIDTH = 256
NA_HEADS = 6
NA_DH = 64
NA_WIN_ROWS = 8
NA_WIN_COLS = 16
GLA_QK = GLA_HEADS * GLA_DK
GLA_V = GLA_HEADS * GLA_DV
NA_W = NA_HEADS * NA_DH

LANES = 128
SUBLANES = 8
VMEM_LIMIT_BYTES = 56 * 1024 * 1024

QK_PAD = 2 * LANES
GLR_PAD = LANES
GLA_GROUP = 128
GLA_BASE = 16
GLA_LEVELS = 4
FFN_CHUNK = 6 * LANES
LATENT_TILE = 1024
GLA_TILE = 2048
HEADS_A = LANES // GLA_DK
GLA_SINGLE_LEVEL_DECAY = 64.0
ADA_COLS = 12 * LANES
NA_Q_ROWS = 4
NA_K_ROWS = 12

NT_DIMS = (((1,), (1,)), ((), ()))


def _cparams(*sem):
    return pltpu.CompilerParams(dimension_semantics=sem, vmem_limit_bytes=VMEM_LIMIT_BYTES)


def _silu(t):
    return t / (1.0 + jnp.exp(-t))


def _norm_mod(t, gain, shift, scale):
    ms = jnp.mean(t * t, axis=-1, keepdims=True)
    return (t * lax.rsqrt(ms + EPS) * gain) * (1.0 + scale) + shift


def _ada_kernel(c_ref, w_ref, b_ref, o_ref):
    c = c_ref[...]
    o_ref[0] = jnp.dot(_silu(c), w_ref[0], preferred_element_type=F32,
                       precision=lax.Precision.HIGHEST) + b_ref[0]


def _ada_call(crows, w_ada, b_ada):
    depth, d, n = w_ada.shape
    rows = crows.shape[0]
    tn = ADA_COLS
    assert n % tn == 0
    return pl.pallas_call(
        _ada_kernel,
        grid=(depth, n // tn),
        in_specs=[pl.BlockSpec((rows, d), lambda l, j: (0, 0)),
                  pl.BlockSpec((1, d, tn), lambda l, j: (l, 0, j)),
                  pl.BlockSpec((1, 1, tn), lambda l, j: (l, 0, j))],
        out_specs=pl.BlockSpec((1, rows, tn), lambda l, j: (l, 0, j)),
        out_shape=jax.ShapeDtypeStruct((depth, rows, n), F32),
        compiler_params=_cparams("arbitrary", "arbitrary"),
        name="ada_mod",
    )(crows, w_ada, b_ada.reshape(depth, 1, n))


INPROJ_OUTS = ((QK_PAD, F32), (QK_PAD, F32), (GLA_V, F32), (GLA_V, F32), (GLR_PAD, F32),
               (3 * SC_WIDTH, F32), (NA_W, BF16), (NA_W, BF16), (NA_W, BF16))
INPROJ_STARTS = tuple(int(s) for s in np.cumsum([0] + [w for w, _ in INPROJ_OUTS]))
INPROJ_WIDTH = INPROJ_STARTS[-1]
MXU_COLS = 2 * LANES
NAV_OUT = len(INPROJ_OUTS) - 1


def _rotary(t, c, s):
    lane = lax.broadcasted_iota(jnp.int32, t.shape, 1)
    swapped = jnp.where(lane % 2 == 0, pltpu.roll(t, LANES - 1, 1), pltpu.roll(t, 1, 1))
    return t * c + swapped * s


def _inproj_kernel(*refs, rope):
    if rope:
        x_ref, sh_ref, sc_ref, g_ref, cos_ref, sin_ref, w_ref = refs[:7]
        outs = refs[7:]
    else:
        x_ref, sh_ref, sc_ref, g_ref, w_ref = refs[:5]
        outs = refs[5:]
    navt_ref = outs[-1]
    nav_pieces = []
    hb = _norm_mod(x_ref[0], g_ref[...], sh_ref[0], sc_ref[0]).astype(BF16)
    for n0 in range(0, INPROJ_WIDTH, MXU_COLS):
        res = jnp.dot(hb, w_ref[:, n0:n0 + MXU_COLS], preferred_element_type=F32)
        for a in range(n0, n0 + MXU_COLS, LANES):
            idx = max(i for i, s0 in enumerate(INPROJ_STARTS[:-1]) if s0 <= a)
            off = a - INPROJ_STARTS[idx]
            piece = res[:, a - n0:a - n0 + LANES]
            if idx < 2:
                if rope:
                    piece = _rotary(piece, cos_ref[:, off:off + LANES], sin_ref[:, off:off + LANES])
                if idx == 0:
                    piece = piece * (GLA_DK ** -0.5)
            if idx == NAV_OUT:
                nav_pieces.append(piece)
            else:
                outs[idx][0, :, off:off + LANES] = piece.astype(outs[idx].dtype)
    navt_ref[0] = jnp.concatenate(nav_pieces, axis=1).T.astype(BF16)


def _inproj_call(x, shift, scale, gain, w_all, rope_tabs, tm):
    b, l, d = x.shape
    bm = shift.shape[0]
    rope = rope_tabs is not None
    mod_map = (lambda i, bb: (bb, 0, 0)) if bm > 1 else (lambda i, bb: (0, 0, 0))
    tok = lambda w: pl.BlockSpec((1, tm, w), lambda i, bb: (bb, i, 0))
    full = lambda a: pl.BlockSpec(a.shape, lambda i, bb: (0,) * a.ndim)
    in_specs = [tok(d), pl.BlockSpec((1, 1, d), mod_map), pl.BlockSpec((1, 1, d), mod_map), full(gain)]
    args = [x, shift, scale, gain]
    if rope:
        in_specs += [pl.BlockSpec((tm, QK_PAD), lambda i, bb: (i, 0))] * 2
        args += list(rope_tabs)
    in_specs.append(full(w_all))
    args.append(w_all)
    return pl.pallas_call(
        functools.partial(_inproj_kernel, rope=rope),
        grid=(l // tm, b),
        in_specs=in_specs,
        out_specs=([tok(w) for w, _ in INPROJ_OUTS[:NAV_OUT]]
                   + [pl.BlockSpec((1, NA_W, tm), lambda i, bb: (bb, 0, i))]),
        out_shape=([jax.ShapeDtypeStruct((b, l, w), dt) for w, dt in INPROJ_OUTS[:NAV_OUT]]
                   + [jax.ShapeDtypeStruct((b, NA_W, l), BF16)]),
        compiler_params=_cparams("arbitrary", "arbitrary"),
        name="in_proj_rope" if rope else "in_proj_ctx",
    )(*args)


def _gla_tables(reverse):
    t = np.arange(GLA_GROUP)
    i, j = t[:, None], t[None, :]
    before = (j >= i) if reverse else (j <= i)
    strictly_after = (j < i) if reverse else (j > i)
    sizes = [GLA_BASE << level for level in range(GLA_LEVELS)]
    assert sizes[-1] == GLA_GROUP
    maps = []
    for n in sizes:
        maps.append((i // n == j // n) & before)
    for n in sizes:
        maps.append((i // n == j // n) & strictly_after)
    masks = [(i // GLA_BASE == j // GLA_BASE) & before]
    for n in sizes[1:]:
        h = n // 2
        q_late, k_early = (i % n) >= h, (j % n) < h
        if reverse:
            q_late, k_early = (i % n) < h, (j % n) >= h
        masks.append((i // n == j // n) & q_late & k_early)
    cm = np.concatenate(maps, axis=0).astype(np.float32)
    cm2 = np.concatenate([maps[GLA_LEVELS - 1], maps[-1]], axis=0).astype(np.float32)
    lm = np.stack(masks, axis=0).astype(np.float32)
    return jnp.asarray(cm, BF16), jnp.asarray(cm2, BF16), jnp.asarray(lm, F32)


def _gla_kernel(*refs, reverse, finalize, ngroups):
    if finalize:
        (q_ref, k_ref, v_ref, glr_ref, wg_ref, bg_ref, cm_ref, cm2_ref, lm_ref, s0_ref,
         r_ref, of_ref, ng_ref, bd_ref, out_ref, sfin_ref, st_ref, gate_ref, pre_ref) = refs
    else:
        (q_ref, k_ref, v_ref, glr_ref, wg_ref, bg_ref, cm_ref, cm2_ref, lm_ref, s0_ref,
         out_ref, sfin_ref, st_ref, gate_ref, pre_ref) = refs
    step = pl.program_id(1)
    g = GLA_GROUP
    kc = QK_PAD // 2
    vc = HEADS_A * GLA_DV

    @pl.when(step == 0)
    def _():
        st_ref[...] = s0_ref[0]

    z = jnp.dot(glr_ref[0].astype(BF16), wg_ref[...], preferred_element_type=F32) + bg_ref[...]
    gate_all = (jnp.minimum(z, 0.0) - jnp.log(1.0 + jnp.exp(-jnp.abs(z)))) * (1.0 / GLA_TAU)
    gate_ref[...] = gate_all
    total = jnp.sum(gate_all.reshape(ngroups, g, QK_PAD), axis=1)
    single_level = jnp.min(total) > -GLA_SINGLE_LEVEL_DECAY
    gcat = jnp.concatenate([gate_all[n * g:(n + 1) * g] for n in range(ngroups)], axis=1)
    gcat_hi = gcat.astype(BF16)
    gcat_lo = (gcat - gcat_hi.astype(F32)).astype(BF16)
    pcat = (jnp.dot(cm2_ref[0:g, :], gcat_hi, preferred_element_type=F32)
            + jnp.dot(cm2_ref[0:g, :], gcat_lo, preferred_element_type=F32))
    for n in range(ngroups):
        pre_ref[n * g:(n + 1) * g, :] = pcat[:, n * QK_PAD:(n + 1) * QK_PAD]

    k_head = lax.broadcasted_iota(jnp.int32, (g, kc), 1) // GLA_DK
    kmask = [k_head == h for h in range(HEADS_A)]
    va_head = lax.broadcasted_iota(jnp.int32, (g, vc), 1) // GLA_DV
    vb_head = lax.broadcasted_iota(jnp.int32, (g, GLA_V - vc), 1) // GLA_DV
    st_row = lax.broadcasted_iota(jnp.int32, (kc, GLA_V), 0) // GLA_DK
    st_col = lax.broadcasted_iota(jnp.int32, (kc, GLA_V), 1)
    state_mask = st_col // GLA_DV == st_row + jnp.where(st_col >= vc, HEADS_A, 0)
    lmask = [lm_ref[l] > 0.5 for l in range(GLA_LEVELS)]
    causal = functools.reduce(jnp.logical_or, lmask)

    def stack_heads(t, masks, n):
        return jnp.concatenate([jnp.where(masks[h], t, 0.0) for h in range(n)], axis=0).astype(BF16)

    def scores(qs, ks):
        sa = lax.dot_general(qs[:, :kc].astype(BF16), stack_heads(ks[:, :kc], kmask, HEADS_A), NT_DIMS,
                             preferred_element_type=F32)
        sb = lax.dot_general(qs[:, kc:].astype(BF16), stack_heads(ks[:, kc:], kmask, GLA_HEADS - HEADS_A),
                             NT_DIMS, preferred_element_type=F32)
        return jnp.concatenate([sa, sb], axis=1)

    def score_stage(gi, robust):
        gidx = (ngroups - 1 - gi) if reverse else gi
        rows = pl.ds(gidx * g, g)
        q = q_ref[0, rows, :]
        k = k_ref[0, rows, :]
        gate = gate_ref[rows, :]
        if robust:
            ghi = gate.astype(BF16)
            glo = (gate - ghi.astype(F32)).astype(BF16)
            cs = (jnp.dot(cm_ref[...], ghi, preferred_element_type=F32)
                  + jnp.dot(cm_ref[...], glo, preferred_element_type=F32))
            p16, p32, p64, p128, s16, s32, s64, s128 = [cs[n * g:(n + 1) * g] for n in range(8)]
            q0 = q * jnp.exp(p16)
            a = [scores(q0, k * jnp.exp(-p16)),
                 scores(q0, k * jnp.exp(s16)),
                 scores(q * jnp.exp(p32), k * jnp.exp(s32)),
                 scores(q * jnp.exp(p64), k * jnp.exp(s64))]
            q4 = q * jnp.exp(p128)
            total = p128[0:1] + s128[0:1]
        else:
            p128 = pre_ref[rows, :]
            total = jnp.sum(gate, axis=0, keepdims=True)
            s128 = total - p128
            q4 = q * jnp.exp(p128)
            a = scores(q4, k * jnp.exp(-p128))
        k4t = (k * jnp.exp(s128)).T.astype(BF16)
        return rows, a, q4.astype(BF16), k4t, total

    def value_stage(carry, robust):
        rows, a, q4b, k4t, total = carry
        v = v_ref[0, rows, :]
        if robust:
            blocks = []
            for h in range(GLA_HEADS):
                sl = slice(h * g, (h + 1) * g)
                blk = jnp.where(lmask[3], a[3][:, sl], 0.0)
                for l in (2, 1, 0):
                    blk = jnp.where(lmask[l], a[l][:, sl], blk)
                blocks.append(blk)
        else:
            blocks = [jnp.where(causal, a[:, h * g:(h + 1) * g], 0.0) for h in range(GLA_HEADS)]
        amat_a = jnp.concatenate(blocks[:HEADS_A], axis=1).astype(BF16)
        amat_b = jnp.concatenate(blocks[HEADS_A:], axis=1).astype(BF16)
        v_a = stack_heads(v[:, :vc], [va_head == h for h in range(HEADS_A)], HEADS_A)
        v_b = stack_heads(v[:, vc:], [vb_head == h for h in range(GLA_HEADS - HEADS_A)], GLA_HEADS - HEADS_A)
        st = st_ref[...]
        stb = st.astype(BF16)
        o_a = (jnp.dot(amat_a, v_a, preferred_element_type=F32)
               + jnp.dot(q4b[:, :kc], stb[:, :vc], preferred_element_type=F32))
        o_b = (jnp.dot(amat_b, v_b, preferred_element_type=F32)
               + jnp.dot(q4b[:, kc:], stb[:, vc:], preferred_element_type=F32))
        o = jnp.concatenate([o_a, o_b], axis=1)
        vb = v.astype(BF16)
        upd = jnp.concatenate([jnp.dot(k4t[:kc], vb[:, :vc], preferred_element_type=F32),
                               jnp.dot(k4t[kc:], vb[:, vc:], preferred_element_type=F32)], axis=1)
        dcol = jnp.exp(jnp.broadcast_to(total, (g, 2 * kc)).T[:, 0:1])
        dec = jnp.concatenate([jnp.broadcast_to(dcol[:kc], (kc, vc)),
                               jnp.broadcast_to(dcol[kc:], (kc, GLA_V - vc))], axis=1)
        st_ref[...] = st * dec + jnp.where(state_mask, upd, 0.0)
        if finalize:
            ot = o + of_ref[0, rows, :]
            ms = jnp.dot((ot * ot).astype(BF16), bd_ref[...], preferred_element_type=F32)
            y = ot * lax.rsqrt(ms + EPS) * ng_ref[...] * _silu(r_ref[0, rows, :])
            out_ref[0, rows, :] = y.astype(out_ref.dtype)
        else:
            out_ref[0, rows, :] = o

    def run(robust):
        pending = None
        for gi in range(ngroups + 1):
            upcoming = score_stage(gi, robust) if gi < ngroups else None
            if pending is not None:
                value_stage(pending, robust)
            pending = upcoming

    @pl.when(single_level)
    def _():
        run(False)

    @pl.when(jnp.logical_not(single_level))
    def _():
        run(True)

    @pl.when(step == pl.num_programs(1) - 1)
    def _():
        sfin_ref[0] = st_ref[...]


def _gla_call(q, k, v, glr, wg, bg, tabs, s0, ts, reverse, fin=None):
    b, l, _ = q.shape
    nb = l // ts
    finalize = fin is not None
    cm, cm2, lm = tabs
    blk = (lambda bb, i: (bb, nb - 1 - i, 0)) if reverse else (lambda bb, i: (bb, i, 0))
    tok = lambda w: pl.BlockSpec((1, ts, w), blk)
    full = lambda a: pl.BlockSpec(a.shape, lambda bb, i: (0,) * a.ndim)
    state = pl.BlockSpec((1, QK_PAD // 2, GLA_V), lambda bb, i: (bb, 0, 0))
    in_specs = [tok(QK_PAD), tok(QK_PAD), tok(GLA_V), tok(GLR_PAD), full(wg), full(bg), full(cm), full(cm2),
                full(lm), state]
    args = [q, k, v, glr, wg, bg, cm, cm2, lm, s0]
    if finalize:
        r, o_other, ng, bd = fin
        in_specs += [tok(GLA_V), tok(GLA_V), full(ng), full(bd)]
        args += [r, o_other, ng, bd]
    out_dt = BF16 if finalize else F32
    return pl.pallas_call(
        functools.partial(_gla_kernel, reverse=reverse, finalize=finalize, ngroups=ts // GLA_GROUP),
        grid=(b, nb),
        in_specs=in_specs,
        out_specs=[tok(GLA_V), state],
        out_shape=[jax.ShapeDtypeStruct((b, l, GLA_V), out_dt),
                   jax.ShapeDtypeStruct((b, QK_PAD // 2, GLA_V), F32)],
        scratch_shapes=[pltpu.VMEM((QK_PAD // 2, GLA_V), F32), pltpu.VMEM((ts, QK_PAD), F32),
                        pltpu.VMEM((ts, QK_PAD), F32)],
        compiler_params=_cparams("arbitrary", "arbitrary"),
        name="gla_bwd" if reverse else "gla_fwd",
    )(*args)


def _na_bias_tables(rpb, rows):
    kh = min(NA_WIN_ROWS, rows)
    n_dr, n_dc = 2 * NA_WIN_ROWS - 1, 2 * NA_WIN_COLS - 1
    qr, kr = np.arange(NA_Q_ROWS)[:, None], np.arange(NA_K_ROWS)[None, :]
    row_pick, row_ok = [], []
    for r0, ks in ((0, 0), (NA_WIN_ROWS // 2, 0), (rows - NA_Q_ROWS, rows - NA_K_ROWS)):
        r, r2 = r0 + qr, ks + kr
        rs = np.clip(r - kh // 2, 0, rows - kh)
        row_ok.append((r2 >= rs) & (r2 < rs + kh))
        row_pick.append(np.eye(n_dr, dtype=np.float32)[np.clip(r2 - r + NA_WIN_ROWS - 1, 0, n_dr - 1)])
    row_pick, row_ok = np.stack(row_pick), np.stack(row_ok)
    qc, kc = np.arange(GRID_W)[:, None], np.arange(GRID_W)[None, :]
    cs = np.clip(qc - NA_WIN_COLS // 2, 0, GRID_W - NA_WIN_COLS)
    col_ok = (kc >= cs) & (kc < cs + NA_WIN_COLS)
    col_pick = np.eye(n_dc, dtype=np.float32)[np.clip(kc - qc + NA_WIN_COLS - 1, 0, n_dc - 1)]
    hi = lax.Precision.HIGHEST
    depth = rpb.shape[0]
    pairs = NA_Q_ROWS // 2
    by_row = jnp.einsum('lhde,ckqd->lchkqe', rpb.astype(F32), np.transpose(row_pick, (0, 2, 1, 3)), precision=hi)
    by_pair = by_row.reshape(depth, 3, NA_HEADS, NA_K_ROWS, pairs, 2 * n_dc)
    pick2 = np.zeros((2, n_dc, GRID_W, 2, GRID_W), np.float32)
    for q2 in range(2):
        pick2[q2, :, :, q2, :] = np.transpose(col_pick, (2, 1, 0))
    pick2 = pick2.reshape(2 * n_dc, GRID_W, 2 * GRID_W)
    bias = jnp.einsum('lchkjf,fym->lchkjym', by_pair, pick2, precision=hi)
    valid = (np.transpose(row_ok, (0, 2, 1)).reshape(3, NA_K_ROWS, pairs, 1, 2, 1)
             & col_ok.T[None, None, None, :, None, :])
    valid = valid.reshape(3, 1, NA_K_ROWS, pairs, GRID_W, 2 * GRID_W)
    return jnp.where(valid[None], bias, NEG_INF)


def _attend_heads(q, parts_of_pair, n_heads):
    lane = lax.broadcasted_iota(jnp.int32, (1, LANES), 1)
    half = [(lane < NA_DH).astype(BF16), (lane >= NA_DH).astype(BF16)]

    def scores(h):
        j, o = divmod(h, 2)
        qm = q[:, j * LANES:(j + 1) * LANES] * half[o]
        out = []
        for kp, _, bias in parts_of_pair(j):
            st = lax.dot_general(kp, qm, NT_DIMS, preferred_element_type=F32)
            if bias is not None:
                n_kr, n_jq = st.shape[0] // GRID_W, st.shape[1] // LANES
                st = jnp.concatenate([jnp.concatenate(
                    [st[kr * GRID_W:(kr + 1) * GRID_W, jq * LANES:(jq + 1) * LANES] + bias(o, kr, jq)
                     for jq in range(n_jq)], axis=1) for kr in range(n_kr)], axis=0)
            out.append(st)
        return out

    def softmax(s):
        m = functools.reduce(jnp.maximum, [jnp.max(t, axis=0, keepdims=True) for t in s])
        p = [jnp.exp(t - m) for t in s]
        denom = functools.reduce(jnp.add, [jnp.sum(t, axis=0, keepdims=True) for t in p])
        return [t.astype(BF16) for t in p], denom

    def values(h, p, denom):
        j, o = divmod(h, 2)
        acc = functools.reduce(jnp.add, [
            jnp.dot(vt[o * NA_DH:(o + 1) * NA_DH, :], pt, preferred_element_type=F32)
            for pt, (_, vt, _) in zip(p, parts_of_pair(j))])
        return acc / denom

    s, p, outs = {}, {}, {}
    for t in range(n_heads + 3):
        if t < n_heads:
            s[t] = scores(t)
        if 0 <= t - 1 < n_heads:
            p[t - 1] = softmax(s.pop(t - 1))
        if 0 <= t - 3 < n_heads:
            outs[t - 3] = values(t - 3, *p.pop(t - 3))
    return jnp.concatenate([outs[h] for h in range(n_heads)], axis=0).T


def _mix_residual(gla, na, u, u_prev, u_next, x, g1, wg, ws, wn, cw, cb, first, last):
    w = SC_WIDTH
    tm = u.shape[0]
    cx = u[:, w:2 * w] * u[:, 2 * w:3 * w]
    cx_prev = (u_prev[SUBLANES - 1:SUBLANES, w:2 * w] * u_prev[SUBLANES - 1:SUBLANES, 2 * w:3 * w]
               * jnp.where(first, 0.0, 1.0))
    cx_next = u_next[0:1, w:2 * w] * u_next[0:1, 2 * w:3 * w] * jnp.where(last, 0.0, 1.0)
    row = lax.broadcasted_iota(jnp.int32, cx.shape, 0)
    prev = jnp.where(row == 0, cx_prev, pltpu.roll(cx, 1, 0))
    nxt = jnp.where(row == tm - 1, cx_next, pltpu.roll(cx, tm - 1, 0))
    sc = u[:, 0:w] * (prev * cw[0:1] + cx * cw[1:2] + nxt * cw[2:3] + cb)
    mix = (jnp.dot(gla, wg, preferred_element_type=F32)
           + jnp.dot(sc.astype(BF16), ws, preferred_element_type=F32)
           + jnp.dot(na, wn, preferred_element_type=F32))
    return x + g1 * mix


def _na_kernel(q_ref, k_ref, vt_ref, kc_ref, vct_ref, bias_ref, gla_ref, scu_ref, scp_ref, scn_ref, x_ref, g1_ref,
               wg_ref, ws_ref, wn_ref, cw_ref, cb_ref, o_ref, *, rows):
    i = pl.program_id(1)
    ks = jnp.clip(i * NA_Q_ROWS - NA_WIN_ROWS // 2, 0, rows - NA_K_ROWS)
    win = pl.ds(pl.multiple_of(ks * GRID_W, NA_Q_ROWS * GRID_W), NA_K_ROWS * GRID_W)
    kw = k_ref[0, win, :]
    vtw = vt_ref[0, :, win]
    kc = kc_ref[0]
    vct = vct_ref[0]

    def parts(j):
        sl = slice(j * LANES, (j + 1) * LANES)
        return [(kw[:, sl], vtw[sl, :], lambda o, kr, jq: bias_ref[0, 2 * j + o, kr, jq]),
                (kc[:, sl], vct[sl, :], None)]

    na = _attend_heads(q_ref[0], parts, NA_HEADS).astype(BF16)
    o_ref[0] = _mix_residual(gla_ref[0], na, scu_ref[0], scp_ref[0], scn_ref[0], x_ref[0], g1_ref[0],
                             wg_ref[...], ws_ref[...], wn_ref[...], cw_ref[...], cb_ref[...],
                             i == 0, i == pl.num_programs(1) - 1)


def _na_call(q, k, vt, kc, vct, bias, gla, scu, x, g1, wg, ws, wn, cw, cb):
    b, l, w = q.shape
    d = x.shape[-1]
    c = kc.shape[1]
    rows = l // GRID_W
    nrb = rows // NA_Q_ROWS
    nq = NA_Q_ROWS * GRID_W
    nblk8, t8 = l // SUBLANES, nq // SUBLANES
    seq = lambda n: pl.BlockSpec((1, n, w), lambda bb, i: (bb, 0, 0))
    seq_t = lambda n: pl.BlockSpec((1, w, n), lambda bb, i: (bb, 0, 0))
    tok = lambda width: pl.BlockSpec((1, nq, width), lambda bb, i: (bb, i, 0))
    full = lambda a: pl.BlockSpec(a.shape, lambda bb, i: (0,) * a.ndim)
    case = lambda bb, i: (jnp.where(i == 0, 0, jnp.where(i == nrb - 1, 2, 1)),) + (0,) * (bias.ndim - 1)
    halo_prev = pl.BlockSpec((1, SUBLANES, 3 * SC_WIDTH), lambda bb, i: (bb, jnp.maximum(i * t8 - 1, 0), 0))
    halo_next = pl.BlockSpec((1, SUBLANES, 3 * SC_WIDTH), lambda bb, i: (bb, jnp.minimum((i + 1) * t8, nblk8 - 1), 0))
    return pl.pallas_call(
        functools.partial(_na_kernel, rows=rows),
        grid=(b, nrb),
        in_specs=[tok(w), seq(l), seq_t(l), seq(c), seq_t(c), pl.BlockSpec((1,) + bias.shape[1:], case),
                  tok(GLA_V), tok(3 * SC_WIDTH), halo_prev, halo_next, tok(d),
                  pl.BlockSpec((1, 1, d), lambda bb, i: (bb, 0, 0)),
                  full(wg), full(ws), full(wn), full(cw), full(cb)],
        out_specs=tok(d),
        out_shape=jax.ShapeDtypeStruct((b, l, d), F32),
        compiler_params=_cparams("arbitrary", "arbitrary"),
        name="na_out_proj",
    )(q, k, vt, kc, vct, bias, gla, scu, scu, scu, x, g1, wg, ws, wn, cw, cb)


def _ctx_attn_kernel(q_ref, k_ref, vt_ref, o_ref):
    k, vt = k_ref[0], vt_ref[0]

    def parts(j):
        sl = slice(j * LANES, (j + 1) * LANES)
        return [(k[:, sl], vt[sl, :], None)]

    o_ref[0] = _attend_heads(q_ref[0], parts, NA_HEADS).astype(o_ref.dtype)


def _ctx_attn_call(q, k, vt):
    b, c, w = q.shape
    spec = pl.BlockSpec((1, c, w), lambda bb: (bb, 0, 0))
    spec_t = pl.BlockSpec((1, w, c), lambda bb: (bb, 0, 0))
    return pl.pallas_call(
        _ctx_attn_kernel, grid=(b,), in_specs=[spec, spec, spec_t], out_specs=spec,
        out_shape=jax.ShapeDtypeStruct((b, c, w), BF16),
        compiler_params=_cparams("arbitrary"), name="ctx_attn",
    )(q, k, vt)


def _outproj_kernel(gla_ref, na_ref, scu_ref, scp_ref, scn_ref, x_ref, g1_ref,
                    wg_ref, ws_ref, wn_ref, cw_ref, cb_ref, o_ref):
    i = pl.program_id(1)
    o_ref[0] = _mix_residual(gla_ref[0], na_ref[0], scu_ref[0], scp_ref[0], scn_ref[0], x_ref[0], g1_ref[0],
                             wg_ref[...], ws_ref[...], wn_ref[...], cw_ref[...], cb_ref[...],
                             i == 0, i == pl.num_programs(1) - 1)


def _outproj_call(gla, na, scu, x, g1, wg, ws, wn, cw, cb, tm):
    b, l, d = x.shape
    bm = g1.shape[0]
    nblk8 = l // SUBLANES
    t8 = tm // SUBLANES
    mod_map = (lambda bb, i: (bb, 0, 0)) if bm > 1 else (lambda bb, i: (0, 0, 0))
    tok = lambda w: pl.BlockSpec((1, tm, w), lambda bb, i: (bb, i, 0))
    full = lambda a: pl.BlockSpec(a.shape, lambda bb, i: (0,) * a.ndim)
    halo_prev = pl.BlockSpec((1, SUBLANES, 3 * SC_WIDTH), lambda bb, i: (bb, jnp.maximum(i * t8 - 1, 0), 0))
    halo_next = pl.BlockSpec((1, SUBLANES, 3 * SC_WIDTH), lambda bb, i: (bb, jnp.minimum((i + 1) * t8, nblk8 - 1), 0))
    return pl.pallas_call(
        _outproj_kernel,
        grid=(b, l // tm),
        in_specs=[tok(GLA_V), tok(NA_W), tok(3 * SC_WIDTH), halo_prev, halo_next, tok(d),
                  pl.BlockSpec((1, 1, d), mod_map), full(wg), full(ws), full(wn), full(cw), full(cb)],
        out_specs=tok(d),
        out_shape=jax.ShapeDtypeStruct((b, l, d), F32),
        compiler_params=_cparams("arbitrary", "arbitrary"),
        name="out_proj",
    )(gla, na, scu, scu, scu, x, g1, wg, ws, wn, cw, cb)


def _ffn_kernel(*refs, final, chunk):
    if final:
        (x_ref, xp_ref, xn_ref, sh_ref, sc_ref, g2_ref, ng_ref, wup_ref, cw_ref, cb_ref, wdn_ref, fg_ref,
         o_ref, act_ref) = refs
    else:
        (x_ref, xp_ref, xn_ref, sh_ref, sc_ref, g2_ref, ng_ref, wup_ref, cw_ref, cb_ref, wdn_ref,
         o_ref, act_ref) = refs
    i = pl.program_id(1)
    last = pl.num_programs(1) - 1
    x = x_ref[0]
    tm = x.shape[0]
    hid = wdn_ref.shape[0]
    nm = lambda t: _norm_mod(t, ng_ref[...], sh_ref[0], sc_ref[0])
    hp = nm(xp_ref[0]) * jnp.where(i > 0, 1.0, 0.0)
    hn = nm(xn_ref[0]) * jnp.where(i < last, 1.0, 0.0)
    hb = jnp.concatenate([hp, nm(x), hn], axis=0).astype(BF16)
    ext = tm + 2 * SUBLANES
    inner = slice(SUBLANES, SUBLANES + tm)

    def conv(u, c0, w):
        cw = cw_ref[:, c0:c0 + w]
        return (pltpu.roll(u, 1, 0)[inner] * cw[0:1] + u[inner] * cw[1:2]
                + pltpu.roll(u, ext - 1, 0)[inner] * cw[2:3] + cb_ref[:, c0:c0 + w])

    for c0 in range(0, hid, chunk):
        w = min(chunk, hid - c0)
        ua = jnp.dot(hb, wup_ref[:, c0:c0 + w], preferred_element_type=F32)
        ub = jnp.dot(hb, wup_ref[:, hid + c0:hid + c0 + w], preferred_element_type=F32)
        act_ref[:, c0:c0 + w] = (_silu(conv(ua, c0, w)) * conv(ub, hid + c0, w)).astype(BF16)
    y = x + g2_ref[0] * jnp.dot(act_ref[...], wdn_ref[...], preferred_element_type=F32)
    if final:
        ms = jnp.mean(y * y, axis=-1, keepdims=True)
        y = y * lax.rsqrt(ms + EPS) * fg_ref[...]
    o_ref[0] = y


def _ffn_call(x, shift, scale, g2, gain, wup, cw, cb, wdn, final_gain, tm):
    b, l, d = x.shape
    bm = shift.shape[0]
    nblk8 = l // SUBLANES
    t8 = tm // SUBLANES
    final = final_gain is not None
    mod_map = (lambda bb, i: (bb, 0, 0)) if bm > 1 else (lambda bb, i: (0, 0, 0))
    tok = pl.BlockSpec((1, tm, d), lambda bb, i: (bb, i, 0))
    full = lambda a: pl.BlockSpec(a.shape, lambda bb, i: (0,) * a.ndim)
    resident = lambda a: pl.BlockSpec(a.shape, lambda bb, i: (0,) * a.ndim, pipeline_mode=pl.Buffered(1))
    halo_prev = pl.BlockSpec((1, SUBLANES, d), lambda bb, i: (bb, jnp.maximum(i * t8 - 1, 0), 0))
    halo_next = pl.BlockSpec((1, SUBLANES, d), lambda bb, i: (bb, jnp.minimum((i + 1) * t8, nblk8 - 1), 0))
    mod = pl.BlockSpec((1, 1, d), mod_map)
    in_specs = [tok, halo_prev, halo_next, mod, mod, mod, full(gain), resident(wup), full(cw), full(cb), resident(wdn)]
    args = [x, x, x, shift, scale, g2, gain, wup, cw, cb, wdn]
    if final:
        in_specs.append(full(final_gain))
        args.append(final_gain)
    return pl.pallas_call(
        functools.partial(_ffn_kernel, final=final, chunk=FFN_CHUNK),
        grid=(b, l // tm),
        in_specs=in_specs,
        out_specs=tok,
        out_shape=jax.ShapeDtypeStruct((b, l, d), F32),
        scratch_shapes=[pltpu.VMEM((tm, wdn.shape[0]), BF16)],
        compiler_params=_cparams("arbitrary", "arbitrary"),
        name="conv_ffn_final" if final else "conv_ffn",
    )(*args)


def _rope_tables(n_tokens):
    t = jnp.arange(n_tokens)
    n_freq = GLA_DK // 4
    inv_freq = ROPE_BASE ** (-jnp.arange(n_freq, dtype=F32) / n_freq)
    row = (t // GRID_W).astype(F32)[:, None] * inv_freq
    col = (t % GRID_W).astype(F32)[:, None] * inv_freq
    ang = jnp.concatenate([row, col], axis=-1)
    cos = jnp.repeat(jnp.cos(ang), 2, axis=-1)
    sin = jnp.repeat(jnp.sin(ang), 2, axis=-1) * jnp.tile(jnp.asarray([-1.0, 1.0], F32), GLA_DK // 2)
    pad = QK_PAD - GLA_QK
    cos = jnp.concatenate([jnp.tile(cos, (1, GLA_HEADS)), jnp.ones((n_tokens, pad), F32)], axis=-1)
    sin = jnp.concatenate([jnp.tile(sin, (1, GLA_HEADS)), jnp.zeros((n_tokens, pad), F32)], axis=-1)
    return cos, sin


def _tile_rows(n, target):
    t = min(n, target)
    assert n % t == 0 and t % GLA_GROUP == 0, (n, t)
    return t


def kernel(x, c, ctx, c_ctx, w_ada, b_ada, norm_mix_g, norm_ffn_g, w_in, gla_wg2_fw, gla_bg_fw, gla_wg2_bw, gla_bg_bw, gla_norm_g, sc_conv_w, sc_conv_b, na_rpb, w_out, ffn_w_up, ffn_conv_w, ffn_conv_b, ffn_w_down, final_norm_g):
    bsz, n_lat, d = x.shape
    n_ctx = ctx.shape[1]
    depth = w_in.shape[0]
    rows = n_lat // GRID_W
    assert n_lat % (NA_Q_ROWS * GRID_W) == 0 and rows >= NA_K_ROWS and bsz + 1 <= 2 * SUBLANES
    tm_lat, tm_ctx = _tile_rows(n_lat, LATENT_TILE), _tile_rows(n_ctx, LATENT_TILE)
    ts_lat = _tile_rows(n_lat, GLA_TILE)

    crows = jnp.zeros((2 * SUBLANES, d), F32).at[:bsz].set(c).at[bsz].set(c_ctx)
    mod_all = _ada_call(crows, w_ada, b_ada)

    zpad = lambda n: jnp.zeros((depth, d, n), BF16)
    o = np.cumsum([0, GLA_QK, GLA_QK, GLA_V, GLA_LOWRANK, GLA_LOWRANK, GLA_V,
                   SC_WIDTH, SC_WIDTH, SC_WIDTH, NA_W, NA_W, NA_W])
    w_in_b = w_in.astype(BF16)
    col = lambda n: w_in_b[:, :, o[n]:o[n + 1]]
    qpad = zpad(QK_PAD - GLA_QK)
    w_all = jnp.concatenate([col(0), qpad, col(1), qpad, col(2), col(5), col(3), col(4),
                             zpad(GLR_PAD - 2 * GLA_LOWRANK), w_in_b[:, :, o[6]:o[9]],
                             col(9) * jnp.asarray(NA_DH ** -0.5, BF16), w_in_b[:, :, o[10]:o[12]]], axis=-1)
    assert w_all.shape[-1] == INPROJ_WIDTH
    w_out_b = w_out.astype(BF16)
    wo_gla, wo_sc, wo_na = w_out_b[:, :GLA_V], w_out_b[:, GLA_V:GLA_V + SC_WIDTH], w_out_b[:, GLA_V + SC_WIDTH:]
    wup = ffn_w_up.astype(BF16)
    wdn = ffn_w_down.astype(BF16)

    def gate_w(w2, first_row):
        full = jnp.zeros((depth, GLR_PAD, QK_PAD), F32)
        return full.at[:, first_row:first_row + GLA_LOWRANK, :GLA_QK].set(w2).astype(BF16)

    def gate_b(bias):
        return jnp.zeros((depth, 1, QK_PAD), F32).at[:, 0, :GLA_QK].set(bias)

    wg_f, wg_b = gate_w(gla_wg2_fw, 0), gate_w(gla_wg2_bw, GLA_LOWRANK)
    bg_f, bg_b = gate_b(gla_bg_fw), gate_b(gla_bg_bw)
    gla_gain = jnp.tile(gla_norm_g, (1, GLA_HEADS))[:, None, :]
    hv = np.arange(GLA_V) // GLA_DV
    head_mean = jnp.asarray((hv[:, None] == hv[None, :]) / GLA_DV, BF16)
    tabs_f, tabs_b = _gla_tables(False), _gla_tables(True)
    rope_tabs = _rope_tables(n_lat)
    zero_state = jnp.zeros((bsz, QK_PAD // 2, GLA_V), F32)
    na_bias = _na_bias_tables(na_rpb, rows)

    xc = ctx
    for layer in range(depth):
        update_ctx = layer < depth - 1
        mod = mod_all[layer]
        sh1, sc1, g1, sh2, sc2, g2 = [mod[:bsz, n * d:(n + 1) * d][:, None, :] for n in range(6)]
        sh1c, sc1c, g1c, sh2c, sc2c, g2c = [mod[bsz:bsz + 1, n * d:(n + 1) * d][:, None, :] for n in range(6)]
        gain_mix = norm_mix_g[layer][None, :]
        gain_ffn = norm_ffn_g[layer][None, :]

        cq, ck, cv, cr, cglr, cscu, cnaq, cnak, cnavt = _inproj_call(
            xc, sh1c, sc1c, gain_mix, w_all[layer], None, tm_ctx)
        lq, lk, lv, lr, lglr, lscu, lnaq, lnak, lnavt = _inproj_call(
            x, sh1, sc1, gain_mix, w_all[layer], rope_tabs, tm_lat)

        fin = lambda r, o_f: (r, o_f, gla_gain[layer], head_mean)
        oc_f, state_f = _gla_call(cq, ck, cv, cglr, wg_f[layer], bg_f[layer], tabs_f, zero_state, tm_ctx, False)
        gla_ctx, state_b = _gla_call(cq, ck, cv, cglr, wg_b[layer], bg_b[layer], tabs_b, zero_state, tm_ctx, True,
                                     fin(cr, oc_f))
        ol_f, _ = _gla_call(lq, lk, lv, lglr, wg_f[layer], bg_f[layer], tabs_f, state_f, ts_lat, False)
        gla_lat, _ = _gla_call(lq, lk, lv, lglr, wg_b[layer], bg_b[layer], tabs_b, state_b, ts_lat, True,
                               fin(lr, ol_f))

        cw_sc, cb_sc = sc_conv_w[layer], sc_conv_b[layer][None, :]
        wo = (wo_gla[layer], wo_sc[layer], wo_na[layer])
        x = _na_call(lnaq, lnak, lnavt, cnak, cnavt, na_bias[layer], gla_lat, lscu, x, g1, *wo, cw_sc, cb_sc)
        ffn = (wup[layer], ffn_conv_w[layer], ffn_conv_b[layer][None, :], wdn[layer])
        x = _ffn_call(x, sh2, sc2, g2, gain_ffn, *ffn,
                      final_norm_g[None, :] if layer == depth - 1 else None, tm_lat)

        if update_ctx:
            na_ctx = _ctx_attn_call(cnaq, cnak, cnavt)
            xc = _outproj_call(gla_ctx, na_ctx, cscu, xc, g1c, *wo, cw_sc, cb_sc, tm_ctx)
            xc = _ffn_call(xc, sh2c, sc2c, g2c, gain_ffn, *ffn, None, tm_ctx)
    return x
```
